```python
import math
import jax, jax.numpy as jnp
from jax import lax
import numpy as np

D_MODEL = 1024
BATCH = 2
SEQ = 8192
DEPTH = 2
DEC_BATCH = 128
DEC_SEQ = 4
PAST_LEN = 16384
PAGE_SIZE = 128

N_HEADS = 16
HEAD_DIM = 64
N_KV_HEADS = 4
Q_PER_KV = N_HEADS // N_KV_HEADS
WINDOW = 128
ROPE_THETA = 10000.0
ATTN_DIM = N_HEADS * HEAD_DIM
QKV_DIM = (N_HEADS + 2 * N_KV_HEADS) * HEAD_DIM
HG_EXPAND = 128
HG_HEADS = D_MODEL // HG_EXPAND
HG_DK = HG_EXPAND
HG_DV = D_MODEL // HG_HEADS
HG_KEY_DIM = HG_HEADS * HG_DK
HG_VAL_DIM = HG_HEADS * HG_DV
HG_IN_DIM = 2 * HG_KEY_DIM + 2 * HG_VAL_DIM
HG_CHUNK = 64
D_FF = 3584
N_EXPERTS = 8
TOP_K = 2
MOE_D_FF = 3584
NORM_EPS = 1e-5
N_ATTN = (DEPTH + 1) // 2
N_HGRN = DEPTH // 2
N_DENSE = (DEPTH + 1) // 2
N_MOE = DEPTH // 2

kernel_name = "swa_sink_hgrn2_hybrid_step"


def rms_norm(x, g):
    xf = x.astype(jnp.float32)
    y = xf * lax.rsqrt(jnp.mean(xf * xf, axis=-1, keepdims=True) + NORM_EPS)
    return (y * g.astype(jnp.float32)).astype(x.dtype)


def rope(x, pos):
    half = HEAD_DIM // 2
    inv = ROPE_THETA ** (-jnp.arange(half, dtype=jnp.float32) / half)
    ang = pos.astype(jnp.float32)[:, None] * inv[None, :]
    cos = jnp.cos(ang)[:, None, :]
    sin = jnp.sin(ang)[:, None, :]
    xf = x.astype(jnp.float32)
    x1, x2 = xf[..., :half], xf[..., half:]
    return jnp.concatenate([x1 * cos - x2 * sin, x2 * cos + x1 * sin], axis=-1).astype(x.dtype)


def attn_project(h, w_qkv, b_qkv, pos):
    B, T, _ = h.shape
    qkv = h @ w_qkv + b_qkv
    q, k, v = jnp.split(qkv, [ATTN_DIM, ATTN_DIM + N_KV_HEADS * HEAD_DIM], axis=-1)
    q = rope(q.reshape(B, T, N_HEADS, HEAD_DIM), pos)
    k = rope(k.reshape(B, T, N_KV_HEADS, HEAD_DIM), pos)
    v = v.reshape(B, T, N_KV_HEADS, HEAD_DIM)
    return q, k, v


def sink_probs(s, mask, sink):
    s = jnp.where(mask, s, -jnp.inf)
    m = jnp.maximum(jnp.max(s, axis=-1, keepdims=True), sink)
    p = jnp.exp(s - m)
    return p / (jnp.sum(p, axis=-1, keepdims=True) + jnp.exp(sink - m))


def swa_prompt(q, k, v, sinks):
    B, T = q.shape[:2]
    nb = T // WINDOW
    qb = q.reshape(B, nb, WINDOW, N_KV_HEADS, Q_PER_KV, HEAD_DIM)
    kb = k.reshape(B, nb, WINDOW, N_KV_HEADS, HEAD_DIM)
    vb = v.reshape(B, nb, WINDOW, N_KV_HEADS, HEAD_DIM)
    pad = ((0, 0), (1, 0), (0, 0), (0, 0), (0, 0))
    kk = jnp.concatenate([jnp.pad(kb, pad)[:, :-1], kb], axis=2)
    vv = jnp.concatenate([jnp.pad(vb, pad)[:, :-1], vb], axis=2)
    s = jnp.einsum('bnqkgd,bnskd->bnkgqs', qb, kk).astype(jnp.float32) * (HEAD_DIM ** -0.5)
    blk = jnp.arange(nb)[:, None] * WINDOW
    qpos = blk + jnp.arange(WINDOW)[None, :]
    kpos = blk - WINDOW + jnp.arange(2 * WINDOW)[None, :]
    rel = qpos[:, :, None] - kpos[:, None, :]
    mask = (rel >= 0) & (rel < WINDOW) & (kpos[:, None, :] >= 0)
    p = sink_probs(s, mask[None, :, None, None], sinks.astype(jnp.float32).reshape(N_KV_HEADS, Q_PER_KV, 1, 1))
    o = jnp.einsum('bnkgqs,bnskd->bnqkgd', p.astype(v.dtype), vv)
    return o.reshape(B, T, ATTN_DIM)


def swa_sample(q, k_new, v_new, k_cache, v_cache, sinks):
    Bd, T = q.shape[:2]
    Wc = k_cache.shape[1]
    kk = jnp.concatenate([k_cache.astype(k_new.dtype), k_new], axis=1)
    vv = jnp.concatenate([v_cache.astype(v_new.dtype), v_new], axis=1)
    qb = q.reshape(Bd, T, N_KV_HEADS, Q_PER_KV, HEAD_DIM)
    s = jnp.einsum('bqkgd,bskd->bkgqs', qb, kk).astype(jnp.float32) * (HEAD_DIM ** -0.5)
    qpos = PAST_LEN + jnp.arange(T)
    kpos = PAST_LEN - Wc + jnp.arange(Wc + T)
    rel = qpos[:, None] - kpos[None, :]
    mask = (rel >= 0) & (rel < WINDOW)
    p = sink_probs(s, mask, sinks.astype(jnp.float32).reshape(N_KV_HEADS, Q_PER_KV, 1, 1))
    o = jnp.einsum('bkgqs,bskd->bqkgd', p.astype(vv.dtype), vv)
    return o.reshape(Bd, T, ATTN_DIM), kk[:, T:], vv[:, T:]


def hgrn_chunked(q, k, v, log_f, S0):
    B, T, H = q.shape[:3]
    C = math.gcd(T, HG_CHUNK)
    nc = T // C

    def to_chunks(a):
        return a.reshape(B, nc, C, H, a.shape[-1]).transpose(1, 0, 3, 2, 4)

    causal = jnp.tril(jnp.ones((C, C), dtype=bool))[:, :, None]

    def step(S, inp):
        qi, ki, vi, gi = inp
        G = jnp.cumsum(gi, axis=-2)
        rel = jnp.where(causal, G[..., :, None, :] - G[..., None, :, :], -jnp.inf)
        A = jnp.einsum('bhtd,bhtsd,bhsd->bhts', qi, jnp.exp(rel), ki)
        o = jnp.einsum('bhts,bhsv->bhtv', A, vi) + jnp.einsum('bhtd,bhdv->bhtv', qi * jnp.exp(G), S)
        G_last = G[..., -1:, :]
        S = jnp.exp(G_last[..., 0, :])[..., None] * S + jnp.einsum('bhsd,bhsv->bhdv', ki * jnp.exp(G_last - G), vi)
        return S, o

    S, o = lax.scan(step, S0, (to_chunks(q), to_chunks(k), to_chunks(v), to_chunks(log_f)))
    o = o.transpose(1, 0, 3, 2, 4).reshape(B, T, H, v.shape[-1])
    return o, S


def hgrn_mixer(h, w_in, lb, g_norm, w_o, S0):
    B, T, _ = h.shape
    z = h @ w_in
    q, f, i, g = jnp.split(z, [HG_KEY_DIM, 2 * HG_KEY_DIM, 2 * HG_KEY_DIM + HG_VAL_DIM], axis=-1)
    q = (jax.nn.silu(q.astype(jnp.float32)) * (HG_DK ** -0.5)).reshape(B, T, HG_HEADS, HG_DK)
    f = f.astype(jnp.float32).reshape(B, T, HG_HEADS, HG_DK)
    lb = lb.astype(jnp.float32).reshape(HG_HEADS, HG_DK)
    log_f = jnp.logaddexp(jnp.log(lb), jnp.log1p(-lb) + jax.nn.log_sigmoid(f))
    k = (1.0 - lb) * jax.nn.sigmoid(-f)
    v = i.astype(jnp.float32).reshape(B, T, HG_HEADS, HG_DV)
    o, S = hgrn_chunked(q, k, v, log_f, S0.astype(jnp.float32))
    gate = jax.nn.silu(g.astype(jnp.float32).reshape(B, T, HG_HEADS, HG_DV))
    o = rms_norm(o, g_norm) * gate
    return o.reshape(B, T, HG_VAL_DIM).astype(h.dtype) @ w_o, S


def swiglu(x, w_gu, w_down):
    a, b = jnp.split(x @ w_gu, 2, axis=-1)
    return (jax.nn.silu(a) * b) @ w_down


def moe_ffn(h, w_router, b_router, w_gu, w_down):
    B, T, D = h.shape
    x = h.reshape(B * T, D)
    logits = (x @ w_router + b_router).astype(jnp.float32)
    top_v, top_i = lax.top_k(logits, TOP_K)
    wts = jax.nn.softmax(top_v, axis=-1)
    gate = jnp.sum(jax.nn.one_hot(top_i, N_EXPERTS, dtype=jnp.float32) * wts[..., None], axis=1)
    y = jnp.zeros((B * T, D), jnp.float32)
    for e in range(N_EXPERTS):
        y = y + gate[:, e:e + 1] * swiglu(x, w_gu[e], w_down[e]).astype(jnp.float32)
    return y.astype(h.dtype).reshape(B, T, D)


def setup_inputs(seed: int = 0) -> dict:
    key = jax.random.key(seed)
    ks = jax.random.split(key, 24)
    f32 = jnp.float32

    def nrm(k, shape, scale):
        return jax.random.normal(k, shape, f32) * scale

    win_rows = min(WINDOW, PAST_LEN)
    return {
        "x_prompt": nrm(ks[0], (BATCH, SEQ, D_MODEL), 1.0),
        "x_sample": nrm(ks[1], (DEC_BATCH, DEC_SEQ, D_MODEL), 1.0),
        "cache_k_win": nrm(ks[2], (N_ATTN, DEC_BATCH, win_rows, N_KV_HEADS, HEAD_DIM), 1.0),
        "cache_v_win": nrm(ks[3], (N_ATTN, DEC_BATCH, win_rows, N_KV_HEADS, HEAD_DIM), 1.0),
        "state_hgrn": nrm(ks[4], (N_HGRN, DEC_BATCH, HG_HEADS, HG_DK, HG_DV), 0.5),
        "norm_mix": 1.0 + nrm(ks[5], (DEPTH, D_MODEL), 0.02),
        "norm_ffn": 1.0 + nrm(ks[6], (DEPTH, D_MODEL), 0.02),
        "norm_final": 1.0 + nrm(ks[7], (D_MODEL,), 0.02),
        "w_qkv": nrm(ks[8], (N_ATTN, D_MODEL, QKV_DIM), D_MODEL ** -0.5),
        "b_qkv": nrm(ks[9], (N_ATTN, QKV_DIM), 0.02),
        "w_o_attn": nrm(ks[10], (N_ATTN, ATTN_DIM, D_MODEL), ATTN_DIM ** -0.5),
        "b_o_attn": nrm(ks[11], (N_ATTN, D_MODEL), 0.02),
        "sinks": nrm(ks[12], (N_ATTN, N_HEADS), 0.5),
        "w_in_hg": nrm(ks[13], (N_HGRN, D_MODEL, HG_IN_DIM), D_MODEL ** -0.5),
        "hg_lower": nrm(ks[14], (DEPTH, HG_KEY_DIM), 0.1),
        "hg_norm": 1.0 + nrm(ks[15], (N_HGRN, HG_DV), 0.02),
        "w_o_hg": nrm(ks[16], (N_HGRN, HG_VAL_DIM, D_MODEL), HG_VAL_DIM ** -0.5),
        "w_gu_dense": nrm(ks[17], (N_DENSE, D_MODEL, 2 * D_FF), D_MODEL ** -0.5),
        "w_down_dense": nrm(ks[18], (N_DENSE, D_FF, D_MODEL), D_FF ** -0.5),
        "w_router": nrm(ks[19], (N_MOE, D_MODEL, N_EXPERTS), D_MODEL ** -0.5),
        "b_router": nrm(ks[20], (N_MOE, N_EXPERTS), 0.01),
        "w_gu_moe": nrm(ks[21], (N_MOE, N_EXPERTS, D_MODEL, 2 * MOE_D_FF), D_MODEL ** -0.5),
        "w_down_moe": nrm(ks[22], (N_MOE, N_EXPERTS, MOE_D_FF, D_MODEL), MOE_D_FF ** -0.5),
    }


def reference(x_prompt, x_sample, cache_k_win, cache_v_win, state_hgrn, norm_mix, norm_ffn, norm_final,
              w_qkv, b_qkv, w_o_attn, b_o_attn, sinks, w_in_hg, hg_lower, hg_norm, w_o_hg,
              w_gu_dense, w_down_dense, w_router, b_router, w_gu_moe, w_down_moe):
    T_p = x_prompt.shape[1]
    T_s = x_sample.shape[1]
    Wc = cache_k_win.shape[2]
    pos_p = jnp.arange(T_p)
    pos_s = PAST_LEN + jnp.arange(T_s)
    lb_sm = jax.nn.softmax(hg_lower.astype(jnp.float32), axis=0)
    lb_all = jnp.cumsum(lb_sm, axis=0) - lb_sm[0]

    xp, xs = x_prompt, x_sample
    kwp, vwp, kws, vws, shp, shs = [], [], [], [], [], []
    for l in range(DEPTH):
        j = l // 2
        hp = rms_norm(xp, norm_mix[l])
        hs = rms_norm(xs, norm_mix[l])
        if l % 2 == 0:
            qp, kp, vp = attn_project(hp, w_qkv[j], b_qkv[j], pos_p)
            op = swa_prompt(qp, kp, vp, sinks[j])
            qs, ks_, vs_ = attn_project(hs, w_qkv[j], b_qkv[j], pos_s)
            os_, kbuf, vbuf = swa_sample(qs, ks_, vs_, cache_k_win[j], cache_v_win[j], sinks[j])
            kwp.append(kp[:, T_p - Wc:])
            vwp.append(vp[:, T_p - Wc:])
            kws.append(kbuf)
            vws.append(vbuf)
            xp = xp + op @ w_o_attn[j] + b_o_attn[j]
            xs = xs + os_ @ w_o_attn[j] + b_o_attn[j]
            hp = rms_norm(xp, norm_ffn[l])
            hs = rms_norm(xs, norm_ffn[l])
            xp = xp + swiglu(hp, w_gu_dense[j], w_down_dense[j])
            xs = xs + swiglu(hs, w_gu_dense[j], w_down_dense[j])
        else:
            S0p = jnp.zeros((xp.shape[0], HG_HEADS, HG_DK, HG_DV), jnp.float32)
            op, Sp = hgrn_mixer(hp, w_in_hg[j], lb_all[l], hg_norm[j], w_o_hg[j], S0p)
            os_, Ss = hgrn_mixer(hs, w_in_hg[j], lb_all[l], hg_norm[j], w_o_hg[j], state_hgrn[j])
            shp.append(Sp)
            shs.append(Ss)
            xp = xp + op
            xs = xs + os_
            hp = rms_norm(xp, norm_ffn[l])
            hs = rms_norm(xs, norm_ffn[l])
            xp = xp + moe_ffn(hp, w_router[j], b_router[j], w_gu_moe[j], w_down_moe[j])
            xs = xs + moe_ffn(hs, w_router[j], b_router[j], w_gu_moe[j], w_down_moe[j])

    y_prompt = rms_norm(xp, norm_final)
    y_sample = rms_norm(xs, norm_final)
    return (y_prompt, y_sample, jnp.stack(kwp), jnp.stack(vwp), jnp.stack(kws), jnp.stack(vws), jnp.stack(shp), jnp.stack(shs))
```

```python
import functools
import math

import jax
import jax.numpy as jnp
from jax import lax
from jax.experimental import pallas as pl
from jax.experimental.pallas import tpu as pltpu

F32 = jnp.float32
BF16 = jnp.bfloat16
HIGHEST = lax.Precision.HIGHEST

NORM_EPS = 1e-5
WINDOW = 128
PAST_LEN = 16384
ROPE_THETA = 10000.0
HG_DK = 128
HG_CHUNK = 64
HG_SAFE_DECAY = 60.0
HG_SAFE_Q = 1e9
TOP_K = 2
LANES = 128
VMEM_LIMIT = 56 * 1024 * 1024

NT_DIMS = (((1,), (1,)), ((), ()))
TN_DIMS = (((0,), (0,)), ((), ()))


def _pick_tile(n, target):
    for t in (1536, 1024, 768, 512, 384, 256, 192, 128, 64, 32, 16, 8):
        if t <= target and n % t == 0:
            return t
    raise ValueError(f"no row tile for {n}")


def _params(*sem):
    return pltpu.CompilerParams(dimension_semantics=sem, vmem_limit_bytes=VMEM_LIMIT)


def _rms(x, g):
    return x * lax.rsqrt(jnp.mean(x * x, axis=-1, keepdims=True) + NORM_EPS) * g


def _qkv_body(x_ref, g_ref, w_ref, b_ref, cos_ref, sin_ref, q_ref, k_ref, v_ref, wbf_ref, *, n_q, n_k, hd):
    @pl.when(pl.program_id(0) == 0)
    def _():
        wbf_ref[...] = w_ref[...].astype(BF16)

    h = _rms(x_ref[...], g_ref[...]).astype(BF16)
    y = jnp.dot(h, wbf_ref[...], preferred_element_type=F32) + b_ref[...]
    cos = cos_ref[...]
    sin = sin_ref[...]
    lane = lax.broadcasted_iota(jnp.int32, cos.shape, 1)
    first = (lane % hd) < (hd // 2)

    def rope(blk):
        partner = jnp.where(first, pltpu.roll(blk, LANES - hd // 2, 1), pltpu.roll(blk, hd // 2, 1))
        return blk * cos + partner * sin

    scale = hd ** -0.5
    for j in range(n_q // LANES):
        q_ref[:, j * LANES:(j + 1) * LANES] = (rope(y[:, j * LANES:(j + 1) * LANES]) * scale).astype(BF16)
    for j in range(n_k // LANES):
        k_ref[:, j * LANES:(j + 1) * LANES] = rope(y[:, n_q + j * LANES:n_q + (j + 1) * LANES])
    v_ref[...] = y[:, n_q + n_k:]


def _qkv_rope(x, g, w, b, cos_t, sin_t, n_q, n_k, hd):
    n, d = x.shape
    n_out = w.shape[1]
    tm = _pick_tile(n, 512)
    row = lambda i: (i, 0)
    fix = lambda i: (0, 0)
    return pl.pallas_call(
        functools.partial(_qkv_body, n_q=n_q, n_k=n_k, hd=hd),
        grid=(n // tm,),
        in_specs=[pl.BlockSpec((tm, d), row), pl.BlockSpec((1, d), fix), pl.BlockSpec((d, n_out), fix),
                  pl.BlockSpec((1, n_out), fix), pl.BlockSpec((tm, LANES), row), pl.BlockSpec((tm, LANES), row)],
        out_specs=[pl.BlockSpec((tm, n_q), row), pl.BlockSpec((tm, n_k), row), pl.BlockSpec((tm, n_k), row)],
        out_shape=[jax.ShapeDtypeStruct((n, n_q), BF16), jax.ShapeDtypeStruct((n, n_k), F32),
                   jax.ShapeDtypeStruct((n, n_k), F32)],
        scratch_shapes=[pltpu.VMEM((d, n_out), BF16)],
        compiler_params=_params("arbitrary"),
        name="qkv_rope",
    )(x, g, w, b, cos_t, sin_t)


def _sink_column(sink_ref, kh, g_per, rows_per):
    blk = lax.broadcasted_iota(jnp.int32, (g_per * rows_per, 1), 0) // rows_per
    col = jnp.full((g_per * rows_per, 1), sink_ref[kh * g_per], F32)
    for g in range(1, g_per):
        col = jnp.where(blk == g, sink_ref[kh * g_per + g], col)
    return col


def _stack_heads(q, kh, g_per, hd):
    return jnp.concatenate([q[:, (kh * g_per + g) * hd:(kh * g_per + g + 1) * hd] for g in range(g_per)], axis=0)


def _swa_prompt_body(sink_ref, q_ref, kp_ref, kc_ref, vp_ref, vc_ref, o_all_ref, o_ref, *, n_kv, g_per, hd, w):
    del o_all_ref
    n = pl.program_id(1)
    q = q_ref[...]
    kk = jnp.concatenate([kp_ref[...], kc_ref[...]], axis=0).astype(BF16)
    vv = jnp.concatenate([vp_ref[...], vc_ref[...]], axis=0).astype(BF16)
    i = lax.broadcasted_iota(jnp.int32, (g_per * w, 2 * w), 0) % w
    j = lax.broadcasted_iota(jnp.int32, (g_per * w, 2 * w), 1)
    mask = (j > i) & (j <= i + w) & ((j >= w) | (n > 0))
    outs = []
    for kh in range(n_kv):
        q4 = _stack_heads(q, kh, g_per, hd)
        s = lax.dot_general(q4, kk[:, kh * hd:(kh + 1) * hd], NT_DIMS, preferred_element_type=F32)
        s = jnp.where(mask, s, -jnp.inf)
        sink = _sink_column(sink_ref, kh, g_per, w)
        m = jnp.maximum(jnp.max(s, axis=-1, keepdims=True), sink)
        p = jnp.exp(s - m)
        p = p / (jnp.sum(p, axis=-1, keepdims=True) + jnp.exp(sink - m))
        o4 = jnp.dot(p.astype(BF16), vv[:, kh * hd:(kh + 1) * hd], preferred_element_type=F32)
        outs.append(jnp.concatenate([o4[g * w:(g + 1) * w] for g in range(g_per)], axis=1))
    o_ref[...] = jnp.concatenate(outs, axis=1).astype(BF16)


def _swa_prompt(q, k, v, sinks, o_all, batch, seq, n_kv, g_per, hd):
    w = WINDOW
    nb = seq // w
    dq = q.shape[1]
    dkv = k.shape[1]
    cur = lambda b, n: (b * nb + n, 0)
    prev = lambda b, n: (b * nb + jnp.maximum(n - 1, 0), 0)
    return pl.pallas_call(
        functools.partial(_swa_prompt_body, n_kv=n_kv, g_per=g_per, hd=hd, w=w),
        grid=(batch, nb),
        in_specs=[pl.BlockSpec(memory_space=pltpu.SMEM),
                  pl.BlockSpec((w, dq), cur), pl.BlockSpec((w, dkv), prev), pl.BlockSpec((w, dkv), cur),
                  pl.BlockSpec((w, dkv), prev), pl.BlockSpec((w, dkv), cur),
                  pl.BlockSpec(memory_space=pl.ANY)],
        out_specs=pl.BlockSpec((w, dq), cur),
        out_shape=jax.ShapeDtypeStruct(o_all.shape, o_all.dtype),
        input_output_aliases={6: 0},
        compiler_params=_params("arbitrary", "arbitrary"),
        name="swa_prompt",
    )(sinks, q, k, k, v, v, o_all)


def _swa_sample_body(sink_ref, q_ref, kn_ref, vn_ref, kc_ref, vc_ref, o_all_ref, o_ref, *, n_kv, g_per, hd, bt, t, wc):
    del o_all_ref
    r = bt * t
    q = q_ref[...]
    kn = kn_ref[...].astype(BF16)
    vn = vn_ref[...].astype(BF16)
    kc = kc_ref[...].reshape(bt * wc, n_kv * hd).astype(BF16)
    vc = vc_ref[...].reshape(bt * wc, n_kv * hd).astype(BF16)
    row_c = lax.broadcasted_iota(jnp.int32, (g_per * r, bt * wc), 0) % r
    col_c = lax.broadcasted_iota(jnp.int32, (g_per * r, bt * wc), 1)
    mask_c = (col_c // wc == row_c // t) & (col_c % wc > row_c % t + (wc - WINDOW))
    row_n = lax.broadcasted_iota(jnp.int32, (g_per * r, r), 0) % r
    col_n = lax.broadcasted_iota(jnp.int32, (g_per * r, r), 1)
    mask_n = (col_n // t == row_n // t) & (col_n % t <= row_n % t)
    outs = []
    for kh in range(n_kv):
        hs = slice(kh * hd, (kh + 1) * hd)
        q4 = _stack_heads(q, kh, g_per, hd)
        sc = jnp.where(mask_c, lax.dot_general(q4, kc[:, hs], NT_DIMS, preferred_element_type=F32), -jnp.inf)
        sn = jnp.where(mask_n, lax.dot_general(q4, kn[:, hs], NT_DIMS, preferred_element_type=F32), -jnp.inf)
        sink = _sink_column(sink_ref, kh, g_per, r)
        m = jnp.maximum(jnp.maximum(jnp.max(sc, axis=-1, keepdims=True), jnp.max(sn, axis=-1, keepdims=True)), sink)
        pc = jnp.exp(sc - m)
        pn = jnp.exp(sn - m)
        den = jnp.sum(pc, axis=-1, keepdims=True) + jnp.sum(pn, axis=-1, keepdims=True) + jnp.exp(sink - m)
        o4 = (jnp.dot((pc / den).astype(BF16), vc[:, hs], preferred_element_type=F32)
              + jnp.dot((pn / den).astype(BF16), vn[:, hs], preferred_element_type=F32))
        outs.append(jnp.concatenate([o4[g * r:(g + 1) * r] for g in range(g_per)], axis=1))
    o_ref[...] = jnp.concatenate(outs, axis=1).astype(BF16)


def _swa_sample(q, k, v, k_cache, v_cache, sinks, o_all, row0, n_seq, t, n_kv, g_per, hd):
    wc = k_cache.shape[1]
    dq = q.shape[1]
    dkv = k.shape[1]
    bt = 8 if n_seq % 8 == 0 else n_seq
    r = bt * t
    assert row0 % r == 0
    off = row0 // r
    rows = lambda i: (off + i, 0)
    return pl.pallas_call(
        functools.partial(_swa_sample_body, n_kv=n_kv, g_per=g_per, hd=hd, bt=bt, t=t, wc=wc),
        grid=(n_seq // bt,),
        in_specs=[pl.BlockSpec(memory_space=pltpu.SMEM),
                  pl.BlockSpec((r, dq), rows), pl.BlockSpec((r, dkv), rows), pl.BlockSpec((r, dkv), rows),
                  pl.BlockSpec((bt, wc, dkv), lambda i: (i, 0, 0)), pl.BlockSpec((bt, wc, dkv), lambda i: (i, 0, 0)),
                  pl.BlockSpec(memory_space=pl.ANY)],
        out_specs=pl.BlockSpec((r, dq), rows),
        out_shape=jax.ShapeDtypeStruct(o_all.shape, o_all.dtype),
        input_output_aliases={6: 0},
        compiler_params=_params("arbitrary"),
        name="swa_sample",
    )(sinks, q, k, v, k_cache, v_cache, o_all)


def _swiglu_step(h, wg_ref, wu_ref, wd_ref, acc_ref):
    a = jnp.dot(h, wg_ref[...].astype(BF16), preferred_element_type=F32)
    b = jnp.dot(h, wu_ref[...].astype(BF16), preferred_element_type=F32)
    act = (a * jax.nn.sigmoid(a) * b).astype(BF16)
    acc_ref[...] += jnp.dot(act, wd_ref[...].astype(BF16), preferred_element_type=F32)


def _oproj_ffn_body(o_ref, x_ref, wo_ref, bo_ref, g_ref, wg_ref, wu_ref, wd_ref, out_ref,
                    wo_bf, x1_ref, h_ref, acc_ref):
    i = pl.program_id(0)
    c = pl.program_id(1)

    @pl.when((i == 0) & (c == 0))
    def _():
        wo_bf[...] = wo_ref[...].astype(BF16)

    @pl.when(c == 0)
    def _():
        x1 = x_ref[...] + jnp.dot(o_ref[...], wo_bf[...], preferred_element_type=F32) + bo_ref[...]
        x1_ref[...] = x1
        h_ref[...] = _rms(x1, g_ref[...]).astype(BF16)
        acc_ref[...] = jnp.zeros_like(acc_ref)

    _swiglu_step(h_ref[...], wg_ref, wu_ref, wd_ref, acc_ref)

    @pl.when(c == pl.num_programs(1) - 1)
    def _():
        out_ref[...] = x1_ref[...] + acc_ref[...]


def _ff_chunk(d_ff):
    for tf in (512, 256, 128):
        if d_ff % tf == 0:
            return tf
    raise ValueError(f"d_ff {d_ff} is not a multiple of {LANES}")


def _oproj_ffn(o, x, wo, bo, g, w_gu, w_down):
    n, d = x.shape
    d_ff = w_down.shape[0]
    tf = _ff_chunk(d_ff)
    nc = d_ff // tf
    tm = _pick_tile(n, 768)
    row = lambda i, c: (i, 0)
    fix = lambda i, c: (0, 0)
    return pl.pallas_call(
        _oproj_ffn_body,
        grid=(n // tm, nc),
        in_specs=[pl.BlockSpec((tm, o.shape[1]), row), pl.BlockSpec((tm, d), row),
                  pl.BlockSpec(wo.shape, fix), pl.BlockSpec((1, d), fix), pl.BlockSpec((1, d), fix),
                  pl.BlockSpec((d, tf), lambda i, c: (0, c)), pl.BlockSpec((d, tf), lambda i, c: (0, c + nc)),
                  pl.BlockSpec((tf, d), lambda i, c: (c, 0))],
        out_specs=pl.BlockSpec((tm, d), row),
        out_shape=jax.ShapeDtypeStruct((n, d), F32),
        scratch_shapes=[pltpu.VMEM(wo.shape, BF16), pltpu.VMEM((tm, d), F32), pltpu.VMEM((tm, d), BF16),
                        pltpu.VMEM((tm, d), F32)],
        compiler_params=_params("arbitrary", "arbitrary"),
        name="oproj_ffn",
    )(o, x, wo, bo, g, w_gu, w_gu, w_down)


def _hg_in_body(x_ref, g_ref, w_ref, lb_ref, q_ref, k_ref, lf_ref, v_ref, gate_ref, h_ref):
    j = pl.program_id(1)

    @pl.when(j == 0)
    def _():
        h_ref[...] = _rms(x_ref[...], g_ref[...]).astype(BF16)

    z = jnp.dot(h_ref[...], w_ref[...].astype(BF16), preferred_element_type=F32)

    @pl.when(j == 0)
    def _():
        q_ref[...] = z * jax.nn.sigmoid(z) * (HG_DK ** -0.5)

    @pl.when(j == 1)
    def _():
        lb = lb_ref[...]
        lf_ref[...] = jnp.logaddexp(jnp.log(lb), jnp.log1p(-lb) + jax.nn.log_sigmoid(z))
        k_ref[...] = (1.0 - lb) * jax.nn.sigmoid(-z)

    @pl.when(j == 2)
    def _():
        v_ref[...] = z

    @pl.when(j == 3)
    def _():
        gate_ref[...] = z * jax.nn.sigmoid(z)


def _hg_inproj(x, g, w_in, lb):
    n, d = x.shape
    tm = _pick_tile(n, 512)
    row = lambda i, j: (i, 0)
    fix = lambda i, j: (0, 0)
    out = jax.ShapeDtypeStruct((n, d), F32)
    return pl.pallas_call(
        _hg_in_body,
        grid=(n // tm, 4),
        in_specs=[pl.BlockSpec((tm, d), row), pl.BlockSpec((1, d), fix), pl.BlockSpec((d, d), lambda i, j: (0, j)),
                  pl.BlockSpec((1, d), fix)],
        out_specs=[pl.BlockSpec((tm, d), row)] * 5,
        out_shape=[out] * 5,
        scratch_shapes=[pltpu.VMEM((tm, d), BF16)],
        compiler_params=_params("arbitrary", "arbitrary"),
        name="hgrn_inproj",
    )(x, g, w_in, lb)


def _hg_core_body(q_ref, k_ref, lf_ref, v_ref, s0_ref, o_ref, sout_ref, st_ref, g_ref, oi_ref, *, c_len, n_heads):
    c = pl.program_id(1)
    dk = HG_DK

    @pl.when(c == 0)
    def _():
        for h in range(n_heads):
            st_ref[h] = s0_ref[0, h].T

    q = q_ref[...]
    k = k_ref[...]
    v = v_ref[...]
    r = lax.broadcasted_iota(jnp.int32, (c_len, c_len), 0)
    s = lax.broadcasted_iota(jnp.int32, (c_len, c_len), 1)
    causal = r >= s
    gcum = jnp.dot(causal.astype(F32), lf_ref[...], precision=HIGHEST, preferred_element_type=F32)
    g_mid = gcum[c_len // 2 - 1:c_len // 2, :]
    g_last = gcum[c_len - 1:c_len, :]
    vb = v.astype(BF16)
    safe = (jnp.max(-g_last) <= HG_SAFE_DECAY) & (jnp.max(jnp.abs(q)) <= HG_SAFE_Q)

    @pl.when(safe)
    def _():
        qi = (q * jnp.exp(gcum - g_mid)).astype(BF16)
        ki = (k * jnp.exp(g_mid - gcum)).astype(BF16)
        for h in range(n_heads):
            hs = slice(h * dk, (h + 1) * dk)
            a = lax.dot_general(qi[:, hs], ki[:, hs], NT_DIMS, preferred_element_type=F32)
            a = jnp.where(causal, a, 0.0).astype(BF16)
            oi_ref[:, hs] = jnp.dot(a, vb[:, hs], preferred_element_type=F32)

    @pl.when(jnp.logical_not(safe))
    def _():
        g_ref[...] = gcum
        lane_h = lax.broadcasted_iota(jnp.int32, (n_heads * dk, n_heads * dk), 0) // dk
        lane_w = lax.broadcasted_iota(jnp.int32, (n_heads * dk, n_heads * dk), 1) // dk
        head_sum = (lane_h == lane_w).astype(BF16)
        t_idx = lax.broadcasted_iota(jnp.int32, (c_len, 1), 0)

        def key_row(j, acc):
            gj = g_ref[pl.ds(j, 1), :]
            decay = jnp.exp(jnp.where(t_idx >= j, gcum - gj, -jnp.inf))
            term = (q * decay * k_ref[pl.ds(j, 1), :]).astype(BF16)
            a_j = jnp.dot(term, head_sum, preferred_element_type=F32)
            return acc + a_j * v_ref[pl.ds(j, 1), :]

        oi_ref[...] = lax.fori_loop(0, c_len, key_row, jnp.zeros((c_len, n_heads * dk), F32))

    qs = (q * jnp.exp(gcum)).astype(BF16)
    ks = (k * jnp.exp(g_last - gcum)).astype(BF16)
    e_last = jnp.exp(g_last)
    for h in range(n_heads):
        hs = slice(h * dk, (h + 1) * dk)
        st = st_ref[h]
        o_ref[:, hs] = oi_ref[:, hs] + lax.dot_general(qs[:, hs], st.astype(BF16), NT_DIMS,
                                                       preferred_element_type=F32)
        st_ref[h] = e_last[:, hs] * st + lax.dot_general(vb[:, hs], ks[:, hs], TN_DIMS,
                                                         preferred_element_type=F32)

    @pl.when(c == pl.num_programs(1) - 1)
    def _():
        for h in range(n_heads):
            sout_ref[0, h] = st_ref[h].T


def _hg_core(q, k, lf, v, s0, row0, n_seq, seq_len, c_len):
    d = q.shape[1]
    n_heads = d // HG_DK
    nc = seq_len // c_len
    assert row0 % c_len == 0
    off = row0 // c_len
    rows = lambda b, c: (off + b * nc + c, 0)
    state = lambda b, c: (b, 0, 0, 0)
    return pl.pallas_call(
        functools.partial(_hg_core_body, c_len=c_len, n_heads=n_heads),
        grid=(n_seq, nc),
        in_specs=[pl.BlockSpec((c_len, d), rows)] * 4 + [pl.BlockSpec((1, n_heads, HG_DK, HG_DK), state)],
        out_specs=[pl.BlockSpec((c_len, d), lambda b, c: (b * nc + c, 0)),
                   pl.BlockSpec((1, n_heads, HG_DK, HG_DK), state)],
        out_shape=[jax.ShapeDtypeStruct((n_seq * seq_len, d), F32),
                   jax.ShapeDtypeStruct((n_seq, n_heads, HG_DK, HG_DK), F32)],
        scratch_shapes=[pltpu.VMEM((n_heads, HG_DK, HG_DK), F32), pltpu.VMEM((c_len, d), F32),
                        pltpu.VMEM((c_len, d), F32)],
        compiler_params=_params("arbitrary", "arbitrary"),
        name=f"hgrn_core_{c_len}",
    )(q, k, lf, v, s0)


def _hg_out_router_body(o_ref, gate_ref, x_ref, gn_ref, wo_ref, g_ref, wr_ref, br_ref, x3_ref, ids_ref, wts_ref,
                        wo_bf, *, n_heads, n_exp):
    @pl.when(pl.program_id(0) == 0)
    def _():
        wo_bf[...] = wo_ref[...].astype(BF16)

    dk = HG_DK
    o = o_ref[...]
    gn = gn_ref[...]
    normed = jnp.concatenate([_rms(o[:, h * dk:(h + 1) * dk], gn) for h in range(n_heads)], axis=1)
    y = (normed * gate_ref[...]).astype(BF16)
    x3 = x_ref[...] + jnp.dot(y, wo_bf[...], preferred_element_type=F32)
    x3_ref[...] = x3
    h4 = _rms(x3, g_ref[...])
    logits = jnp.dot(h4, wr_ref[...], precision=HIGHEST, preferred_element_type=F32) + br_ref[...]
    lane = lax.broadcasted_iota(jnp.int32, logits.shape, 1)
    logits = jnp.where(lane < n_exp, logits, -jnp.inf)
    m1 = jnp.max(logits, axis=-1, keepdims=True)
    i1 = jnp.min(jnp.where(logits == m1, lane, LANES), axis=-1, keepdims=True)
    rest = jnp.where(lane == i1, -jnp.inf, logits)
    m2 = jnp.max(rest, axis=-1, keepdims=True)
    i2 = jnp.min(jnp.where(rest == m2, lane, LANES), axis=-1, keepdims=True)
    e2 = jnp.exp(m2 - m1)
    den = 1.0 + e2
    ids_ref[...] = jnp.where(lane == 0, i1, jnp.where(lane == 1, i2, 0))
    wts_ref[...] = jnp.where(lane == 0, 1.0 / den, jnp.where(lane == 1, e2 / den, 0.0))


def _hg_out_router(o, gate, x, gn, wo, g, w_router, b_router, n_exp):
    n, d = x.shape
    tm = _pick_tile(n, 512)
    row = lambda i: (i, 0)
    fix = lambda i: (0, 0)
    return pl.pallas_call(
        functools.partial(_hg_out_router_body, n_heads=d // HG_DK, n_exp=n_exp),
        grid=(n // tm,),
        in_specs=[pl.BlockSpec((tm, d), row), pl.BlockSpec((tm, d), row), pl.BlockSpec((tm, d), row),
                  pl.BlockSpec((1, HG_DK), fix), pl.BlockSpec(wo.shape, fix), pl.BlockSpec((1, d), fix),
                  pl.BlockSpec((d, LANES), fix), pl.BlockSpec((1, LANES), fix)],
        out_specs=[pl.BlockSpec((tm, d), row), pl.BlockSpec((tm, LANES), row), pl.BlockSpec((tm, LANES), row)],
        out_shape=[jax.ShapeDtypeStruct((n, d), F32), jax.ShapeDtypeStruct((n, LANES), jnp.int32),
                   jax.ShapeDtypeStruct((n, LANES), F32)],
        scratch_shapes=[pltpu.VMEM(wo.shape, BF16)],
        compiler_params=_params("arbitrary"),
        name="hgrn_out_router",
    )(o, gate, x, gn, wo, g, w_router, b_router)


def _moe_body(te_ref, nv_ref, rows_ref, x_hbm, g_ref, wg_ref, wu_ref, wd_ref, y_hbm,
              xg_ref, h_ref, acc_ref, sem_in, sem_out, *, tm):
    del te_ref
    i = pl.program_id(0)
    c = pl.program_id(1)
    nv = nv_ref[i]

    def row_in(j, token):
        return pltpu.make_async_copy(x_hbm.at[pl.ds(token, 1)], xg_ref.at[pl.ds(j, 1)], sem_in)

    def row_out(j, slot):
        return pltpu.make_async_copy(xg_ref.at[pl.ds(j, 1)], y_hbm.at[pl.ds(slot, 1)], sem_out)

    @pl.when(nv > 0)
    def _():
        @pl.when(c == 0)
        def _():
            def start(j, carry):
                row_in(j, lax.div(rows_ref[0, 0, j], TOP_K)).start()
                return carry

            def wait(j, carry):
                row_in(j, 0).wait()
                return carry

            lax.fori_loop(0, tm, start, 0)
            lax.fori_loop(0, tm, wait, 0)
            h_ref[...] = _rms(xg_ref[...], g_ref[...]).astype(BF16)
            acc_ref[...] = jnp.zeros_like(acc_ref)

        _swiglu_step(h_ref[...], wg_ref, wu_ref, wd_ref, acc_ref)

        @pl.when(c == pl.num_programs(1) - 1)
        def _():
            xg_ref[...] = acc_ref[...]

            def start(j, carry):
                row_out(j, rows_ref[0, 0, j]).start()
                return carry

            def wait(j, carry):
                row_out(j, 0).wait()
                return carry

            lax.fori_loop(0, nv, start, 0)
            lax.fori_loop(0, nv, wait, 0)


def _moe_experts(x, g, ids, w_gu, w_down, tm):
    n, d = x.shape
    n_exp, d_ff = w_down.shape[0], w_down.shape[1]
    tf = _ff_chunk(d_ff)
    nc = d_ff // tf
    n_asg = n * TOP_K
    n_tiles = (n_asg + n_exp * (tm - 1) + tm - 1) // tm

    e_flat = ids.reshape(n_asg)
    onehot = (e_flat[:, None] == jnp.arange(n_exp, dtype=jnp.int32)[None, :]).astype(jnp.int32)
    counts = jnp.sum(onehot, axis=0)
    rank = jnp.sum((jnp.cumsum(onehot, axis=0) - onehot) * onehot, axis=1)
    tiles_per = (counts + tm - 1) // tm
    tile_end = jnp.cumsum(tiles_per)
    tile_start = tile_end - tiles_per
    slot = tile_start[e_flat] * tm + rank
    rows = jnp.zeros((n_tiles * tm,), jnp.int32).at[slot].set(jnp.arange(n_asg, dtype=jnp.int32))
    tile = jnp.arange(n_tiles, dtype=jnp.int32)
    last = tile_end[-1] - 1
    te = jnp.minimum(jnp.searchsorted(tile_end, jnp.minimum(tile, last), side="right"), n_exp - 1).astype(jnp.int32)
    nv = jnp.where(tile <= last, jnp.clip(counts[te] - (tile - tile_start[te]) * tm, 0, tm), 0).astype(jnp.int32)

    def chunk(i, c, te_ref, nv_ref):
        return jnp.where(nv_ref[i] > 0, c, nc - 1)

    return pl.pallas_call(
        functools.partial(_moe_body, tm=tm),
        grid_spec=pltpu.PrefetchScalarGridSpec(
            num_scalar_prefetch=2,
            grid=(n_tiles, nc),
            in_specs=[pl.BlockSpec((1, 1, tm), lambda i, c, te_ref, nv_ref: (i, 0, 0), memory_space=pltpu.SMEM),
                      pl.BlockSpec(memory_space=pl.ANY),
                      pl.BlockSpec((1, d), lambda i, c, te_ref, nv_ref: (0, 0)),
                      pl.BlockSpec((None, d, tf), lambda i, c, te_ref, nv_ref: (te_ref[i], 0, chunk(i, c, te_ref, nv_ref))),
                      pl.BlockSpec((None, d, tf),
                                   lambda i, c, te_ref, nv_ref: (te_ref[i], 0, chunk(i, c, te_ref, nv_ref) + nc)),
                      pl.BlockSpec((None, tf, d), lambda i, c, te_ref, nv_ref: (te_ref[i], chunk(i, c, te_ref, nv_ref), 0))],
            out_specs=pl.BlockSpec(memory_space=pl.ANY),
            scratch_shapes=[pltpu.VMEM((tm, d), F32), pltpu.VMEM((tm, d), BF16), pltpu.VMEM((tm, d), F32),
                            pltpu.SemaphoreType.DMA, pltpu.SemaphoreType.DMA]),
        out_shape=jax.ShapeDtypeStruct((n_asg, d), F32),
        compiler_params=_params("arbitrary", "arbitrary"),
        name="moe_experts",
    )(te, nv, rows.reshape(n_tiles, 1, tm), x, g, w_gu, w_gu, w_down)


def _combine_body(x_ref, y_ref, wts_ref, g_ref, out_ref):
    d = x_ref.shape[1]
    wts = wts_ref[...]
    moe = wts[:, 0:1] * y_ref[:, :d] + wts[:, 1:2] * y_ref[:, d:]
    out_ref[...] = _rms(x_ref[...] + moe, g_ref[...])


def _combine(x, y2, wts, g):
    n, d = x.shape
    tm = _pick_tile(n, 512)
    row = lambda i: (i, 0)
    return pl.pallas_call(
        _combine_body,
        grid=(n // tm,),
        in_specs=[pl.BlockSpec((tm, d), row), pl.BlockSpec((tm, TOP_K * d), row), pl.BlockSpec((tm, LANES), row),
                  pl.BlockSpec((1, d), lambda i: (0, 0))],
        out_specs=pl.BlockSpec((tm, d), row),
        out_shape=jax.ShapeDtypeStruct((n, d), F32),
        compiler_params=_params("arbitrary"),
        name="moe_combine",
    )(x, y2, wts, g)


def _rope_tables(pos, hd):
    half = hd // 2
    inv = ROPE_THETA ** (-jnp.arange(half, dtype=F32) / half)
    ang = pos.astype(F32)[:, None] * inv[None, :]
    cos = jnp.cos(ang)
    sin = jnp.sin(ang)
    reps = LANES // hd
    return jnp.tile(jnp.concatenate([cos, cos], axis=1), (1, reps)), jnp.tile(jnp.concatenate([-sin, sin], axis=1), (1, reps))


def kernel(x_prompt, x_sample, cache_k_win, cache_v_win, state_hgrn, norm_mix, norm_ffn, norm_final,
           w_qkv, b_qkv, w_o_attn, b_o_attn, sinks, w_in_hg, hg_lower, hg_norm, w_o_hg,
           w_gu_dense, w_down_dense, w_router, b_router, w_gu_moe, w_down_moe):
    batch, seq, d = x_prompt.shape
    n_seq, t_dec, _ = x_sample.shape
    n_kv, hd = cache_k_win.shape[3], cache_k_win.shape[4]
    n_heads = sinks.shape[1]
    g_per = n_heads // n_kv
    n_q, n_k = n_heads * hd, n_kv * hd
    n_p, n_s = batch * seq, n_seq * t_dec
    n_exp = w_router.shape[2]
    wc = cache_k_win.shape[2]
    assert norm_mix.shape[0] == 2 and hd * 2 == LANES and d % HG_DK == 0 and seq % WINDOW == 0 and wc == WINDOW

    x = jnp.concatenate([x_prompt.reshape(n_p, d), x_sample.reshape(n_s, d)], axis=0)

    pos = jnp.concatenate([jnp.tile(jnp.arange(seq), batch), jnp.tile(PAST_LEN + jnp.arange(t_dec), n_seq)])
    cos_t, sin_t = _rope_tables(pos, hd)
    q, k, v = _qkv_rope(x, norm_mix[0:1], w_qkv[0], b_qkv[0:1], cos_t, sin_t, n_q, n_k, hd)
    o = _swa_prompt(q, k, v, sinks[0], jnp.zeros((n_p + n_s, n_q), BF16), batch, seq, n_kv, g_per, hd)
    kc = cache_k_win[0].reshape(n_seq, wc, n_k)
    vc = cache_v_win[0].reshape(n_seq, wc, n_k)
    o = _swa_sample(q, k, v, kc, vc, sinks[0], o, n_p, n_seq, t_dec, n_kv, g_per, hd)
    k_win_p = k[:n_p].reshape(batch, seq, n_kv, hd)[:, seq - wc:]
    v_win_p = v[:n_p].reshape(batch, seq, n_kv, hd)[:, seq - wc:]
    k_win_s = jnp.concatenate([kc, k[n_p:].reshape(n_seq, t_dec, n_k)], axis=1)[:, t_dec:].reshape(n_seq, wc, n_kv, hd)
    v_win_s = jnp.concatenate([vc, v[n_p:].reshape(n_seq, t_dec, n_k)], axis=1)[:, t_dec:].reshape(n_seq, wc, n_kv, hd)
    x = _oproj_ffn(o, x, w_o_attn[0], b_o_attn[0:1], norm_ffn[0:1], w_gu_dense[0], w_down_dense[0])

    lb_sm = jax.nn.softmax(hg_lower.astype(F32), axis=0)
    lb = (jnp.cumsum(lb_sm, axis=0) - lb_sm[0])[1:2]
    hq, hk, hlf, hv, hgate = _hg_inproj(x, norm_mix[1:2], w_in_hg[0], lb)
    n_hh = d // HG_DK
    c_p = math.gcd(seq, HG_CHUNK)
    o_p, s_p = _hg_core(hq, hk, hlf, hv, jnp.zeros((batch, n_hh, HG_DK, HG_DK), F32), 0, batch, seq, c_p)
    c_s = 8 * ((t_dec + 7) // 8)
    pad = lambda a: jnp.pad(a[n_p:].reshape(n_seq, t_dec, d), ((0, 0), (0, c_s - t_dec), (0, 0))).reshape(n_seq * c_s, d)
    o_s, s_s = _hg_core(pad(hq), pad(hk), pad(hlf), pad(hv), state_hgrn[0], 0, n_seq, c_s, c_s)
    o_s = o_s.reshape(n_seq, c_s, d)[:, :t_dec].reshape(n_s, d)
    o_h = jnp.concatenate([o_p, o_s], axis=0)

    wr = jnp.pad(w_router[0], ((0, 0), (0, LANES - n_exp)))
    br = jnp.pad(b_router[0:1], ((0, 0), (0, LANES - n_exp)))
    x3, ids, wts = _hg_out_router(o_h, hgate, x, hg_norm[0:1], w_o_hg[0], norm_ffn[1:2], wr, br, n_exp)
    y2 = _moe_experts(x3, norm_ffn[1:2], ids[:, :TOP_K], w_gu_moe[0], w_down_moe[0], tm=768)
    y = _combine(x3, y2.reshape(n_p + n_s, TOP_K * d), wts, norm_final.reshape(1, d))

    return (y[:n_p].reshape(batch, seq, d), y[n_p:].reshape(n_seq, t_dec, d),
            k_win_p[None], v_win_p[None], k_win_s[None], v_win_s[None], s_p[None], s_s[None])
```

```python
import functools
import math

import jax
import jax.numpy as jnp
from jax import lax
from jax.experimental import pallas as pl
from jax.experimental.pallas import tpu as pltpu

F32 = jnp.float32
BF16 = jnp.bfloat16
HIGHEST = lax.Precision.HIGHEST

NORM_EPS = 1e-5
WINDOW = 128
PAST_LEN = 16384
ROPE_THETA = 10000.0
HG_DK = 128
HG_CHUNK = 64
HG_SAFE_DECAY = 60.0
HG_SAFE_Q = 1e9
TOP_K = 2
LANES = 128
SUBLANES = 8
VMEM_LIMIT = 56 * 1024 * 1024

NT_DIMS = (((1,), (1,)), ((), ()))
TN_DIMS = (((0,), (0,)), ((), ()))


def _pick_tile(n, target):
    for t in (1536, 1024, 768, 512, 384, 256, 192, 128, 64, 32, 16, 8):
        if t <= target and n % t == 0:
            return t
    raise ValueError(f"no row tile for {n}")


def _params(*sem):
    return pltpu.CompilerParams(dimension_semantics=sem, vmem_limit_bytes=VMEM_LIMIT)


def _rms(x, g):
    return x * lax.rsqrt(jnp.mean(x * x, axis=-1, keepdims=True) + NORM_EPS) * g


def _split_maps(npt):
    return (lambda i, *_: (jnp.minimum(i, npt - 1), 0)), (lambda i, *_: (jnp.maximum(i - npt, 0), 0))


def _store_split(is_prompt, ref_p, ref_s, val):
    @pl.when(is_prompt)
    def _():
        ref_p[...] = val

    @pl.when(jnp.logical_not(is_prompt))
    def _():
        ref_s[...] = val


def _qkv_body(xp_ref, xs_ref, g_ref, w_ref, b_ref, cos_ref, sin_ref, x_ref, q_ref, k_ref, v_ref, wbf_ref,
              *, npt, n_q, n_k, hd):
    i = pl.program_id(0)

    @pl.when(i == 0)
    def _():
        wbf_ref[...] = w_ref[...].astype(BF16)

    x = jnp.where(i < npt, xp_ref[...], xs_ref[...])
    x_ref[...] = x
    h = _rms(x, g_ref[...]).astype(BF16)
    y = jnp.dot(h, wbf_ref[...], preferred_element_type=F32) + b_ref[...]
    cos = cos_ref[...]
    sin = sin_ref[...]
    lane = lax.broadcasted_iota(jnp.int32, cos.shape, 1)
    first = (lane % hd) < (hd // 2)

    def rope(blk):
        partner = jnp.where(first, pltpu.roll(blk, LANES - hd // 2, 1), pltpu.roll(blk, hd // 2, 1))
        return blk * cos + partner * sin

    scale = hd ** -0.5
    for j in range(n_q // LANES):
        q_ref[:, j * LANES:(j + 1) * LANES] = (rope(y[:, j * LANES:(j + 1) * LANES]) * scale).astype(BF16)
    for j in range(n_k // LANES):
        k_ref[:, j * LANES:(j + 1) * LANES] = rope(y[:, n_q + j * LANES:n_q + (j + 1) * LANES])
    v_ref[...] = y[:, n_q + n_k:]


def _qkv_rope(xp, xs, g, w, b, cos_t, sin_t, n_q, n_k, hd):
    (n_p, d), n_s = xp.shape, xs.shape[0]
    n = n_p + n_s
    n_out = w.shape[1]
    tm = _pick_tile(math.gcd(n_p, n_s), 512)
    npt = n_p // tm
    row = lambda i: (i, 0)
    fix = lambda i: (0, 0)
    rp, rs = _split_maps(npt)
    return pl.pallas_call(
        functools.partial(_qkv_body, npt=npt, n_q=n_q, n_k=n_k, hd=hd),
        grid=(n // tm,),
        in_specs=[pl.BlockSpec((tm, d), rp), pl.BlockSpec((tm, d), rs), pl.BlockSpec((1, d), fix),
                  pl.BlockSpec((d, n_out), fix), pl.BlockSpec((1, n_out), fix),
                  pl.BlockSpec((tm, LANES), row), pl.BlockSpec((tm, LANES), row)],
        out_specs=[pl.BlockSpec((tm, d), row), pl.BlockSpec((tm, n_q), row), pl.BlockSpec((tm, n_k), row),
                   pl.BlockSpec((tm, n_k), row)],
        out_shape=[jax.ShapeDtypeStruct((n, d), F32), jax.ShapeDtypeStruct((n, n_q), BF16),
                   jax.ShapeDtypeStruct((n, n_k), F32), jax.ShapeDtypeStruct((n, n_k), F32)],
        scratch_shapes=[pltpu.VMEM((d, n_out), BF16)],
        compiler_params=_params("arbitrary"),
        name="qkv_rope",
    )(xp, xs, g, w, b, cos_t, sin_t)


def _sink_column(sink_ref, kh, g_per, rows_per):
    blk = lax.broadcasted_iota(jnp.int32, (g_per * rows_per, 1), 0) // rows_per
    col = jnp.full((g_per * rows_per, 1), sink_ref[kh * g_per], F32)
    for g in range(1, g_per):
        col = jnp.where(blk == g, sink_ref[kh * g_per + g], col)
    return col


def _stack_heads(q, kh, g_per, hd):
    return jnp.concatenate([q[:, (kh * g_per + g) * hd:(kh * g_per + g + 1) * hd] for g in range(g_per)], axis=0)


def _swa_prompt_body(sink_ref, q_ref, kp_ref, kc_ref, vp_ref, vc_ref, o_all_ref, o_ref, *, n_kv, g_per, hd, w):
    del o_all_ref
    n = pl.program_id(1)
    q = q_ref[...]
    kk = jnp.concatenate([kp_ref[...], kc_ref[...]], axis=0).astype(BF16)
    vv = jnp.concatenate([vp_ref[...], vc_ref[...]], axis=0).astype(BF16)
    i = lax.broadcasted_iota(jnp.int32, (g_per * w, 2 * w), 0) % w
    j = lax.broadcasted_iota(jnp.int32, (g_per * w, 2 * w), 1)
    mask = (j > i) & (j <= i + w) & ((j >= w) | (n > 0))
    outs = []
    for kh in range(n_kv):
        q4 = _stack_heads(q, kh, g_per, hd)
        s = lax.dot_general(q4, kk[:, kh * hd:(kh + 1) * hd], NT_DIMS, preferred_element_type=F32)
        s = jnp.where(mask, s, -jnp.inf)
        sink = _sink_column(sink_ref, kh, g_per, w)
        m = jnp.maximum(jnp.max(s, axis=-1, keepdims=True), sink)
        p = jnp.exp(s - m)
        p = p / (jnp.sum(p, axis=-1, keepdims=True) + jnp.exp(sink - m))
        o4 = jnp.dot(p.astype(BF16), vv[:, kh * hd:(kh + 1) * hd], preferred_element_type=F32)
        outs.append(jnp.concatenate([o4[g * w:(g + 1) * w] for g in range(g_per)], axis=1))
    o_ref[...] = jnp.concatenate(outs, axis=1).astype(BF16)


def _swa_prompt(q, k, v, sinks, o_all, batch, seq, n_kv, g_per, hd):
    w = WINDOW
    nb = seq // w
    dq = q.shape[1]
    dkv = k.shape[1]
    cur = lambda b, n: (b * nb + n, 0)
    prev = lambda b, n: (b * nb + jnp.maximum(n - 1, 0), 0)
    return pl.pallas_call(
        functools.partial(_swa_prompt_body, n_kv=n_kv, g_per=g_per, hd=hd, w=w),
        grid=(batch, nb),
        in_specs=[pl.BlockSpec(memory_space=pltpu.SMEM),
                  pl.BlockSpec((w, dq), cur), pl.BlockSpec((w, dkv), prev), pl.BlockSpec((w, dkv), cur),
                  pl.BlockSpec((w, dkv), prev), pl.BlockSpec((w, dkv), cur),
                  pl.BlockSpec(memory_space=pl.ANY)],
        out_specs=pl.BlockSpec((w, dq), cur),
        out_shape=jax.ShapeDtypeStruct(o_all.shape, o_all.dtype),
        input_output_aliases={6: 0},
        compiler_params=_params("arbitrary", "arbitrary"),
        name="swa_prompt",
    )(sinks, q, k, k, v, v, o_all)


def _swa_sample_body(sink_ref, q_ref, kn_ref, vn_ref, kc_ref, vc_ref, o_all_ref, o_ref, *, n_kv, g_per, hd, bt, t, wc):
    del o_all_ref
    r = bt * t
    q = q_ref[...]
    kn = kn_ref[...].astype(BF16)
    vn = vn_ref[...].astype(BF16)
    kc = kc_ref[...].reshape(bt * wc, n_kv * hd).astype(BF16)
    vc = vc_ref[...].reshape(bt * wc, n_kv * hd).astype(BF16)
    row_c = lax.broadcasted_iota(jnp.int32, (g_per * r, bt * wc), 0) % r
    col_c = lax.broadcasted_iota(jnp.int32, (g_per * r, bt * wc), 1)
    mask_c = (col_c // wc == row_c // t) & (col_c % wc > row_c % t + (wc - WINDOW))
    row_n = lax.broadcasted_iota(jnp.int32, (g_per * r, r), 0) % r
    col_n = lax.broadcasted_iota(jnp.int32, (g_per * r, r), 1)
    mask_n = (col_n // t == row_n // t) & (col_n % t <= row_n % t)
    outs = []
    for kh in range(n_kv):
        hs = slice(kh * hd, (kh + 1) * hd)
        q4 = _stack_heads(q, kh, g_per, hd)
        sc = jnp.where(mask_c, lax.dot_general(q4, kc[:, hs], NT_DIMS, preferred_element_type=F32), -jnp.inf)
        sn = jnp.where(mask_n, lax.dot_general(q4, kn[:, hs], NT_DIMS, preferred_element_type=F32), -jnp.inf)
        sink = _sink_column(sink_ref, kh, g_per, r)
        m = jnp.maximum(jnp.maximum(jnp.max(sc, axis=-1, keepdims=True), jnp.max(sn, axis=-1, keepdims=True)), sink)
        pc = jnp.exp(sc - m)
        pn = jnp.exp(sn - m)
        den = jnp.sum(pc, axis=-1, keepdims=True) + jnp.sum(pn, axis=-1, keepdims=True) + jnp.exp(sink - m)
        o4 = (jnp.dot((pc / den).astype(BF16), vc[:, hs], preferred_element_type=F32)
              + jnp.dot((pn / den).astype(BF16), vn[:, hs], preferred_element_type=F32))
        outs.append(jnp.concatenate([o4[g * r:(g + 1) * r] for g in range(g_per)], axis=1))
    o_ref[...] = jnp.concatenate(outs, axis=1).astype(BF16)


def _swa_sample(q, k, v, k_cache, v_cache, sinks, o_all, row0, n_seq, t, n_kv, g_per, hd):
    wc = k_cache.shape[1]
    dq = q.shape[1]
    dkv = k.shape[1]
    bt = 8 if n_seq % 8 == 0 else n_seq
    r = bt * t
    assert row0 % r == 0
    off = row0 // r
    rows = lambda i: (off + i, 0)
    return pl.pallas_call(
        functools.partial(_swa_sample_body, n_kv=n_kv, g_per=g_per, hd=hd, bt=bt, t=t, wc=wc),
        grid=(n_seq // bt,),
        in_specs=[pl.BlockSpec(memory_space=pltpu.SMEM),
                  pl.BlockSpec((r, dq), rows), pl.BlockSpec((r, dkv), rows), pl.BlockSpec((r, dkv), rows),
                  pl.BlockSpec((bt, wc, dkv), lambda i: (i, 0, 0)), pl.BlockSpec((bt, wc, dkv), lambda i: (i, 0, 0)),
                  pl.BlockSpec(memory_space=pl.ANY)],
        out_specs=pl.BlockSpec((r, dq), rows),
        out_shape=jax.ShapeDtypeStruct(o_all.shape, o_all.dtype),
        input_output_aliases={6: 0},
        compiler_params=_params("arbitrary"),
        name="swa_sample",
    )(sinks, q, k, v, k_cache, v_cache, o_all)


def _swiglu_step(h, wg_ref, wu_ref, wd_ref, acc_ref):
    a = jnp.dot(h, wg_ref[...].astype(BF16), preferred_element_type=F32)
    b = jnp.dot(h, wu_ref[...].astype(BF16), preferred_element_type=F32)
    act = (a * jax.nn.sigmoid(a) * b).astype(BF16)
    acc_ref[...] += jnp.dot(act, wd_ref[...].astype(BF16), preferred_element_type=F32)


def _oproj_ffn_body(o_ref, x_ref, wo_ref, bo_ref, g_ref, wg_ref, wu_ref, wd_ref, out_ref,
                    wo_bf, x1_ref, h_ref, acc_ref):
    i = pl.program_id(0)
    c = pl.program_id(1)

    @pl.when((i == 0) & (c == 0))
    def _():
        wo_bf[...] = wo_ref[...].astype(BF16)

    @pl.when(c == 0)
    def _():
        x1 = x_ref[...] + jnp.dot(o_ref[...], wo_bf[...], preferred_element_type=F32) + bo_ref[...]
        x1_ref[...] = x1
        h_ref[...] = _rms(x1, g_ref[...]).astype(BF16)
        acc_ref[...] = jnp.zeros_like(acc_ref)

    _swiglu_step(h_ref[...], wg_ref, wu_ref, wd_ref, acc_ref)

    @pl.when(c == pl.num_programs(1) - 1)
    def _():
        out_ref[...] = x1_ref[...] + acc_ref[...]


def _ff_chunk(d_ff):
    for tf in (512, 256, 128):
        if d_ff % tf == 0:
            return tf
    raise ValueError(f"d_ff {d_ff} is not a multiple of {LANES}")


def _oproj_ffn(o, x, wo, bo, g, w_gu, w_down):
    n, d = x.shape
    d_ff = w_down.shape[0]
    tf = _ff_chunk(d_ff)
    nc = d_ff // tf
    tm = _pick_tile(n, 768)
    row = lambda i, c: (i, 0)
    fix = lambda i, c: (0, 0)
    return pl.pallas_call(
        _oproj_ffn_body,
        grid=(n // tm, nc),
        in_specs=[pl.BlockSpec((tm, o.shape[1]), row), pl.BlockSpec((tm, d), row),
                  pl.BlockSpec(wo.shape, fix), pl.BlockSpec((1, d), fix), pl.BlockSpec((1, d), fix),
                  pl.BlockSpec((d, tf), lambda i, c: (0, c)), pl.BlockSpec((d, tf), lambda i, c: (0, c + nc)),
                  pl.BlockSpec((tf, d), lambda i, c: (c, 0))],
        out_specs=pl.BlockSpec((tm, d), row),
        out_shape=jax.ShapeDtypeStruct((n, d), F32),
        scratch_shapes=[pltpu.VMEM(wo.shape, BF16), pltpu.VMEM((tm, d), F32), pltpu.VMEM((tm, d), BF16),
                        pltpu.VMEM((tm, d), F32)],
        compiler_params=_params("arbitrary", "arbitrary"),
        name="oproj_ffn",
    )(o, x, wo, bo, g, w_gu, w_gu, w_down)


def _hg_in_body(x_ref, g_ref, w_ref, lb_ref, *refs, npt, part):
    out_refs, (wbf_ref, tmp_ref) = refs[:-2], refs[-2:]
    i = pl.program_id(0)
    is_p = i < npt

    @pl.when(i == 0)
    def _():
        wbf_ref[...] = w_ref[...].astype(BF16)

    h = _rms(x_ref[...], g_ref[...]).astype(BF16)
    z = jnp.dot(h, wbf_ref[...], preferred_element_type=F32)

    def emit(ref_p, ref_s, val):
        tmp_ref[...] = val
        _store_split(is_p, ref_p, ref_s, tmp_ref[...])

    if part == "q":
        emit(out_refs[0], out_refs[1], z * jax.nn.sigmoid(z) * (HG_DK ** -0.5))
    elif part == "f":
        lb = lb_ref[...]
        emit(out_refs[0], out_refs[2], jnp.logaddexp(jnp.log(lb), jnp.log1p(-lb) + jax.nn.log_sigmoid(z)))
        emit(out_refs[1], out_refs[3], (1.0 - lb) * jax.nn.sigmoid(-z))
    elif part == "i":
        emit(out_refs[0], out_refs[1], z)
    else:
        emit(out_refs[0], out_refs[1], z * jax.nn.sigmoid(z))


def _hg_inproj_part(x, g, w_in, lb, n_p, part):
    n, d = x.shape
    n_s = n - n_p
    col = "qfig".index(part)
    n_res = 2 if part == "f" else 1
    tm = _pick_tile(math.gcd(n_p, n_s), 512)
    npt = n_p // tm
    row = lambda i: (i, 0)
    fix = lambda i: (0, 0)
    rp, rs = _split_maps(npt)
    return pl.pallas_call(
        functools.partial(_hg_in_body, npt=npt, part=part),
        grid=(n // tm,),
        in_specs=[pl.BlockSpec((tm, d), row), pl.BlockSpec((1, d), fix), pl.BlockSpec((d, d), lambda i: (0, col)),
                  pl.BlockSpec((1, d), fix)],
        out_specs=[pl.BlockSpec((tm, d), rp)] * n_res + [pl.BlockSpec((tm, d), rs)] * n_res,
        out_shape=[jax.ShapeDtypeStruct((n_p, d), F32)] * n_res + [jax.ShapeDtypeStruct((n_s, d), F32)] * n_res,
        scratch_shapes=[pltpu.VMEM((d, d), BF16), pltpu.VMEM((tm, d), F32)],
        compiler_params=_params("arbitrary"),
        name=f"hgrn_inproj_{part}",
    )(x, g, w_in, lb)


def _hg_core_body(q_ref, k_ref, lf_ref, v_ref, s0_ref, o_ref, sout_ref, st_ref, g_ref, oi_ref,
                  *, c_len, n_chunk, n_seq, n_heads):
    c = pl.program_id(1)
    dk = HG_DK
    rb = c_len * n_chunk
    units = [(s, j) for s in range(n_seq) for j in range(n_chunk)]

    @pl.when(c == 0)
    def _():
        for s in range(n_seq):
            for h in range(n_heads):
                st_ref[s, h] = s0_ref[s, h].T

    r = lax.broadcasted_iota(jnp.int32, (rb, rb), 0)
    cidx = lax.broadcasted_iota(jnp.int32, (rb, rb), 1)
    block_causal = ((r >= cidx) & (r // c_len == cidx // c_len)).astype(F32)
    causal = (lax.broadcasted_iota(jnp.int32, (c_len, c_len), 0) >= lax.broadcasted_iota(jnp.int32, (c_len, c_len), 1))

    def rows(j):
        return slice(j * c_len, (j + 1) * c_len)

    gcum, safe = [], None
    for s in range(n_seq):
        gs = jnp.dot(block_causal, lf_ref[s], precision=HIGHEST, preferred_element_type=F32)
        gcum.append(gs)
        for j in range(n_chunk):
            ok = ((jnp.max(-gs[(j + 1) * c_len - 1:(j + 1) * c_len, :]) <= HG_SAFE_DECAY)
                  & (jnp.max(jnp.abs(q_ref[s, rows(j), :])) <= HG_SAFE_Q))
            safe = ok if safe is None else (safe & ok)

    def g_mid(s, j):
        return gcum[s][j * c_len + c_len // 2 - 1:j * c_len + c_len // 2, :]

    def g_last(s, j):
        return gcum[s][(j + 1) * c_len - 1:(j + 1) * c_len, :]

    @pl.when(safe)
    def _():
        for u, (s, j) in enumerate(units):
            g = gcum[s][rows(j), :]
            qi = (q_ref[s, rows(j), :] * jnp.exp(g - g_mid(s, j))).astype(BF16)
            ki = (k_ref[s, rows(j), :] * jnp.exp(g_mid(s, j) - g)).astype(BF16)
            vb = v_ref[s, rows(j), :].astype(BF16)
            for h in range(n_heads):
                hs = slice(h * dk, (h + 1) * dk)
                a = lax.dot_general(qi[:, hs], ki[:, hs], NT_DIMS, preferred_element_type=F32)
                a = jnp.where(causal, a, 0.0).astype(BF16)
                oi_ref[u, :, hs] = jnp.dot(a, vb[:, hs], preferred_element_type=F32)

    @pl.when(jnp.logical_not(safe))
    def _():
        lane_h = lax.broadcasted_iota(jnp.int32, (n_heads * dk, n_heads * dk), 0) // dk
        lane_w = lax.broadcasted_iota(jnp.int32, (n_heads * dk, n_heads * dk), 1) // dk
        head_sum = (lane_h == lane_w).astype(BF16)
        t_idx = lax.broadcasted_iota(jnp.int32, (c_len, 1), 0)
        for u, (s, j) in enumerate(units):
            g = gcum[s][rows(j), :]
            g_ref[...] = g
            q = q_ref[s, rows(j), :]

            def key_row(i, acc, s=s, j=j, g=g, q=q):
                gi = g_ref[pl.ds(i, 1), :]
                decay = jnp.exp(jnp.where(t_idx >= i, g - gi, -jnp.inf))
                term = (q * decay * k_ref[s, pl.ds(j * c_len + i, 1), :]).astype(BF16)
                a_i = jnp.dot(term, head_sum, preferred_element_type=F32)
                return acc + a_i * v_ref[s, pl.ds(j * c_len + i, 1), :]

            oi_ref[u] = lax.fori_loop(0, c_len, key_row, jnp.zeros((c_len, n_heads * dk), F32))

    for u, (s, j) in enumerate(units):
        g = gcum[s][rows(j), :]
        qs = (q_ref[s, rows(j), :] * jnp.exp(g)).astype(BF16)
        ks = (k_ref[s, rows(j), :] * jnp.exp(g_last(s, j) - g)).astype(BF16)
        vb = v_ref[s, rows(j), :].astype(BF16)
        e_last = jnp.exp(g_last(s, j))
        for h in range(n_heads):
            hs = slice(h * dk, (h + 1) * dk)
            st = st_ref[s, h]
            o_ref[s, rows(j), hs] = oi_ref[u, :, hs] + lax.dot_general(qs[:, hs], st.astype(BF16), NT_DIMS,
                                                                       preferred_element_type=F32)
            st_ref[s, h] = e_last[:, hs] * st + lax.dot_general(vb[:, hs], ks[:, hs], TN_DIMS,
                                                                preferred_element_type=F32)

    @pl.when(c == pl.num_programs(1) - 1)
    def _():
        for s in range(n_seq):
            for h in range(n_heads):
                sout_ref[s, h] = st_ref[s, h].T


def _hg_core(q, k, lf, v, s0, c_len, n_chunk, n_seq):
    batch, t, d = q.shape
    n_heads = d // HG_DK
    rb = c_len * n_chunk
    assert t % rb == 0 and batch % n_seq == 0
    rows = lambda b, c: (b, c, 0)
    state = lambda b, c: (b, 0, 0, 0)
    return pl.pallas_call(
        functools.partial(_hg_core_body, c_len=c_len, n_chunk=n_chunk, n_seq=n_seq, n_heads=n_heads),
        grid=(batch // n_seq, t // rb),
        in_specs=[pl.BlockSpec((n_seq, rb, d), rows)] * 4 + [pl.BlockSpec((n_seq, n_heads, HG_DK, HG_DK), state)],
        out_specs=[pl.BlockSpec((n_seq, rb, d), rows), pl.BlockSpec((n_seq, n_heads, HG_DK, HG_DK), state)],
        out_shape=[jax.ShapeDtypeStruct((batch, t, d), F32),
                   jax.ShapeDtypeStruct((batch, n_heads, HG_DK, HG_DK), F32)],
        scratch_shapes=[pltpu.VMEM((n_seq, n_heads, HG_DK, HG_DK), F32), pltpu.VMEM((c_len, d), F32),
                        pltpu.VMEM((n_seq * n_chunk, c_len, d), F32)],
        compiler_params=_params("arbitrary", "arbitrary"),
        name=f"hgrn_core_{c_len}",
    )(q, k, lf, v, s0)


def _hg_out_router_body(op_ref, os_ref, gp_ref, gs_ref, x_ref, gn_ref, wo_ref, g_ref, wr_ref, br_ref,
                        x3_ref, ids_ref, wts_ref, wo_bf, *, npt, n_heads, n_exp):
    i = pl.program_id(0)

    @pl.when(i == 0)
    def _():
        wo_bf[...] = wo_ref[...].astype(BF16)

    dk = HG_DK
    o = jnp.where(i < npt, op_ref[...], os_ref[...])
    gate = jnp.where(i < npt, gp_ref[...], gs_ref[...])
    gn = gn_ref[...]
    normed = jnp.concatenate([_rms(o[:, h * dk:(h + 1) * dk], gn) for h in range(n_heads)], axis=1)
    y = (normed * gate).astype(BF16)
    x3 = x_ref[...] + jnp.dot(y, wo_bf[...], preferred_element_type=F32)
    x3_ref[...] = x3
    h4 = _rms(x3, g_ref[...])
    logits = jnp.dot(h4, wr_ref[...], precision=HIGHEST, preferred_element_type=F32) + br_ref[...]
    lane = lax.broadcasted_iota(jnp.int32, logits.shape, 1)
    logits = jnp.where(lane < n_exp, logits, -jnp.inf)
    m1 = jnp.max(logits, axis=-1, keepdims=True)
    i1 = jnp.min(jnp.where(logits == m1, lane, LANES), axis=-1, keepdims=True)
    rest = jnp.where(lane == i1, -jnp.inf, logits)
    m2 = jnp.max(rest, axis=-1, keepdims=True)
    i2 = jnp.min(jnp.where(rest == m2, lane, LANES), axis=-1, keepdims=True)
    e2 = jnp.exp(m2 - m1)
    den = 1.0 + e2
    ids_ref[...] = jnp.where(lane == 0, i1, jnp.where(lane == 1, i2, 0))
    wts_ref[...] = jnp.where(lane == 0, 1.0 / den, jnp.where(lane == 1, e2 / den, 0.0))


def _hg_out_router(o_p, o_s, gate_p, gate_s, x, gn, wo, g, w_router, b_router, n_exp):
    n, d = x.shape
    n_p, n_s = o_p.shape[0], o_s.shape[0]
    tm = _pick_tile(math.gcd(n_p, n_s), 512)
    npt = n_p // tm
    row = lambda i: (i, 0)
    fix = lambda i: (0, 0)
    rp, rs = _split_maps(npt)
    return pl.pallas_call(
        functools.partial(_hg_out_router_body, npt=npt, n_heads=d // HG_DK, n_exp=n_exp),
        grid=(n // tm,),
        in_specs=[pl.BlockSpec((tm, d), rp), pl.BlockSpec((tm, d), rs), pl.BlockSpec((tm, d), rp),
                  pl.BlockSpec((tm, d), rs), pl.BlockSpec((tm, d), row),
                  pl.BlockSpec((1, HG_DK), fix), pl.BlockSpec(wo.shape, fix), pl.BlockSpec((1, d), fix),
                  pl.BlockSpec((d, LANES), fix), pl.BlockSpec((1, LANES), fix)],
        out_specs=[pl.BlockSpec((tm, d), row), pl.BlockSpec((tm, LANES), row), pl.BlockSpec((tm, LANES), row)],
        out_shape=[jax.ShapeDtypeStruct((n, d), F32), jax.ShapeDtypeStruct((n, LANES), jnp.int32),
                   jax.ShapeDtypeStruct((n, LANES), F32)],
        scratch_shapes=[pltpu.VMEM(wo.shape, BF16)],
        compiler_params=_params("arbitrary"),
        name="hgrn_out_router",
    )(o_p, o_s, gate_p, gate_s, x, gn, wo, g, w_router, b_router)


def _moe_body(te_ref, nv_ref, rows_ref, x_hbm, g_ref, wg_ref, wu_ref, wd_ref, y_hbm,
              xg_ref, h_ref, acc_ref, sem_in, sem_out, *, tm, n_tok):
    del te_ref
    i = pl.program_id(0)
    c = pl.program_id(1)
    nv = nv_ref[i]

    def row_in(j, token):
        return pltpu.make_async_copy(x_hbm.at[pl.ds(token, 1)], xg_ref.at[pl.ds(j, 1)], sem_in)

    def row_out(j, asg):
        return pltpu.make_async_copy(xg_ref.at[pl.ds(j, 1)], y_hbm.at[pl.ds(asg, 1)], sem_out)

    @pl.when(nv > 0)
    def _():
        @pl.when(c == 0)
        def _():
            def start(j, carry):
                a = rows_ref[0, 0, j]
                row_in(j, jnp.where(a >= n_tok, a - n_tok, a)).start()
                return carry

            def wait(j, carry):
                row_in(j, 0).wait()
                return carry

            lax.fori_loop(0, tm, start, 0, unroll=8)
            lax.fori_loop(0, tm, wait, 0, unroll=8)
            h_ref[...] = _rms(xg_ref[...], g_ref[...]).astype(BF16)
            acc_ref[...] = jnp.zeros_like(acc_ref)

        _swiglu_step(h_ref[...], wg_ref, wu_ref, wd_ref, acc_ref)

        @pl.when(c == pl.num_programs(1) - 1)
        def _():
            xg_ref[...] = acc_ref[...]

            def start(j, carry):
                row_out(j, rows_ref[0, 0, j]).start()
                return carry

            def wait(j, carry):
                row_out(j, 0).wait()
                return carry

            lax.fori_loop(0, nv, start, 0)
            lax.fori_loop(0, nv, wait, 0)


def _moe_experts(x, g, ids, w_gu, w_down, tm):
    n, d = x.shape
    n_exp, d_ff = w_down.shape[0], w_down.shape[1]
    tf = _ff_chunk(d_ff)
    nc = d_ff // tf
    n_asg = n * TOP_K
    n_tiles = (n_asg + n_exp * (tm - 1) + tm - 1) // tm

    e_flat = ids.T.reshape(n_asg)
    onehot = (e_flat[:, None] == jnp.arange(n_exp, dtype=jnp.int32)[None, :]).astype(jnp.int32)
    counts = jnp.sum(onehot, axis=0)
    rank = jnp.sum((jnp.cumsum(onehot, axis=0) - onehot) * onehot, axis=1)
    tiles_per = (counts + tm - 1) // tm
    tile_end = jnp.cumsum(tiles_per)
    tile_start = tile_end - tiles_per
    slot = jnp.sum(onehot * tile_start[None, :], axis=1) * tm + rank
    rows = jnp.zeros((n_tiles * tm,), jnp.int32).at[slot].set(jnp.arange(n_asg, dtype=jnp.int32),
                                                               unique_indices=True)
    tile = jnp.arange(n_tiles, dtype=jnp.int32)
    last = tile_end[-1] - 1
    tile_c = jnp.minimum(tile, last)
    te = jnp.minimum(jnp.sum((tile_c[:, None] >= tile_end[None, :]).astype(jnp.int32), axis=1), n_exp - 1)
    te_hot = (te[:, None] == jnp.arange(n_exp, dtype=jnp.int32)[None, :]).astype(jnp.int32)
    cnt_t = jnp.sum(te_hot * counts[None, :], axis=1)
    start_t = jnp.sum(te_hot * tile_start[None, :], axis=1)
    nv = jnp.where(tile <= last, jnp.clip(cnt_t - (tile - start_t) * tm, 0, tm), 0).astype(jnp.int32)

    def chunk(i, c, te_ref, nv_ref):
        return jnp.where(nv_ref[i] > 0, c, nc - 1)

    return pl.pallas_call(
        functools.partial(_moe_body, tm=tm, n_tok=n),
        grid_spec=pltpu.PrefetchScalarGridSpec(
            num_scalar_prefetch=2,
            grid=(n_tiles, nc),
            in_specs=[pl.BlockSpec((1, 1, tm), lambda i, c, te_ref, nv_ref: (i, 0, 0), memory_space=pltpu.SMEM),
                      pl.BlockSpec(memory_space=pl.ANY),
                      pl.BlockSpec((1, d), lambda i, c, te_ref, nv_ref: (0, 0)),
                      pl.BlockSpec((None, d, tf), lambda i, c, te_ref, nv_ref: (te_ref[i], 0, chunk(i, c, te_ref, nv_ref))),
                      pl.BlockSpec((None, d, tf),
                                   lambda i, c, te_ref, nv_ref: (te_ref[i], 0, chunk(i, c, te_ref, nv_ref) + nc)),
                      pl.BlockSpec((None, tf, d), lambda i, c, te_ref, nv_ref: (te_ref[i], chunk(i, c, te_ref, nv_ref), 0))],
            out_specs=pl.BlockSpec(memory_space=pl.ANY),
            scratch_shapes=[pltpu.VMEM((tm, d), F32), pltpu.VMEM((tm, d), BF16), pltpu.VMEM((tm, d), F32),
                            pltpu.SemaphoreType.DMA, pltpu.SemaphoreType.DMA]),
        out_shape=jax.ShapeDtypeStruct((n_asg, d), F32),
        compiler_params=_params("arbitrary", "arbitrary"),
        name="moe_experts",
    )(te.astype(jnp.int32), nv, rows.reshape(n_tiles, 1, tm), x, g, w_gu, w_gu, w_down)


def _combine_body(x_ref, y0_ref, y1_ref, wts_ref, g_ref, outp_ref, outs_ref, *, npt):
    wts = wts_ref[...]
    moe = wts[:, 0:1] * y0_ref[...] + wts[:, 1:2] * y1_ref[...]
    _store_split(pl.program_id(0) < npt, outp_ref, outs_ref, _rms(x_ref[...] + moe, g_ref[...]))


def _combine(x, y2, wts, g, n_p):
    n, d = x.shape
    n_s = n - n_p
    tm = _pick_tile(math.gcd(n_p, n_s), 512)
    nt = n // tm
    npt = n_p // tm
    row = lambda i: (i, 0)
    rp, rs = _split_maps(npt)
    return pl.pallas_call(
        functools.partial(_combine_body, npt=npt),
        grid=(nt,),
        in_specs=[pl.BlockSpec((tm, d), row), pl.BlockSpec((tm, d), row), pl.BlockSpec((tm, d), lambda i: (i + nt, 0)),
                  pl.BlockSpec((tm, LANES), row), pl.BlockSpec((1, d), lambda i: (0, 0))],
        out_specs=[pl.BlockSpec((tm, d), rp), pl.BlockSpec((tm, d), rs)],
        out_shape=[jax.ShapeDtypeStruct((n_p, d), F32), jax.ShapeDtypeStruct((n_s, d), F32)],
        compiler_params=_params("arbitrary"),
        name="moe_combine",
    )(x, y2, y2, wts, g)


def _rope_tables(pos, hd):
    half = hd // 2
    inv = ROPE_THETA ** (-jnp.arange(half, dtype=F32) / half)
    ang = pos.astype(F32)[:, None] * inv[None, :]
    cos = jnp.cos(ang)
    sin = jnp.sin(ang)
    reps = LANES // hd
    return jnp.tile(jnp.concatenate([cos, cos], axis=1), (1, reps)), jnp.tile(jnp.concatenate([-sin, sin], axis=1), (1, reps))


def kernel(x_prompt, x_sample, cache_k_win, cache_v_win, state_hgrn, norm_mix, norm_ffn, norm_final,
           w_qkv, b_qkv, w_o_attn, b_o_attn, sinks, w_in_hg, hg_lower, hg_norm, w_o_hg,
           w_gu_dense, w_down_dense, w_router, b_router, w_gu_moe, w_down_moe):
    batch, seq, d = x_prompt.shape
    n_seq, t_dec, _ = x_sample.shape
    n_kv, hd = cache_k_win.shape[3], cache_k_win.shape[4]
    n_heads = sinks.shape[1]
    g_per = n_heads // n_kv
    n_q, n_k = n_heads * hd, n_kv * hd
    n_p, n_s = batch * seq, n_seq * t_dec
    n_exp = w_router.shape[2]
    wc = cache_k_win.shape[2]
    assert norm_mix.shape[0] == 2 and hd * 2 == LANES and d % HG_DK == 0 and seq % WINDOW == 0 and wc == WINDOW

    pos = jnp.concatenate([jnp.tile(jnp.arange(seq), batch), jnp.tile(PAST_LEN + jnp.arange(t_dec), n_seq)])
    cos_t, sin_t = _rope_tables(pos, hd)
    x, q, k, v = _qkv_rope(x_prompt.reshape(n_p, d), x_sample.reshape(n_s, d), norm_mix[0:1], w_qkv[0], b_qkv[0:1],
                           cos_t, sin_t, n_q, n_k, hd)
    o = _swa_prompt(q, k, v, sinks[0], jnp.zeros((n_p + n_s, n_q), BF16), batch, seq, n_kv, g_per, hd)
    kc = cache_k_win[0].reshape(n_seq, wc, n_k)
    vc = cache_v_win[0].reshape(n_seq, wc, n_k)
    o = _swa_sample(q, k, v, kc, vc, sinks[0], o, n_p, n_seq, t_dec, n_kv, g_per, hd)
    last_win = lambda a: jnp.stack([a[(b + 1) * seq - wc:(b + 1) * seq] for b in range(batch)]).reshape(batch, wc, n_kv, hd)
    k_win_p, v_win_p = last_win(k), last_win(v)
    k_win_s = jnp.concatenate([kc[:, t_dec:], k[n_p:].reshape(n_seq, t_dec, n_k)], axis=1).reshape(n_seq, wc, n_kv, hd)
    v_win_s = jnp.concatenate([vc[:, t_dec:], v[n_p:].reshape(n_seq, t_dec, n_k)], axis=1).reshape(n_seq, wc, n_kv, hd)
    x = _oproj_ffn(o, x, w_o_attn[0], b_o_attn[0:1], norm_ffn[0:1], w_gu_dense[0], w_down_dense[0])

    lb_sm = jax.nn.softmax(hg_lower.astype(F32), axis=0)
    lb = (jnp.cumsum(lb_sm, axis=0) - lb_sm[0])[1:2]
    in_part = functools.partial(_hg_inproj_part, x, norm_mix[1:2], w_in_hg[0], lb, n_p)
    hq, sq = in_part("q")
    hlf, hk, slf, sk = in_part("f")
    hv, sv = in_part("i")
    hgate, sgate = in_part("g")
    n_hh = d // HG_DK
    c_p = math.gcd(seq, HG_CHUNK)
    as_seq = lambda a: a.reshape(batch, seq, d)
    o_p, s_p = _hg_core(as_seq(hq), as_seq(hk), as_seq(hlf), as_seq(hv), jnp.zeros((batch, n_hh, HG_DK, HG_DK), F32),
                        c_p, 2 if seq % (2 * c_p) == 0 else 1, 2 if batch % 2 == 0 else 1)
    c_s = SUBLANES * ((t_dec + SUBLANES - 1) // SUBLANES)
    pad = lambda a: jnp.pad(a.reshape(n_seq, t_dec, d), ((0, 0), (0, c_s - t_dec), (0, 0)))
    o_s, s_s = _hg_core(pad(sq), pad(sk), pad(slf), pad(sv), state_hgrn[0], c_s, 1, 8 if n_seq % 8 == 0 else 1)
    o_s = o_s[:, :t_dec].reshape(n_s, d)

    wr = jnp.pad(w_router[0], ((0, 0), (0, LANES - n_exp)))
    br = jnp.pad(b_router[0:1], ((0, 0), (0, LANES - n_exp)))
    x3, ids, wts = _hg_out_router(o_p.reshape(n_p, d), o_s, hgate, sgate, x, hg_norm[0:1], w_o_hg[0], norm_ffn[1:2],
                                  wr, br, n_exp)
    y2 = _moe_experts(x3, norm_ffn[1:2], ids[:, :TOP_K], w_gu_moe[0], w_down_moe[0], tm=768)
    y_p, y_s = _combine(x3, y2, wts, norm_final.reshape(1, d), n_p)

    return (y_p.reshape(batch, seq, d), y_s.reshape(n_seq, t_dec, d),
            k_win_p[None], v_win_p[None], k_win_s[None], v_win_s[None], s_p[None], s_s[None])
```

```python
import functools
import math

import jax
import jax.numpy as jnp
from jax import lax
from jax.experimental import pallas as pl
from jax.experimental.pallas import tpu as pltpu

F32 = jnp.float32
BF16 = jnp.bfloat16

NORM_EPS = 1e-5
WINDOW = 128
PAST_LEN = 16384
ROPE_THETA = 10000.0
HG_DK = 128
HG_CHUNK = 64
HG_SAFE_DECAY = 60.0
HG_SAFE_Q = 1e9
TOP_K = 2
LANES = 128
SUBLANES = 8
VMEM_LIMIT = 56 * 1024 * 1024

NT_DIMS = (((1,), (1,)), ((), ()))
TN_DIMS = (((0,), (0,)), ((), ()))


def _pick_tile(n, target):
    for t in (1536, 1024, 768, 512, 384, 256, 192, 128, 64, 32, 16, 8):
        if t <= target and n % t == 0:
            return t
    raise ValueError(f"no row tile for {n}")


def _params(*sem):
    return pltpu.CompilerParams(dimension_semantics=sem, vmem_limit_bytes=VMEM_LIMIT)


def _rms(x, g):
    return x * lax.rsqrt(jnp.mean(x * x, axis=-1, keepdims=True) + NORM_EPS) * g


def _split_maps(npt):
    return (lambda i, *_: (jnp.minimum(i, npt - 1), 0)), (lambda i, *_: (jnp.maximum(i - npt, 0), 0))


def _store_split(is_prompt, ref_p, ref_s, val):
    @pl.when(is_prompt)
    def _():
        ref_p[...] = val

    @pl.when(jnp.logical_not(is_prompt))
    def _():
        ref_s[...] = val


def _qkv_body(xp_ref, xs_ref, g_ref, w_ref, b_ref, cos_ref, sin_ref, x_ref, q_ref, k_ref, v_ref, wbf_ref,
              *, npt, n_q, n_k, hd):
    i = pl.program_id(0)

    @pl.when(i == 0)
    def _():
        wbf_ref[...] = w_ref[...].astype(BF16)

    x = jnp.where(i < npt, xp_ref[...], xs_ref[...])
    x_ref[...] = x
    h = _rms(x, g_ref[...]).astype(BF16)
    y = jnp.dot(h, wbf_ref[...], preferred_element_type=F32) + b_ref[...]
    cos = cos_ref[...]
    sin = sin_ref[...]
    lane = lax.broadcasted_iota(jnp.int32, cos.shape, 1)
    first = (lane % hd) < (hd // 2)

    def rope(blk):
        partner = jnp.where(first, pltpu.roll(blk, LANES - hd // 2, 1), pltpu.roll(blk, hd // 2, 1))
        return blk * cos + partner * sin

    scale = hd ** -0.5
    for j in range(n_q // LANES):
        q_ref[:, j * LANES:(j + 1) * LANES] = (rope(y[:, j * LANES:(j + 1) * LANES]) * scale).astype(BF16)
    for j in range(n_k // LANES):
        k_ref[:, j * LANES:(j + 1) * LANES] = rope(y[:, n_q + j * LANES:n_q + (j + 1) * LANES])
    v_ref[...] = y[:, n_q + n_k:]


def _qkv_rope(xp, xs, g, w, b, cos_t, sin_t, n_q, n_k, hd):
    (n_p, d), n_s = xp.shape, xs.shape[0]
    n = n_p + n_s
    n_out = w.shape[1]
    tm = _pick_tile(math.gcd(n_p, n_s), 512)
    npt = n_p // tm
    row = lambda i: (i, 0)
    fix = lambda i: (0, 0)
    rp, rs = _split_maps(npt)
    return pl.pallas_call(
        functools.partial(_qkv_body, npt=npt, n_q=n_q, n_k=n_k, hd=hd),
        grid=(n // tm,),
        in_specs=[pl.BlockSpec((tm, d), rp), pl.BlockSpec((tm, d), rs), pl.BlockSpec((1, d), fix),
                  pl.BlockSpec((d, n_out), fix), pl.BlockSpec((1, n_out), fix),
                  pl.BlockSpec((tm, LANES), row), pl.BlockSpec((tm, LANES), row)],
        out_specs=[pl.BlockSpec((tm, d), row), pl.BlockSpec((tm, n_q), row), pl.BlockSpec((tm, n_k), row),
                   pl.BlockSpec((tm, n_k), row)],
        out_shape=[jax.ShapeDtypeStruct((n, d), F32), jax.ShapeDtypeStruct((n, n_q), BF16),
                   jax.ShapeDtypeStruct((n, n_k), F32), jax.ShapeDtypeStruct((n, n_k), F32)],
        scratch_shapes=[pltpu.VMEM((d, n_out), BF16)],
        compiler_params=_params("arbitrary"),
        name="qkv_rope",
    )(xp, xs, g, w, b, cos_t, sin_t)


def _sink_column(sink_ref, kh, g_per, rows_per):
    blk = lax.broadcasted_iota(jnp.int32, (g_per * rows_per, 1), 0) // rows_per
    col = jnp.full((g_per * rows_per, 1), sink_ref[kh * g_per], F32)
    for g in range(1, g_per):
        col = jnp.where(blk == g, sink_ref[kh * g_per + g], col)
    return col


def _stack_heads(q, kh, g_per, hd):
    return jnp.concatenate([q[:, (kh * g_per + g) * hd:(kh * g_per + g + 1) * hd] for g in range(g_per)], axis=0)


def _swa_prompt_body(sink_ref, q_ref, kp_ref, kc_ref, vp_ref, vc_ref, o_all_ref, o_ref, *, n_kv, g_per, hd, w):
    del o_all_ref
    n = pl.program_id(1)
    q = q_ref[...]
    kk = jnp.concatenate([kp_ref[...], kc_ref[...]], axis=0).astype(BF16)
    vv = jnp.concatenate([vp_ref[...], vc_ref[...]], axis=0).astype(BF16)
    i = lax.broadcasted_iota(jnp.int32, (g_per * w, 2 * w), 0) % w
    j = lax.broadcasted_iota(jnp.int32, (g_per * w, 2 * w), 1)
    mask = (j > i) & (j <= i + w) & ((j >= w) | (n > 0))
    outs = []
    for kh in range(n_kv):
        q4 = _stack_heads(q, kh, g_per, hd)
        s = lax.dot_general(q4, kk[:, kh * hd:(kh + 1) * hd], NT_DIMS, preferred_element_type=F32)
        s = jnp.where(mask, s, -jnp.inf)
        sink = _sink_column(sink_ref, kh, g_per, w)
        m = jnp.maximum(jnp.max(s, axis=-1, keepdims=True), sink)
        p = jnp.exp(s - m)
        p = p / (jnp.sum(p, axis=-1, keepdims=True) + jnp.exp(sink - m))
        o4 = jnp.dot(p.astype(BF16), vv[:, kh * hd:(kh + 1) * hd], preferred_element_type=F32)
        outs.append(jnp.concatenate([o4[g * w:(g + 1) * w] for g in range(g_per)], axis=1))
    o_ref[...] = jnp.concatenate(outs, axis=1).astype(BF16)


def _swa_prompt(q, k, v, sinks, o_all, batch, seq, n_kv, g_per, hd):
    w = WINDOW
    nb = seq // w
    dq = q.shape[1]
    dkv = k.shape[1]
    cur = lambda b, n: (b * nb + n, 0)
    prev = lambda b, n: (b * nb + jnp.maximum(n - 1, 0), 0)
    return pl.pallas_call(
        functools.partial(_swa_prompt_body, n_kv=n_kv, g_per=g_per, hd=hd, w=w),
        grid=(batch, nb),
        in_specs=[pl.BlockSpec(memory_space=pltpu.SMEM),
                  pl.BlockSpec((w, dq), cur), pl.BlockSpec((w, dkv), prev), pl.BlockSpec((w, dkv), cur),
                  pl.BlockSpec((w, dkv), prev), pl.BlockSpec((w, dkv), cur),
                  pl.BlockSpec(memory_space=pl.ANY)],
        out_specs=pl.BlockSpec((w, dq), cur),
        out_shape=jax.ShapeDtypeStruct(o_all.shape, o_all.dtype),
        input_output_aliases={6: 0},
        compiler_params=_params("arbitrary", "arbitrary"),
        name="swa_prompt",
    )(sinks, q, k, k, v, v, o_all)


def _swa_sample_body(sink_ref, q_ref, kn_ref, vn_ref, kc_ref, vc_ref, o_all_ref, o_ref, *, n_kv, g_per, hd, bt, t, wc):
    del o_all_ref
    r = bt * t
    q = q_ref[...]
    kn = kn_ref[...].astype(BF16)
    vn = vn_ref[...].astype(BF16)
    kc = kc_ref[...].reshape(bt * wc, n_kv * hd).astype(BF16)
    vc = vc_ref[...].reshape(bt * wc, n_kv * hd).astype(BF16)
    row_c = lax.broadcasted_iota(jnp.int32, (g_per * r, bt * wc), 0) % r
    col_c = lax.broadcasted_iota(jnp.int32, (g_per * r, bt * wc), 1)
    mask_c = (col_c // wc == row_c // t) & (col_c % wc > row_c % t + (wc - WINDOW))
    row_n = lax.broadcasted_iota(jnp.int32, (g_per * r, r), 0) % r
    col_n = lax.broadcasted_iota(jnp.int32, (g_per * r, r), 1)
    mask_n = (col_n // t == row_n // t) & (col_n % t <= row_n % t)
    outs = []
    for kh in range(n_kv):
        hs = slice(kh * hd, (kh + 1) * hd)
        q4 = _stack_heads(q, kh, g_per, hd)
        sc = jnp.where(mask_c, lax.dot_general(q4, kc[:, hs], NT_DIMS, preferred_element_type=F32), -jnp.inf)
        sn = jnp.where(mask_n, lax.dot_general(q4, kn[:, hs], NT_DIMS, preferred_element_type=F32), -jnp.inf)
        sink = _sink_column(sink_ref, kh, g_per, r)
        m = jnp.maximum(jnp.maximum(jnp.max(sc, axis=-1, keepdims=True), jnp.max(sn, axis=-1, keepdims=True)), sink)
        pc = jnp.exp(sc - m)
        pn = jnp.exp(sn - m)
        den = jnp.sum(pc, axis=-1, keepdims=True) + jnp.sum(pn, axis=-1, keepdims=True) + jnp.exp(sink - m)
        o4 = (jnp.dot((pc / den).astype(BF16), vc[:, hs], preferred_element_type=F32)
              + jnp.dot((pn / den).astype(BF16), vn[:, hs], preferred_element_type=F32))
        outs.append(jnp.concatenate([o4[g * r:(g + 1) * r] for g in range(g_per)], axis=1))
    o_ref[...] = jnp.concatenate(outs, axis=1).astype(BF16)


def _swa_sample(q, k, v, k_cache, v_cache, sinks, o_all, row0, n_seq, t, n_kv, g_per, hd):
    wc = k_cache.shape[1]
    dq = q.shape[1]
    dkv = k.shape[1]
    bt = 8 if n_seq % 8 == 0 else n_seq
    r = bt * t
    assert row0 % r == 0
    off = row0 // r
    rows = lambda i: (off + i, 0)
    return pl.pallas_call(
        functools.partial(_swa_sample_body, n_kv=n_kv, g_per=g_per, hd=hd, bt=bt, t=t, wc=wc),
        grid=(n_seq // bt,),
        in_specs=[pl.BlockSpec(memory_space=pltpu.SMEM),
                  pl.BlockSpec((r, dq), rows), pl.BlockSpec((r, dkv), rows), pl.BlockSpec((r, dkv), rows),
                  pl.BlockSpec((bt, wc, dkv), lambda i: (i, 0, 0)), pl.BlockSpec((bt, wc, dkv), lambda i: (i, 0, 0)),
                  pl.BlockSpec(memory_space=pl.ANY)],
        out_specs=pl.BlockSpec((r, dq), rows),
        out_shape=jax.ShapeDtypeStruct(o_all.shape, o_all.dtype),
        input_output_aliases={6: 0},
        compiler_params=_params("arbitrary"),
        name="swa_sample",
    )(sinks, q, k, v, k_cache, v_cache, o_all)


def _swiglu_step(h, wg_ref, wu_ref, wd_ref, acc_ref):
    a = jnp.dot(h, wg_ref[...].astype(BF16), preferred_element_type=F32)
    b = jnp.dot(h, wu_ref[...].astype(BF16), preferred_element_type=F32)
    act = (a * jax.nn.sigmoid(a) * b).astype(BF16)
    acc_ref[...] += jnp.dot(act, wd_ref[...].astype(BF16), preferred_element_type=F32)


def _oproj_ffn_body(o_ref, x_ref, wo_ref, bo_ref, g_ref, wg_ref, wu_ref, wd_ref, out_ref,
                    wo_bf, x1_ref, h_ref, acc_ref):
    i = pl.program_id(0)
    c = pl.program_id(1)

    @pl.when((i == 0) & (c == 0))
    def _():
        wo_bf[...] = wo_ref[...].astype(BF16)

    @pl.when(c == 0)
    def _():
        x1 = x_ref[...] + jnp.dot(o_ref[...], wo_bf[...], preferred_element_type=F32) + bo_ref[...]
        x1_ref[...] = x1
        h_ref[...] = _rms(x1, g_ref[...]).astype(BF16)
        acc_ref[...] = jnp.zeros_like(acc_ref)

    _swiglu_step(h_ref[...], wg_ref, wu_ref, wd_ref, acc_ref)

    @pl.when(c == pl.num_programs(1) - 1)
    def _():
        out_ref[...] = x1_ref[...] + acc_ref[...]


def _ff_chunk(d_ff):
    for tf in (512, 256, 128):
        if d_ff % tf == 0:
            return tf
    raise ValueError(f"d_ff {d_ff} is not a multiple of {LANES}")


def _oproj_ffn(o, x, wo, bo, g, w_gu, w_down):
    n, d = x.shape
    d_ff = w_down.shape[0]
    tf = _ff_chunk(d_ff)
    nc = d_ff // tf
    tm = _pick_tile(n, 768)
    row = lambda i, c: (i, 0)
    fix = lambda i, c: (0, 0)
    return pl.pallas_call(
        _oproj_ffn_body,
        grid=(n // tm, nc),
        in_specs=[pl.BlockSpec((tm, o.shape[1]), row), pl.BlockSpec((tm, d), row),
                  pl.BlockSpec(wo.shape, fix), pl.BlockSpec((1, d), fix), pl.BlockSpec((1, d), fix),
                  pl.BlockSpec((d, tf), lambda i, c: (0, c)), pl.BlockSpec((d, tf), lambda i, c: (0, c + nc)),
                  pl.BlockSpec((tf, d), lambda i, c: (c, 0))],
        out_specs=pl.BlockSpec((tm, d), row),
        out_shape=jax.ShapeDtypeStruct((n, d), F32),
        scratch_shapes=[pltpu.VMEM(wo.shape, BF16), pltpu.VMEM((tm, d), F32), pltpu.VMEM((tm, d), BF16),
                        pltpu.VMEM((tm, d), F32)],
        compiler_params=_params("arbitrary", "arbitrary"),
        name="oproj_ffn",
    )(o, x, wo, bo, g, w_gu, w_gu, w_down)


def _hg_in_body(x_ref, g_ref, w_ref, lb_ref, *refs, npt, part):
    out_refs, (wbf_ref, tmp_ref) = refs[:-2], refs[-2:]
    i = pl.program_id(0)
    is_p = i < npt

    @pl.when(i == 0)
    def _():
        wbf_ref[...] = w_ref[...].astype(BF16)

    h = _rms(x_ref[...], g_ref[...]).astype(BF16)
    z = jnp.dot(h, wbf_ref[...], preferred_element_type=F32)

    def emit(ref_p, ref_s, val):
        tmp_ref[...] = val
        _store_split(is_p, ref_p, ref_s, tmp_ref[...])

    if part == "q":
        emit(out_refs[0], out_refs[1], z * jax.nn.sigmoid(z) * (HG_DK ** -0.5))
    elif part == "f":
        lb = lb_ref[...]
        emit(out_refs[0], out_refs[2], jnp.logaddexp(jnp.log(lb), jnp.log1p(-lb) + jax.nn.log_sigmoid(z)))
        emit(out_refs[1], out_refs[3], (1.0 - lb) * jax.nn.sigmoid(-z))
    elif part == "i":
        emit(out_refs[0], out_refs[1], z)
    else:
        emit(out_refs[0], out_refs[1], z * jax.nn.sigmoid(z))


def _hg_inproj_part(x, g, w_in, lb, n_p, part):
    n, d = x.shape
    n_s = n - n_p
    col = "qfig".index(part)
    n_res = 2 if part == "f" else 1
    tm = _pick_tile(math.gcd(n_p, n_s), 512)
    npt = n_p // tm
    row = lambda i: (i, 0)
    fix = lambda i: (0, 0)
    rp, rs = _split_maps(npt)
    return pl.pallas_call(
        functools.partial(_hg_in_body, npt=npt, part=part),
        grid=(n // tm,),
        in_specs=[pl.BlockSpec((tm, d), row), pl.BlockSpec((1, d), fix), pl.BlockSpec((d, d), lambda i: (0, col)),
                  pl.BlockSpec((1, d), fix)],
        out_specs=[pl.BlockSpec((tm, d), rp)] * n_res + [pl.BlockSpec((tm, d), rs)] * n_res,
        out_shape=[jax.ShapeDtypeStruct((n_p, d), F32)] * n_res + [jax.ShapeDtypeStruct((n_s, d), F32)] * n_res,
        scratch_shapes=[pltpu.VMEM((d, d), BF16), pltpu.VMEM((tm, d), F32)],
        compiler_params=_params("arbitrary"),
        name=f"hgrn_inproj_{part}",
    )(x, g, w_in, lb)


def _hg_core_body(q_ref, k_ref, lf_ref, v_ref, s0_ref, o_ref, sout_ref, st_ref, g_ref, oi_ref,
                  *, c_len, n_chunk, n_seq, n_heads):
    c = pl.program_id(1)
    dk = HG_DK
    rb = c_len * n_chunk
    units = [(s, j) for s in range(n_seq) for j in range(n_chunk)]

    n_pairs = n_heads // 2
    state_diag = (lax.broadcasted_iota(jnp.int32, (2 * dk, 2 * dk), 0) // dk
                  == lax.broadcasted_iota(jnp.int32, (2 * dk, 2 * dk), 1) // dk)
    rows_diag = (lax.broadcasted_iota(jnp.int32, (2 * c_len, 2 * dk), 0) // c_len
                 == lax.broadcasted_iota(jnp.int32, (2 * c_len, 2 * dk), 1) // dk)

    def pair_tile(x):
        x2 = jnp.concatenate([x, x], axis=0)
        return jnp.where(rows_diag, x2, jnp.zeros_like(x2))

    @pl.when(c == 0)
    def _():
        for s in range(n_seq):
            for p in range(n_pairs):
                st_ref[s, p] = jnp.zeros((2 * dk, 2 * dk), F32)
                st_ref[s, p, 0:dk, 0:dk] = s0_ref[s, 2 * p].T
                st_ref[s, p, dk:2 * dk, dk:2 * dk] = s0_ref[s, 2 * p + 1].T

    r = lax.broadcasted_iota(jnp.int32, (rb, rb), 0)
    cidx = lax.broadcasted_iota(jnp.int32, (rb, rb), 1)
    block_causal = ((r >= cidx) & (r // c_len == cidx // c_len)).astype(BF16)
    causal2 = (lax.broadcasted_iota(jnp.int32, (c_len, 2 * c_len), 0)
               >= lax.broadcasted_iota(jnp.int32, (c_len, 2 * c_len), 1) % c_len)

    def rows(j):
        return slice(j * c_len, (j + 1) * c_len)

    gcum, safe = [], None
    for s in range(n_seq):
        lf = lf_ref[s]
        lf_hi = lf.astype(BF16)
        rest = lf - lf_hi.astype(F32)
        lf_mid = rest.astype(BF16)
        lf_lo = (rest - lf_mid.astype(F32)).astype(BF16)
        gs = ((jnp.dot(block_causal, lf_lo, preferred_element_type=F32)
               + jnp.dot(block_causal, lf_mid, preferred_element_type=F32))
              + jnp.dot(block_causal, lf_hi, preferred_element_type=F32))
        gcum.append(gs)
        for j in range(n_chunk):
            ok = ((jnp.max(-gs[(j + 1) * c_len - 1:(j + 1) * c_len, :]) <= HG_SAFE_DECAY)
                  & (jnp.max(jnp.abs(q_ref[s, rows(j), :])) <= HG_SAFE_Q))
            safe = ok if safe is None else (safe & ok)

    def g_mid(s, j):
        return gcum[s][j * c_len + c_len // 2 - 1:j * c_len + c_len // 2, :]

    def g_last(s, j):
        return gcum[s][(j + 1) * c_len - 1:(j + 1) * c_len, :]

    @pl.when(safe)
    def _():
        for u, (s, j) in enumerate(units):
            g = gcum[s][rows(j), :]
            qi = (q_ref[s, rows(j), :] * jnp.exp(g - g_mid(s, j))).astype(BF16)
            ki = (k_ref[s, rows(j), :] * jnp.exp(g_mid(s, j) - g)).astype(BF16)
            vb = v_ref[s, rows(j), :].astype(BF16)
            for p in range(n_pairs):
                ps = slice(2 * p * dk, 2 * (p + 1) * dk)
                a = lax.dot_general(qi[:, ps], pair_tile(ki[:, ps]), NT_DIMS, preferred_element_type=F32)
                a = jnp.where(causal2, a, 0.0).astype(BF16)
                oi_ref[u, :, ps] = jnp.dot(a, pair_tile(vb[:, ps]), preferred_element_type=F32)

    @pl.when(jnp.logical_not(safe))
    def _():
        lane_h = lax.broadcasted_iota(jnp.int32, (n_heads * dk, n_heads * dk), 0) // dk
        lane_w = lax.broadcasted_iota(jnp.int32, (n_heads * dk, n_heads * dk), 1) // dk
        head_sum = (lane_h == lane_w).astype(BF16)
        t_idx = lax.broadcasted_iota(jnp.int32, (c_len, 1), 0)
        for u, (s, j) in enumerate(units):
            g = gcum[s][rows(j), :]
            g_ref[...] = g
            q = q_ref[s, rows(j), :]

            def key_row(i, acc, s=s, j=j, g=g, q=q):
                gi = g_ref[pl.ds(i, 1), :]
                decay = jnp.exp(jnp.where(t_idx >= i, g - gi, -jnp.inf))
                term = (q * decay * k_ref[s, pl.ds(j * c_len + i, 1), :]).astype(BF16)
                a_i = jnp.dot(term, head_sum, preferred_element_type=F32)
                return acc + a_i * v_ref[s, pl.ds(j * c_len + i, 1), :]

            oi_ref[u] = lax.fori_loop(0, c_len, key_row, jnp.zeros((c_len, n_heads * dk), F32))

    for u, (s, j) in enumerate(units):
        g = gcum[s][rows(j), :]
        qs = (q_ref[s, rows(j), :] * jnp.exp(g)).astype(BF16)
        ks = (k_ref[s, rows(j), :] * jnp.exp(g_last(s, j) - g)).astype(BF16)
        vb = v_ref[s, rows(j), :].astype(BF16)
        e_last = jnp.exp(g_last(s, j))
        for p in range(n_pairs):
            ps = slice(2 * p * dk, 2 * (p + 1) * dk)
            st = st_ref[s, p]
            o_ref[s, rows(j), ps] = oi_ref[u, :, ps] + lax.dot_general(qs[:, ps], st.astype(BF16), NT_DIMS,
                                                                       preferred_element_type=F32)
            upd = lax.dot_general(vb[:, ps], ks[:, ps], TN_DIMS, preferred_element_type=F32)
            st_ref[s, p] = jnp.where(state_diag, e_last[:, ps] * st + upd, 0.0)

    @pl.when(c == pl.num_programs(1) - 1)
    def _():
        for s in range(n_seq):
            for p in range(n_pairs):
                sout_ref[s, 2 * p] = st_ref[s, p, 0:dk, 0:dk].T
                sout_ref[s, 2 * p + 1] = st_ref[s, p, dk:2 * dk, dk:2 * dk].T


def _hg_core(q, k, lf, v, s0, c_len, n_chunk, n_seq):
    batch, t, d = q.shape
    n_heads = d // HG_DK
    rb = c_len * n_chunk
    assert t % rb == 0 and batch % n_seq == 0 and n_heads % 2 == 0
    rows = lambda b, c: (b, c, 0)
    state = lambda b, c: (b, 0, 0, 0)
    return pl.pallas_call(
        functools.partial(_hg_core_body, c_len=c_len, n_chunk=n_chunk, n_seq=n_seq, n_heads=n_heads),
        grid=(batch // n_seq, t // rb),
        in_specs=[pl.BlockSpec((n_seq, rb, d), rows)] * 4 + [pl.BlockSpec((n_seq, n_heads, HG_DK, HG_DK), state)],
        out_specs=[pl.BlockSpec((n_seq, rb, d), rows), pl.BlockSpec((n_seq, n_heads, HG_DK, HG_DK), state)],
        out_shape=[jax.ShapeDtypeStruct((batch, t, d), F32),
                   jax.ShapeDtypeStruct((batch, n_heads, HG_DK, HG_DK), F32)],
        scratch_shapes=[pltpu.VMEM((n_seq, n_heads // 2, 2 * HG_DK, 2 * HG_DK), F32), pltpu.VMEM((c_len, d), F32),
                        pltpu.VMEM((n_seq * n_chunk, c_len, d), F32)],
        compiler_params=_params("arbitrary", "arbitrary"),
        name=f"hgrn_core_{c_len}",
    )(q, k, lf, v, s0)


def _hg_out_router_body(op_ref, os_ref, gp_ref, gs_ref, x_ref, gn_ref, wo_ref, g_ref, wr_ref, br_ref,
                        x3_ref, ids_ref, wts_ref, wo_bf, *, npt, n_heads, n_exp):
    i = pl.program_id(0)

    @pl.when(i == 0)
    def _():
        wo_bf[...] = wo_ref[...].astype(BF16)

    dk = HG_DK
    o = jnp.where(i < npt, op_ref[...], os_ref[...])
    gate = jnp.where(i < npt, gp_ref[...], gs_ref[...])
    gn = gn_ref[...]
    normed = jnp.concatenate([_rms(o[:, h * dk:(h + 1) * dk], gn) for h in range(n_heads)], axis=1)
    y = (normed * gate).astype(BF16)
    x3 = x_ref[...] + jnp.dot(y, wo_bf[...], preferred_element_type=F32)
    x3_ref[...] = x3
    h4 = _rms(x3, g_ref[...])
    h_hi = h4.astype(BF16)
    h_lo = (h4 - h_hi.astype(F32)).astype(BF16)
    wr = wr_ref[...]
    w_hi = wr.astype(BF16)
    w_lo = (wr - w_hi.astype(F32)).astype(BF16)
    logits = (jnp.dot(h_hi, w_hi, preferred_element_type=F32)
              + (jnp.dot(h_lo, w_hi, preferred_element_type=F32) + jnp.dot(h_hi, w_lo, preferred_element_type=F32))
              + br_ref[...])
    lane = lax.broadcasted_iota(jnp.int32, logits.shape, 1)
    logits = jnp.where(lane < n_exp, logits, -jnp.inf)
    m1 = jnp.max(logits, axis=-1, keepdims=True)
    i1 = jnp.min(jnp.where(logits == m1, lane, LANES), axis=-1, keepdims=True)
    rest = jnp.where(lane == i1, -jnp.inf, logits)
    m2 = jnp.max(rest, axis=-1, keepdims=True)
    i2 = jnp.min(jnp.where(rest == m2, lane, LANES), axis=-1, keepdims=True)
    e2 = jnp.exp(m2 - m1)
    den = 1.0 + e2
    ids_ref[...] = jnp.where(lane == 0, i1, jnp.where(lane == 1, i2, 0))
    wts_ref[...] = jnp.where(lane == 0, 1.0 / den, jnp.where(lane == 1, e2 / den, 0.0))


def _hg_out_router(o_p, o_s, gate_p, gate_s, x, gn, wo, g, w_router, b_router, n_exp):
    n, d = x.shape
    n_p, n_s = o_p.shape[0], o_s.shape[0]
    tm = _pick_tile(math.gcd(n_p, n_s), 512)
    npt = n_p // tm
    row = lambda i: (i, 0)
    fix = lambda i: (0, 0)
    rp, rs = _split_maps(npt)
    return pl.pallas_call(
        functools.partial(_hg_out_router_body, npt=npt, n_heads=d // HG_DK, n_exp=n_exp),
        grid=(n // tm,),
        in_specs=[pl.BlockSpec((tm, d), rp), pl.BlockSpec((tm, d), rs), pl.BlockSpec((tm, d), rp),
                  pl.BlockSpec((tm, d), rs), pl.BlockSpec((tm, d), row),
                  pl.BlockSpec((1, HG_DK), fix), pl.BlockSpec(wo.shape, fix), pl.BlockSpec((1, d), fix),
                  pl.BlockSpec((d, LANES), fix), pl.BlockSpec((1, LANES), fix)],
        out_specs=[pl.BlockSpec((tm, d), row), pl.BlockSpec((tm, LANES), row), pl.BlockSpec((tm, LANES), row)],
        out_shape=[jax.ShapeDtypeStruct((n, d), F32), jax.ShapeDtypeStruct((n, LANES), jnp.int32),
                   jax.ShapeDtypeStruct((n, LANES), F32)],
        scratch_shapes=[pltpu.VMEM(wo.shape, BF16)],
        compiler_params=_params("arbitrary"),
        name="hgrn_out_router",
    )(o_p, o_s, gate_p, gate_s, x, gn, wo, g, w_router, b_router)


def _moe_body(te_ref, nv_ref, last_ref, rin0_ref, rin_next_ref, rout_prev_ref, rout_cur_ref,
              x_hbm, g_ref, wg_ref, wu_ref, wd_ref, y_hbm,
              xg_ref, h_ref, acc_ref, sem_in, sem_out, *, tm, rows_per_step):
    del te_ref
    i = pl.program_id(0)
    c = pl.program_id(1)
    slot = i % 2
    other = 1 - slot

    def row_in(buf, j, token):
        return pltpu.make_async_copy(x_hbm.at[pl.ds(token, 1)], xg_ref.at[buf, pl.ds(j, 1)], sem_in)

    def row_out(buf, j, dst):
        return pltpu.make_async_copy(acc_ref.at[buf, pl.ds(j, 1)], y_hbm.at[pl.ds(dst, 1)], sem_out)

    def each_row(fn):
        def body(j, carry):
            fn(j)
            return carry
        lax.fori_loop(0, tm, body, 0, unroll=8)

    @pl.when(nv_ref[i] > 0)
    def _():
        @pl.when((i == 0) & (c == 0))
        def _():
            acc_ref[1] = jnp.zeros(acc_ref.shape[1:], F32)
            each_row(lambda j: row_in(0, j, rin0_ref[0, 0, j]).start())
            each_row(lambda j: row_in(0, j, 0).wait())

        @pl.when(c == 0)
        def _():
            h_ref[...] = _rms(xg_ref[slot], g_ref[...]).astype(BF16)
            acc_ref[slot] = jnp.zeros(acc_ref.shape[1:], F32)

        for u in range(rows_per_step):
            j = c * rows_per_step + u
            row_in(other, j, rin_next_ref[0, 0, j]).start()
            row_out(other, j, rout_prev_ref[0, 0, j]).start()
        _swiglu_step(h_ref[...], wg_ref, wu_ref, wd_ref, acc_ref.at[slot])

        @pl.when(c == pl.num_programs(1) - 1)
        def _():
            each_row(lambda j: row_in(other, j, 0).wait())
            each_row(lambda j: row_out(other, j, 0).wait())

            @pl.when(i == last_ref[0])
            def _():
                each_row(lambda j: row_out(slot, j, rout_cur_ref[0, 0, j]).start())
                each_row(lambda j: row_out(slot, j, 0).wait())


MOE_TILE_ROWS = 768
BF16_SUBLANES = 16


def _moe_experts(x, g, ids, w_gu, w_down):
    n, d = x.shape
    n_exp, d_ff = w_down.shape[0], w_down.shape[1]
    tf = _ff_chunk(d_ff)
    nc = d_ff // tf
    unit = BF16_SUBLANES * nc
    tm = unit * max(1, round(MOE_TILE_ROWS / unit))
    n_asg = n * TOP_K
    n_tiles = (n_asg + n_exp * (tm - 1) + tm - 1) // tm

    e_flat = ids.T.reshape(n_asg)
    onehot = (e_flat[:, None] == jnp.arange(n_exp, dtype=jnp.int32)[None, :]).astype(jnp.int32)
    counts = jnp.sum(onehot, axis=0)
    rank = jnp.sum((jnp.cumsum(onehot, axis=0) - onehot) * onehot, axis=1)
    tiles_per = (counts + tm - 1) // tm
    tile_end = jnp.cumsum(tiles_per)
    tile_start = tile_end - tiles_per
    slot = jnp.sum(onehot * tile_start[None, :], axis=1) * tm + rank
    asg = jnp.full((n_tiles * tm,), -1, jnp.int32).at[slot].set(jnp.arange(n_asg, dtype=jnp.int32),
                                                                 unique_indices=True)
    pad_row = n_asg + jnp.arange(n_tiles * tm, dtype=jnp.int32) % tm
    rows_in = jnp.where(asg >= 0, jnp.where(asg >= n, asg - n, asg), 0)
    rows_out = jnp.where(asg >= 0, asg, pad_row)
    rows_in_ext = jnp.concatenate([rows_in, jnp.zeros((tm,), jnp.int32)]).reshape(n_tiles + 1, 1, tm)
    rows_out_ext = jnp.concatenate([pad_row[:tm], rows_out]).reshape(n_tiles + 1, 1, tm)
    tile = jnp.arange(n_tiles, dtype=jnp.int32)
    last = tile_end[-1] - 1
    tile_c = jnp.minimum(tile, last)
    te = jnp.minimum(jnp.sum((tile_c[:, None] >= tile_end[None, :]).astype(jnp.int32), axis=1), n_exp - 1)
    te_hot = (te[:, None] == jnp.arange(n_exp, dtype=jnp.int32)[None, :]).astype(jnp.int32)
    cnt_t = jnp.sum(te_hot * counts[None, :], axis=1)
    start_t = jnp.sum(te_hot * tile_start[None, :], axis=1)
    nv = jnp.where(tile <= last, jnp.clip(cnt_t - (tile - start_t) * tm, 0, tm), 0).astype(jnp.int32)

    def chunk(i, c, nv_ref):
        return jnp.where(nv_ref[i] > 0, c, nc - 1)

    smem_rows = lambda index: pl.BlockSpec((1, 1, tm), index, memory_space=pltpu.SMEM)
    return pl.pallas_call(
        functools.partial(_moe_body, tm=tm, rows_per_step=tm // nc),
        grid_spec=pltpu.PrefetchScalarGridSpec(
            num_scalar_prefetch=3,
            grid=(n_tiles, nc),
            in_specs=[smem_rows(lambda i, c, te_ref, nv_ref, last_ref: (0, 0, 0)),
                      smem_rows(lambda i, c, te_ref, nv_ref, last_ref: (i + 1, 0, 0)),
                      smem_rows(lambda i, c, te_ref, nv_ref, last_ref: (i, 0, 0)),
                      smem_rows(lambda i, c, te_ref, nv_ref, last_ref: (i + 1, 0, 0)),
                      pl.BlockSpec(memory_space=pl.ANY),
                      pl.BlockSpec((1, d), lambda i, c, te_ref, nv_ref, last_ref: (0, 0)),
                      pl.BlockSpec((None, d, tf), lambda i, c, te_ref, nv_ref, last_ref: (te_ref[i], 0, chunk(i, c, nv_ref))),
                      pl.BlockSpec((None, d, tf),
                                   lambda i, c, te_ref, nv_ref, last_ref: (te_ref[i], 0, chunk(i, c, nv_ref) + nc)),
                      pl.BlockSpec((None, tf, d), lambda i, c, te_ref, nv_ref, last_ref: (te_ref[i], chunk(i, c, nv_ref), 0))],
            out_specs=pl.BlockSpec(memory_space=pl.ANY),
            scratch_shapes=[pltpu.VMEM((2, tm, d), F32), pltpu.VMEM((tm, d), BF16), pltpu.VMEM((2, tm, d), F32),
                            pltpu.SemaphoreType.DMA, pltpu.SemaphoreType.DMA]),
        out_shape=jax.ShapeDtypeStruct((n_asg + tm, d), F32),
        compiler_params=_params("arbitrary", "arbitrary"),
        name="moe_experts",
    )(te.astype(jnp.int32), nv, last.reshape(1).astype(jnp.int32), rows_in_ext, rows_in_ext, rows_out_ext, rows_out_ext,
      x, g, w_gu, w_gu, w_down)


def _combine_body(x_ref, y0_ref, y1_ref, wts_ref, g_ref, outp_ref, outs_ref, *, npt):
    wts = wts_ref[...]
    moe = wts[:, 0:1] * y0_ref[...] + wts[:, 1:2] * y1_ref[...]
    _store_split(pl.program_id(0) < npt, outp_ref, outs_ref, _rms(x_ref[...] + moe, g_ref[...]))


def _combine(x, y2, wts, g, n_p):
    n, d = x.shape
    n_s = n - n_p
    tm = _pick_tile(math.gcd(n_p, n_s), 512)
    nt = n // tm
    npt = n_p // tm
    row = lambda i: (i, 0)
    rp, rs = _split_maps(npt)
    return pl.pallas_call(
        functools.partial(_combine_body, npt=npt),
        grid=(nt,),
        in_specs=[pl.BlockSpec((tm, d), row), pl.BlockSpec((tm, d), row), pl.BlockSpec((tm, d), lambda i: (i + nt, 0)),
                  pl.BlockSpec((tm, LANES), row), pl.BlockSpec((1, d), lambda i: (0, 0))],
        out_specs=[pl.BlockSpec((tm, d), rp), pl.BlockSpec((tm, d), rs)],
        out_shape=[jax.ShapeDtypeStruct((n_p, d), F32), jax.ShapeDtypeStruct((n_s, d), F32)],
        compiler_params=_params("arbitrary"),
        name="moe_combine",
    )(x, y2, y2, wts, g)


def _rope_tables(pos, hd):
    half = hd // 2
    inv = ROPE_THETA ** (-jnp.arange(half, dtype=F32) / half)
    ang = pos.astype(F32)[:, None] * inv[None, :]
    cos = jnp.cos(ang)
    sin = jnp.sin(ang)
    reps = LANES // hd
    return jnp.tile(jnp.concatenate([cos, cos], axis=1), (1, reps)), jnp.tile(jnp.concatenate([-sin, sin], axis=1), (1, reps))


def kernel(x_prompt, x_sample, cache_k_win, cache_v_win, state_hgrn, norm_mix, norm_ffn, norm_final,
           w_qkv, b_qkv, w_o_attn, b_o_attn, sinks, w_in_hg, hg_lower, hg_norm, w_o_hg,
           w_gu_dense, w_down_dense, w_router, b_router, w_gu_moe, w_down_moe):
    batch, seq, d = x_prompt.shape
    n_seq, t_dec, _ = x_sample.shape
    n_kv, hd = cache_k_win.shape[3], cache_k_win.shape[4]
    n_heads = sinks.shape[1]
    g_per = n_heads // n_kv
    n_q, n_k = n_heads * hd, n_kv * hd
    n_p, n_s = batch * seq, n_seq * t_dec
    n_exp = w_router.shape[2]
    wc = cache_k_win.shape[2]
    assert norm_mix.shape[0] == 2 and hd * 2 == LANES and d % HG_DK == 0 and seq % WINDOW == 0 and wc == WINDOW

    pos = jnp.concatenate([jnp.tile(jnp.arange(seq), batch), jnp.tile(PAST_LEN + jnp.arange(t_dec), n_seq)])
    cos_t, sin_t = _rope_tables(pos, hd)
    x, q, k, v = _qkv_rope(x_prompt.reshape(n_p, d), x_sample.reshape(n_s, d), norm_mix[0:1], w_qkv[0], b_qkv[0:1],
                           cos_t, sin_t, n_q, n_k, hd)
    o = _swa_prompt(q, k, v, sinks[0], jnp.zeros((n_p + n_s, n_q), BF16), batch, seq, n_kv, g_per, hd)
    kc = cache_k_win[0].reshape(n_seq, wc, n_k)
    vc = cache_v_win[0].reshape(n_seq, wc, n_k)
    o = _swa_sample(q, k, v, kc, vc, sinks[0], o, n_p, n_seq, t_dec, n_kv, g_per, hd)
    last_win = lambda a: jnp.stack([a[(b + 1) * seq - wc:(b + 1) * seq] for b in range(batch)]).reshape(batch, wc, n_kv, hd)
    k_win_p, v_win_p = last_win(k), last_win(v)
    k_win_s = jnp.concatenate([kc[:, t_dec:], k[n_p:].reshape(n_seq, t_dec, n_k)], axis=1).reshape(n_seq, wc, n_kv, hd)
    v_win_s = jnp.concatenate([vc[:, t_dec:], v[n_p:].reshape(n_seq, t_dec, n_k)], axis=1).reshape(n_seq, wc, n_kv, hd)
    x = _oproj_ffn(o, x, w_o_attn[0], b_o_attn[0:1], norm_ffn[0:1], w_gu_dense[0], w_down_dense[0])

    lb_sm = jax.nn.softmax(hg_lower.astype(F32), axis=0)
    lb = (jnp.cumsum(lb_sm, axis=0) - lb_sm[0])[1:2]
    in_part = functools.partial(_hg_inproj_part, x, norm_mix[1:2], w_in_hg[0], lb, n_p)
    hq, sq = in_part("q")
    hlf, hk, slf, sk = in_part("f")
    hv, sv = in_part("i")
    hgate, sgate = in_part("g")
    n_hh = d // HG_DK
    c_p = math.gcd(seq, HG_CHUNK)
    as_seq = lambda a: a.reshape(batch, seq, d)
    o_p, s_p = _hg_core(as_seq(hq), as_seq(hk), as_seq(hlf), as_seq(hv), jnp.zeros((batch, n_hh, HG_DK, HG_DK), F32),
                        c_p, 2 if seq % (2 * c_p) == 0 else 1, 2 if batch % 2 == 0 else 1)
    c_s = SUBLANES * ((t_dec + SUBLANES - 1) // SUBLANES)
    pad = lambda a: jnp.pad(a.reshape(n_seq, t_dec, d), ((0, 0), (0, c_s - t_dec), (0, 0)))
    o_s, s_s = _hg_core(pad(sq), pad(sk), pad(slf), pad(sv), state_hgrn[0], c_s, 1, 8 if n_seq % 8 == 0 else 1)
    o_s = o_s[:, :t_dec].reshape(n_s, d)

    wr = jnp.pad(w_router[0], ((0, 0), (0, LANES - n_exp)))
    br = jnp.pad(b_router[0:1], ((0, 0), (0, LANES - n_exp)))
    x3, ids, wts = _hg_out_router(o_p.reshape(n_p, d), o_s, hgate, sgate, x, hg_norm[0:1], w_o_hg[0], norm_ffn[1:2],
                                  wr, br, n_exp)
    y2 = _moe_experts(x3, norm_ffn[1:2], ids[:, :TOP_K], w_gu_moe[0], w_down_moe[0])
    y_p, y_s = _combine(x3, y2, wts, norm_final.reshape(1, d), n_p)

    return (y_p.reshape(batch, seq, d), y_s.reshape(n_seq, t_dec, d),
            k_win_p[None], v_win_p[None], k_win_s[None], v_win_s[None], s_p[None], s_s[None])
```

```python
import functools
import math

import jax
import jax.numpy as jnp
from jax import lax
from jax.experimental import pallas as pl
from jax.experimental.pallas import tpu as pltpu

F32 = jnp.float32
BF16 = jnp.bfloat16

NORM_EPS = 1e-5
WINDOW = 128
PAST_LEN = 16384
ROPE_THETA = 10000.0
HG_DK = 128
HG_CHUNK = 64
HG_SAFE_DECAY = 60.0
HG_SAFE_Q = 1e9
TOP_K = 2
LANES = 128
SUBLANES = 8
VMEM_LIMIT = 56 * 1024 * 1024

NT_DIMS = (((1,), (1,)), ((), ()))
TN_DIMS = (((0,), (0,)), ((), ()))


def _pick_tile(n, target):
    for t in (1536, 1024, 768, 512, 384, 256, 192, 128, 64, 32, 16, 8):
        if t <= target and n % t == 0:
            return t
    raise ValueError(f"no row tile for {n}")


def _params(*sem):
    return pltpu.CompilerParams(dimension_semantics=sem, vmem_limit_bytes=VMEM_LIMIT)


def _rms(x, g):
    return x * lax.rsqrt(jnp.mean(x * x, axis=-1, keepdims=True) + NORM_EPS) * g


def _split_maps(npt):
    return (lambda i, *_: (jnp.minimum(i, npt - 1), 0)), (lambda i, *_: (jnp.maximum(i - npt, 0), 0))


def _store_split(is_prompt, ref_p, ref_s, val):
    @pl.when(is_prompt)
    def _():
        ref_p[...] = val

    @pl.when(jnp.logical_not(is_prompt))
    def _():
        ref_s[...] = val


def _qkv_body(xp_ref, xs_ref, g_ref, w_ref, b_ref, cos_ref, sin_ref, x_ref, q_ref, k_ref, v_ref, wbf_ref,
              *, npt, n_q, n_k, hd):
    i = pl.program_id(0)

    @pl.when(i == 0)
    def _():
        wbf_ref[...] = w_ref[...].astype(BF16)

    x = jnp.where(i < npt, xp_ref[...], xs_ref[...])
    x_ref[...] = x
    h = _rms(x, g_ref[...]).astype(BF16)
    y = jnp.dot(h, wbf_ref[...], preferred_element_type=F32) + b_ref[...]
    cos = cos_ref[...]
    sin = sin_ref[...]
    lane = lax.broadcasted_iota(jnp.int32, cos.shape, 1)
    first = (lane % hd) < (hd // 2)

    def rope(blk):
        partner = jnp.where(first, pltpu.roll(blk, LANES - hd // 2, 1), pltpu.roll(blk, hd // 2, 1))
        return blk * cos + partner * sin

    scale = hd ** -0.5
    for j in range(n_q // LANES):
        q_ref[:, j * LANES:(j + 1) * LANES] = (rope(y[:, j * LANES:(j + 1) * LANES]) * scale).astype(BF16)
    for j in range(n_k // LANES):
        k_ref[:, j * LANES:(j + 1) * LANES] = rope(y[:, n_q + j * LANES:n_q + (j + 1) * LANES])
    v_ref[...] = y[:, n_q + n_k:]


def _qkv_rope(xp, xs, g, w, b, cos_t, sin_t, n_q, n_k, hd):
    (n_p, d), n_s = xp.shape, xs.shape[0]
    n = n_p + n_s
    n_out = w.shape[1]
    tm = _pick_tile(math.gcd(n_p, n_s), 512)
    npt = n_p // tm
    row = lambda i: (i, 0)
    fix = lambda i: (0, 0)
    rp, rs = _split_maps(npt)
    return pl.pallas_call(
        functools.partial(_qkv_body, npt=npt, n_q=n_q, n_k=n_k, hd=hd),
        grid=(n // tm,),
        in_specs=[pl.BlockSpec((tm, d), rp), pl.BlockSpec((tm, d), rs), pl.BlockSpec((1, d), fix),
                  pl.BlockSpec((d, n_out), fix), pl.BlockSpec((1, n_out), fix),
                  pl.BlockSpec((tm, LANES), row), pl.BlockSpec((tm, LANES), row)],
        out_specs=[pl.BlockSpec((tm, d), row), pl.BlockSpec((tm, n_q), row), pl.BlockSpec((tm, n_k), row),
                   pl.BlockSpec((tm, n_k), row)],
        out_shape=[jax.ShapeDtypeStruct((n, d), F32), jax.ShapeDtypeStruct((n, n_q), BF16),
                   jax.ShapeDtypeStruct((n, n_k), F32), jax.ShapeDtypeStruct((n, n_k), F32)],
        scratch_shapes=[pltpu.VMEM((d, n_out), BF16)],
        compiler_params=_params("arbitrary"),
        name="qkv_rope",
    )(xp, xs, g, w, b, cos_t, sin_t)


def _sink_column(sink_ref, kh, g_per, rows_per):
    blk = lax.broadcasted_iota(jnp.int32, (g_per * rows_per, 1), 0) // rows_per
    col = jnp.full((g_per * rows_per, 1), sink_ref[kh * g_per], F32)
    for g in range(1, g_per):
        col = jnp.where(blk == g, sink_ref[kh * g_per + g], col)
    return col


def _stack_heads(q, kh, g_per, hd):
    return jnp.concatenate([q[:, (kh * g_per + g) * hd:(kh * g_per + g + 1) * hd] for g in range(g_per)], axis=0)


def _swa_prompt_body(sink_ref, q_ref, kp_ref, kc_ref, vp_ref, vc_ref, o_all_ref, o_ref, *, n_kv, g_per, hd, w):
    del o_all_ref
    n = pl.program_id(1)
    q = q_ref[...]
    kk = jnp.concatenate([kp_ref[...], kc_ref[...]], axis=0).astype(BF16)
    vv = jnp.concatenate([vp_ref[...], vc_ref[...]], axis=0).astype(BF16)
    i = lax.broadcasted_iota(jnp.int32, (g_per * w, 2 * w), 0) % w
    j = lax.broadcasted_iota(jnp.int32, (g_per * w, 2 * w), 1)
    mask = (j > i) & (j <= i + w) & ((j >= w) | (n > 0))
    outs = []
    for kh in range(n_kv):
        q4 = _stack_heads(q, kh, g_per, hd)
        s = lax.dot_general(q4, kk[:, kh * hd:(kh + 1) * hd], NT_DIMS, preferred_element_type=F32)
        s = jnp.where(mask, s, -jnp.inf)
        sink = _sink_column(sink_ref, kh, g_per, w)
        m = jnp.maximum(jnp.max(s, axis=-1, keepdims=True), sink)
        p = jnp.exp(s - m)
        p = p / (jnp.sum(p, axis=-1, keepdims=True) + jnp.exp(sink - m))
        o4 = jnp.dot(p.astype(BF16), vv[:, kh * hd:(kh + 1) * hd], preferred_element_type=F32)
        outs.append(jnp.concatenate([o4[g * w:(g + 1) * w] for g in range(g_per)], axis=1))
    o_ref[...] = jnp.concatenate(outs, axis=1).astype(BF16)


def _swa_prompt(q, k, v, sinks, o_all, batch, seq, n_kv, g_per, hd):
    w = WINDOW
    nb = seq // w
    dq = q.shape[1]
    dkv = k.shape[1]
    cur = lambda b, n: (b * nb + n, 0)
    prev = lambda b, n: (b * nb + jnp.maximum(n - 1, 0), 0)
    return pl.pallas_call(
        functools.partial(_swa_prompt_body, n_kv=n_kv, g_per=g_per, hd=hd, w=w),
        grid=(batch, nb),
        in_specs=[pl.BlockSpec(memory_space=pltpu.SMEM),
                  pl.BlockSpec((w, dq), cur), pl.BlockSpec((w, dkv), prev), pl.BlockSpec((w, dkv), cur),
                  pl.BlockSpec((w, dkv), prev), pl.BlockSpec((w, dkv), cur),
                  pl.BlockSpec(memory_space=pl.ANY)],
        out_specs=pl.BlockSpec((w, dq), cur),
        out_shape=jax.ShapeDtypeStruct(o_all.shape, o_all.dtype),
        input_output_aliases={6: 0},
        compiler_params=_params("arbitrary", "arbitrary"),
        name="swa_prompt",
    )(sinks, q, k, k, v, v, o_all)


def _swa_sample_body(sink_ref, q_ref, kn_ref, vn_ref, kc_ref, vc_ref, o_all_ref, o_ref, *, n_kv, g_per, hd, bt, t, wc):
    del o_all_ref
    r = bt * t
    q = q_ref[...]
    kn = kn_ref[...].astype(BF16)
    vn = vn_ref[...].astype(BF16)
    kc = kc_ref[...].reshape(bt * wc, n_kv * hd).astype(BF16)
    vc = vc_ref[...].reshape(bt * wc, n_kv * hd).astype(BF16)
    row_c = lax.broadcasted_iota(jnp.int32, (g_per * r, bt * wc), 0) % r
    col_c = lax.broadcasted_iota(jnp.int32, (g_per * r, bt * wc), 1)
    mask_c = (col_c // wc == row_c // t) & (col_c % wc > row_c % t + (wc - WINDOW))
    row_n = lax.broadcasted_iota(jnp.int32, (g_per * r, r), 0) % r
    col_n = lax.broadcasted_iota(jnp.int32, (g_per * r, r), 1)
    mask_n = (col_n // t == row_n // t) & (col_n % t <= row_n % t)
    outs = []
    for kh in range(n_kv):
        hs = slice(kh * hd, (kh + 1) * hd)
        q4 = _stack_heads(q, kh, g_per, hd)
        sc = jnp.where(mask_c, lax.dot_general(q4, kc[:, hs], NT_DIMS, preferred_element_type=F32), -jnp.inf)
        sn = jnp.where(mask_n, lax.dot_general(q4, kn[:, hs], NT_DIMS, preferred_element_type=F32), -jnp.inf)
        sink = _sink_column(sink_ref, kh, g_per, r)
        m = jnp.maximum(jnp.maximum(jnp.max(sc, axis=-1, keepdims=True), jnp.max(sn, axis=-1, keepdims=True)), sink)
        pc = jnp.exp(sc - m)
        pn = jnp.exp(sn - m)
        den = jnp.sum(pc, axis=-1, keepdims=True) + jnp.sum(pn, axis=-1, keepdims=True) + jnp.exp(sink - m)
        o4 = (jnp.dot((pc / den).astype(BF16), vc[:, hs], preferred_element_type=F32)
              + jnp.dot((pn / den).astype(BF16), vn[:, hs], preferred_element_type=F32))
        outs.append(jnp.concatenate([o4[g * r:(g + 1) * r] for g in range(g_per)], axis=1))
    o_ref[...] = jnp.concatenate(outs, axis=1).astype(BF16)


def _swa_sample(q, k, v, k_cache, v_cache, sinks, o_all, row0, n_seq, t, n_kv, g_per, hd):
    wc = k_cache.shape[1]
    dq = q.shape[1]
    dkv = k.shape[1]
    bt = 8 if n_seq % 8 == 0 else n_seq
    r = bt * t
    assert row0 % r == 0
    off = row0 // r
    rows = lambda i: (off + i, 0)
    return pl.pallas_call(
        functools.partial(_swa_sample_body, n_kv=n_kv, g_per=g_per, hd=hd, bt=bt, t=t, wc=wc),
        grid=(n_seq // bt,),
        in_specs=[pl.BlockSpec(memory_space=pltpu.SMEM),
                  pl.BlockSpec((r, dq), rows), pl.BlockSpec((r, dkv), rows), pl.BlockSpec((r, dkv), rows),
                  pl.BlockSpec((bt, wc, dkv), lambda i: (i, 0, 0)), pl.BlockSpec((bt, wc, dkv), lambda i: (i, 0, 0)),
                  pl.BlockSpec(memory_space=pl.ANY)],
        out_specs=pl.BlockSpec((r, dq), rows),
        out_shape=jax.ShapeDtypeStruct(o_all.shape, o_all.dtype),
        input_output_aliases={6: 0},
        compiler_params=_params("arbitrary"),
        name="swa_sample",
    )(sinks, q, k, v, k_cache, v_cache, o_all)


def _swiglu_step(h, wg, wu, wd, acc_ref):
    a = jnp.dot(h, wg, preferred_element_type=F32)
    b = jnp.dot(h, wu, preferred_element_type=F32)
    act = (a * jax.nn.sigmoid(a) * b).astype(BF16)
    acc_ref[...] += jnp.dot(act, wd, preferred_element_type=F32)


def _oproj_ffn_body(o_ref, x_ref, wo_ref, bo_ref, g_ref, wg_ref, wu_ref, wd_ref, out_ref,
                    wo_bf, x1_ref, h_ref, acc_ref):
    i = pl.program_id(0)
    c = pl.program_id(1)

    @pl.when((i == 0) & (c == 0))
    def _():
        wo_bf[...] = wo_ref[...].astype(BF16)

    @pl.when(c == 0)
    def _():
        x1 = x_ref[...] + jnp.dot(o_ref[...], wo_bf[...], preferred_element_type=F32) + bo_ref[...]
        x1_ref[...] = x1
        h_ref[...] = _rms(x1, g_ref[...]).astype(BF16)
        acc_ref[...] = jnp.zeros_like(acc_ref)

    _swiglu_step(h_ref[...], wg_ref[...].astype(BF16), wu_ref[...].astype(BF16), wd_ref[...].astype(BF16), acc_ref)

    @pl.when(c == pl.num_programs(1) - 1)
    def _():
        out_ref[...] = x1_ref[...] + acc_ref[...]


def _ff_chunk(d_ff):
    for tf in (512, 256, 128):
        if d_ff % tf == 0:
            return tf
    raise ValueError(f"d_ff {d_ff} is not a multiple of {LANES}")


def _oproj_ffn(o, x, wo, bo, g, w_gu, w_down):
    n, d = x.shape
    d_ff = w_down.shape[0]
    tf = _ff_chunk(d_ff)
    nc = d_ff // tf
    tm = _pick_tile(n, 768)
    row = lambda i, c: (i, 0)
    fix = lambda i, c: (0, 0)
    return pl.pallas_call(
        _oproj_ffn_body,
        grid=(n // tm, nc),
        in_specs=[pl.BlockSpec((tm, o.shape[1]), row), pl.BlockSpec((tm, d), row),
                  pl.BlockSpec(wo.shape, fix), pl.BlockSpec((1, d), fix), pl.BlockSpec((1, d), fix),
                  pl.BlockSpec((d, tf), lambda i, c: (0, c)), pl.BlockSpec((d, tf), lambda i, c: (0, c + nc)),
                  pl.BlockSpec((tf, d), lambda i, c: (c, 0))],
        out_specs=pl.BlockSpec((tm, d), row),
        out_shape=jax.ShapeDtypeStruct((n, d), F32),
        scratch_shapes=[pltpu.VMEM(wo.shape, BF16), pltpu.VMEM((tm, d), F32), pltpu.VMEM((tm, d), BF16),
                        pltpu.VMEM((tm, d), F32)],
        compiler_params=_params("arbitrary", "arbitrary"),
        name="oproj_ffn",
    )(o, x, wo, bo, g, w_gu, w_gu, w_down)


def _hg_in_body(x_ref, g_ref, w_ref, lb_ref, *refs, npt, part):
    out_refs, (wbf_ref, tmp_ref) = refs[:-2], refs[-2:]
    i = pl.program_id(0)
    is_p = i < npt

    @pl.when(i == 0)
    def _():
        wbf_ref[...] = w_ref[...].astype(BF16)

    h = _rms(x_ref[...], g_ref[...]).astype(BF16)
    z = jnp.dot(h, wbf_ref[...], preferred_element_type=F32)

    def emit(ref_p, ref_s, val):
        tmp_ref[...] = val
        _store_split(is_p, ref_p, ref_s, tmp_ref[...])

    if part == "q":
        emit(out_refs[0], out_refs[1], z * jax.nn.sigmoid(z) * (HG_DK ** -0.5))
    elif part == "f":
        lb = lb_ref[...]
        t = jnp.exp(-jnp.abs(z))
        log_sig = jnp.minimum(z, 0.0) - jnp.log(1.0 + t)
        a = jnp.log(lb)
        b = jnp.log1p(-lb) + log_sig
        emit(out_refs[0], out_refs[2], jnp.maximum(a, b) + jnp.log(1.0 + jnp.exp(-jnp.abs(a - b))))
        emit(out_refs[1], out_refs[3], (1.0 - lb) * (jnp.where(z >= 0.0, t, 1.0) / (1.0 + t)))
    elif part == "i":
        emit(out_refs[0], out_refs[1], z)
    else:
        emit(out_refs[0], out_refs[1], z * jax.nn.sigmoid(z))


def _hg_inproj_part(x, g, w_in, lb, n_p, part):
    n, d = x.shape
    n_s = n - n_p
    col = "qfig".index(part)
    n_res = 2 if part == "f" else 1
    tm = _pick_tile(math.gcd(n_p, n_s), 512)
    npt = n_p // tm
    row = lambda i: (i, 0)
    fix = lambda i: (0, 0)
    rp, rs = _split_maps(npt)
    return pl.pallas_call(
        functools.partial(_hg_in_body, npt=npt, part=part),
        grid=(n // tm,),
        in_specs=[pl.BlockSpec((tm, d), row), pl.BlockSpec((1, d), fix), pl.BlockSpec((d, d), lambda i: (0, col)),
                  pl.BlockSpec((1, d), fix)],
        out_specs=[pl.BlockSpec((tm, d), rp)] * n_res + [pl.BlockSpec((tm, d), rs)] * n_res,
        out_shape=[jax.ShapeDtypeStruct((n_p, d), F32)] * n_res + [jax.ShapeDtypeStruct((n_s, d), F32)] * n_res,
        scratch_shapes=[pltpu.VMEM((d, d), BF16), pltpu.VMEM((tm, d), F32)],
        compiler_params=_params("arbitrary"),
        name=f"hgrn_inproj_{part}",
    )(x, g, w_in, lb)


def _hg_core_body(q_ref, k_ref, lf_ref, v_ref, s0_ref, o_ref, sout_ref, st_ref, g_ref, oi_ref,
                  *, c_len, n_chunk, n_seq, n_heads):
    c = pl.program_id(1)
    dk = HG_DK
    rb = c_len * n_chunk
    units = [(s, j) for s in range(n_seq) for j in range(n_chunk)]

    n_pairs = n_heads // 2
    state_diag = (lax.broadcasted_iota(jnp.int32, (2 * dk, 2 * dk), 0) // dk
                  == lax.broadcasted_iota(jnp.int32, (2 * dk, 2 * dk), 1) // dk)
    rows_diag = (lax.broadcasted_iota(jnp.int32, (2 * c_len, 2 * dk), 0) // c_len
                 == lax.broadcasted_iota(jnp.int32, (2 * c_len, 2 * dk), 1) // dk)

    def pair_tile(x):
        x2 = jnp.concatenate([x, x], axis=0)
        return jnp.where(rows_diag, x2, jnp.zeros_like(x2))

    @pl.when(c == 0)
    def _():
        for s in range(n_seq):
            for p in range(n_pairs):
                st_ref[s, p] = jnp.zeros((2 * dk, 2 * dk), F32)
                st_ref[s, p, 0:dk, 0:dk] = s0_ref[s, 2 * p].T
                st_ref[s, p, dk:2 * dk, dk:2 * dk] = s0_ref[s, 2 * p + 1].T

    r = lax.broadcasted_iota(jnp.int32, (rb, rb), 0)
    cidx = lax.broadcasted_iota(jnp.int32, (rb, rb), 1)
    block_causal = ((r >= cidx) & (r // c_len == cidx // c_len)).astype(BF16)
    causal2 = (lax.broadcasted_iota(jnp.int32, (c_len, 2 * c_len), 0)
               >= lax.broadcasted_iota(jnp.int32, (c_len, 2 * c_len), 1) % c_len)

    def rows(j):
        return slice(j * c_len, (j + 1) * c_len)

    gcum, safe = [], None
    for s in range(n_seq):
        lf = lf_ref[s]
        lf_hi = lf.astype(BF16)
        rest = lf - lf_hi.astype(F32)
        lf_mid = rest.astype(BF16)
        lf_lo = (rest - lf_mid.astype(F32)).astype(BF16)
        gs = ((jnp.dot(block_causal, lf_lo, preferred_element_type=F32)
               + jnp.dot(block_causal, lf_mid, preferred_element_type=F32))
              + jnp.dot(block_causal, lf_hi, preferred_element_type=F32))
        gcum.append(gs)
        for j in range(n_chunk):
            ok = ((jnp.max(-gs[(j + 1) * c_len - 1:(j + 1) * c_len, :]) <= HG_SAFE_DECAY)
                  & (jnp.max(jnp.abs(q_ref[s, rows(j), :])) <= HG_SAFE_Q))
            safe = ok if safe is None else (safe & ok)

    def g_mid(s, j):
        return gcum[s][j * c_len + c_len // 2 - 1:j * c_len + c_len // 2, :]

    def g_last(s, j):
        return gcum[s][(j + 1) * c_len - 1:(j + 1) * c_len, :]

    @pl.when(safe)
    def _():
        for u, (s, j) in enumerate(units):
            g = gcum[s][rows(j), :]
            qi = (q_ref[s, rows(j), :] * jnp.exp(g - g_mid(s, j))).astype(BF16)
            ki = (k_ref[s, rows(j), :] * jnp.exp(g_mid(s, j) - g)).astype(BF16)
            vb = v_ref[s, rows(j), :].astype(BF16)
            for p in range(n_pairs):
                ps = slice(2 * p * dk, 2 * (p + 1) * dk)
                a = lax.dot_general(qi[:, ps], pair_tile(ki[:, ps]), NT_DIMS, preferred_element_type=F32)
                a = jnp.where(causal2, a, 0.0).astype(BF16)
                oi_ref[u, :, ps] = jnp.dot(a, pair_tile(vb[:, ps]), preferred_element_type=F32)

    @pl.when(jnp.logical_not(safe))
    def _():
        lane_h = lax.broadcasted_iota(jnp.int32, (n_heads * dk, n_heads * dk), 0) // dk
        lane_w = lax.broadcasted_iota(jnp.int32, (n_heads * dk, n_heads * dk), 1) // dk
        head_sum = (lane_h == lane_w).astype(BF16)
        t_idx = lax.broadcasted_iota(jnp.int32, (c_len, 1), 0)
        for u, (s, j) in enumerate(units):
            g = gcum[s][rows(j), :]
            g_ref[...] = g
            q = q_ref[s, rows(j), :]

            def key_row(i, acc, s=s, j=j, g=g, q=q):
                gi = g_ref[pl.ds(i, 1), :]
                decay = jnp.exp(jnp.where(t_idx >= i, g - gi, -jnp.inf))
                term = (q * decay * k_ref[s, pl.ds(j * c_len + i, 1), :]).astype(BF16)
                a_i = jnp.dot(term, head_sum, preferred_element_type=F32)
                return acc + a_i * v_ref[s, pl.ds(j * c_len + i, 1), :]

            oi_ref[u] = lax.fori_loop(0, c_len, key_row, jnp.zeros((c_len, n_heads * dk), F32))

    for u, (s, j) in enumerate(units):
        g = gcum[s][rows(j), :]
        qs = (q_ref[s, rows(j), :] * jnp.exp(g)).astype(BF16)
        ks = (k_ref[s, rows(j), :] * jnp.exp(g_last(s, j) - g)).astype(BF16)
        vb = v_ref[s, rows(j), :].astype(BF16)
        e_last = jnp.exp(g_last(s, j))
        for p in range(n_pairs):
            ps = slice(2 * p * dk, 2 * (p + 1) * dk)
            st = st_ref[s, p]
            o_ref[s, rows(j), ps] = oi_ref[u, :, ps] + lax.dot_general(qs[:, ps], st.astype(BF16), NT_DIMS,
                                                                       preferred_element_type=F32)
            upd = lax.dot_general(vb[:, ps], ks[:, ps], TN_DIMS, preferred_element_type=F32)
            st_ref[s, p] = jnp.where(state_diag, e_last[:, ps] * st + upd, 0.0)

    @pl.when(c == pl.num_programs(1) - 1)
    def _():
        for s in range(n_seq):
            for p in range(n_pairs):
                sout_ref[s, 2 * p] = st_ref[s, p, 0:dk, 0:dk].T
                sout_ref[s, 2 * p + 1] = st_ref[s, p, dk:2 * dk, dk:2 * dk].T


def _hg_core(q, k, lf, v, s0, c_len, n_chunk, n_seq):
    batch, t, d = q.shape
    n_heads = d // HG_DK
    rb = c_len * n_chunk
    assert t % rb == 0 and batch % n_seq == 0 and n_heads % 2 == 0
    rows = lambda b, c: (b, c, 0)
    state = lambda b, c: (b, 0, 0, 0)
    return pl.pallas_call(
        functools.partial(_hg_core_body, c_len=c_len, n_chunk=n_chunk, n_seq=n_seq, n_heads=n_heads),
        grid=(batch // n_seq, t // rb),
        in_specs=[pl.BlockSpec((n_seq, rb, d), rows)] * 4 + [pl.BlockSpec((n_seq, n_heads, HG_DK, HG_DK), state)],
        out_specs=[pl.BlockSpec((n_seq, rb, d), rows), pl.BlockSpec((n_seq, n_heads, HG_DK, HG_DK), state)],
        out_shape=[jax.ShapeDtypeStruct((batch, t, d), F32),
                   jax.ShapeDtypeStruct((batch, n_heads, HG_DK, HG_DK), F32)],
        scratch_shapes=[pltpu.VMEM((n_seq, n_heads // 2, 2 * HG_DK, 2 * HG_DK), F32), pltpu.VMEM((c_len, d), F32),
                        pltpu.VMEM((n_seq * n_chunk, c_len, d), F32)],
        compiler_params=_params("arbitrary", "arbitrary"),
        name=f"hgrn_core_{c_len}",
    )(q, k, lf, v, s0)


def _hg_out_router_body(op_ref, os_ref, gp_ref, gs_ref, x_ref, gn_ref, wo_ref, g_ref, wr_ref, br_ref,
                        x3_ref, ids_ref, wts_ref, wo_bf, *, npt, n_heads, n_exp):
    i = pl.program_id(0)

    @pl.when(i == 0)
    def _():
        wo_bf[...] = wo_ref[...].astype(BF16)

    dk = HG_DK
    o = jnp.where(i < npt, op_ref[...], os_ref[...])
    gate = jnp.where(i < npt, gp_ref[...], gs_ref[...])
    gn = gn_ref[...]
    normed = jnp.concatenate([_rms(o[:, h * dk:(h + 1) * dk], gn) for h in range(n_heads)], axis=1)
    y = (normed * gate).astype(BF16)
    x3 = x_ref[...] + jnp.dot(y, wo_bf[...], preferred_element_type=F32)
    x3_ref[...] = x3
    h4 = _rms(x3, g_ref[...])
    h_hi = h4.astype(BF16)
    h_lo = (h4 - h_hi.astype(F32)).astype(BF16)
    wr = wr_ref[...]
    w_hi = wr.astype(BF16)
    w_lo = (wr - w_hi.astype(F32)).astype(BF16)
    logits = (jnp.dot(h_hi, w_hi, preferred_element_type=F32)
              + (jnp.dot(h_lo, w_hi, preferred_element_type=F32) + jnp.dot(h_hi, w_lo, preferred_element_type=F32))
              + br_ref[...])
    lane = lax.broadcasted_iota(jnp.int32, logits.shape, 1)
    logits = jnp.where(lane < n_exp, logits, -jnp.inf)
    m1 = jnp.max(logits, axis=-1, keepdims=True)
    i1 = jnp.min(jnp.where(logits == m1, lane, LANES), axis=-1, keepdims=True)
    rest = jnp.where(lane == i1, -jnp.inf, logits)
    m2 = jnp.max(rest, axis=-1, keepdims=True)
    i2 = jnp.min(jnp.where(rest == m2, lane, LANES), axis=-1, keepdims=True)
    e2 = jnp.exp(m2 - m1)
    den = 1.0 + e2
    ids_ref[...] = jnp.where(lane == 0, i1, jnp.where(lane == 1, i2, 0))
    wts_ref[...] = jnp.where(lane == 0, 1.0 / den, jnp.where(lane == 1, e2 / den, 0.0))


def _hg_out_router(o_p, o_s, gate_p, gate_s, x, gn, wo, g, w_router, b_router, n_exp):
    n, d = x.shape
    n_p, n_s = o_p.shape[0], o_s.shape[0]
    tm = _pick_tile(math.gcd(n_p, n_s), 512)
    npt = n_p // tm
    row = lambda i: (i, 0)
    fix = lambda i: (0, 0)
    rp, rs = _split_maps(npt)
    return pl.pallas_call(
        functools.partial(_hg_out_router_body, npt=npt, n_heads=d // HG_DK, n_exp=n_exp),
        grid=(n // tm,),
        in_specs=[pl.BlockSpec((tm, d), rp), pl.BlockSpec((tm, d), rs), pl.BlockSpec((tm, d), rp),
                  pl.BlockSpec((tm, d), rs), pl.BlockSpec((tm, d), row),
                  pl.BlockSpec((1, HG_DK), fix), pl.BlockSpec(wo.shape, fix), pl.BlockSpec((1, d), fix),
                  pl.BlockSpec((d, LANES), fix), pl.BlockSpec((1, LANES), fix)],
        out_specs=[pl.BlockSpec((tm, d), row), pl.BlockSpec((tm, LANES), row), pl.BlockSpec((tm, LANES), row)],
        out_shape=[jax.ShapeDtypeStruct((n, d), F32), jax.ShapeDtypeStruct((n, LANES), jnp.int32),
                   jax.ShapeDtypeStruct((n, LANES), F32)],
        scratch_shapes=[pltpu.VMEM(wo.shape, BF16)],
        compiler_params=_params("arbitrary"),
        name="hgrn_out_router",
    )(o_p, o_s, gate_p, gate_s, x, gn, wo, g, w_router, b_router)


def _moe_body(te_ref, nv_ref, first_ref, last_ref, rin0_ref, rin_next_ref, rout_prev_ref, rout_cur_ref,
              x_hbm, g_ref, wg_ref, wu_ref, wd_ref, y_hbm,
              xg_ref, h_ref, acc_ref, wg_res, wu_res, wd_res, sem_in, sem_out, *, tm, rows_per_step):
    del te_ref
    i = pl.program_id(0)
    c = pl.program_id(1)
    slot = i % 2
    other = 1 - slot

    def row_in(buf, j, token):
        return pltpu.make_async_copy(x_hbm.at[pl.ds(token, 1)], xg_ref.at[buf, pl.ds(j, 1)], sem_in)

    def row_out(buf, j, dst):
        return pltpu.make_async_copy(acc_ref.at[buf, pl.ds(j, 1)], y_hbm.at[pl.ds(dst, 1)], sem_out)

    def each_row(fn):
        def body(j, carry):
            fn(j)
            return carry
        lax.fori_loop(0, tm, body, 0, unroll=8)

    @pl.when(nv_ref[i] > 0)
    def _():
        @pl.when((i == 0) & (c == 0))
        def _():
            acc_ref[1] = jnp.zeros(acc_ref.shape[1:], F32)
            each_row(lambda j: row_in(0, j, rin0_ref[0, 0, j]).start())
            each_row(lambda j: row_in(0, j, 0).wait())

        @pl.when(c == 0)
        def _():
            h_ref[...] = _rms(xg_ref[slot], g_ref[...]).astype(BF16)
            acc_ref[slot] = jnp.zeros(acc_ref.shape[1:], F32)

        def stream_rows():
            for u in range(rows_per_step):
                j = c * rows_per_step + u
                row_in(other, j, rin_next_ref[0, 0, j]).start(priority=1)
                row_out(other, j, rout_prev_ref[0, 0, j]).start(priority=1)

        @pl.when(first_ref[i] > 0)
        def _():
            stream_rows()
            wg_res[c] = wg_ref[...].astype(BF16)
            wu_res[c] = wu_ref[...].astype(BF16)
            wd_res[c] = wd_ref[...].astype(BF16)
            _swiglu_step(h_ref[...], wg_res[c], wu_res[c], wd_res[c], acc_ref.at[slot])

        @pl.when(first_ref[i] == 0)
        def _():
            stream_rows()
            _swiglu_step(h_ref[...], wg_res[c], wu_res[c], wd_res[c], acc_ref.at[slot])

        @pl.when(c == pl.num_programs(1) - 1)
        def _():
            each_row(lambda j: row_in(other, j, 0).wait())
            each_row(lambda j: row_out(other, j, 0).wait())

            @pl.when(i == last_ref[0])
            def _():
                each_row(lambda j: row_out(slot, j, rout_cur_ref[0, 0, j]).start())
                each_row(lambda j: row_out(slot, j, 0).wait())


MOE_TILE_ROWS = 672
BF16_SUBLANES = 16


def _moe_experts(x, g, ids, w_gu, w_down):
    n, d = x.shape
    n_exp, d_ff = w_down.shape[0], w_down.shape[1]
    tf = _ff_chunk(d_ff)
    nc = d_ff // tf
    unit = BF16_SUBLANES * nc
    tm = unit * max(1, round(MOE_TILE_ROWS / unit))
    n_asg = n * TOP_K
    n_tiles = (n_asg + n_exp * (tm - 1) + tm - 1) // tm

    e_flat = ids.T.reshape(n_asg)
    onehot = (e_flat[:, None] == jnp.arange(n_exp, dtype=jnp.int32)[None, :]).astype(jnp.int32)
    counts = jnp.sum(onehot, axis=0)
    rank = jnp.sum((jnp.cumsum(onehot, axis=0) - onehot) * onehot, axis=1)
    tiles_per = (counts + tm - 1) // tm
    tile_end = jnp.cumsum(tiles_per)
    tile_start = tile_end - tiles_per
    slot = jnp.sum(onehot * tile_start[None, :], axis=1) * tm + rank
    asg = jnp.full((n_tiles * tm,), -1, jnp.int32).at[slot].set(jnp.arange(n_asg, dtype=jnp.int32),
                                                                 unique_indices=True)
    pad_row = n_asg + jnp.arange(n_tiles * tm, dtype=jnp.int32) % tm
    rows_in = jnp.where(asg >= 0, jnp.where(asg >= n, asg - n, asg), 0)
    rows_out = jnp.where(asg >= 0, asg, pad_row)
    rows_in_ext = jnp.concatenate([rows_in, jnp.zeros((tm,), jnp.int32)]).reshape(n_tiles + 1, 1, tm)
    rows_out_ext = jnp.concatenate([pad_row[:tm], rows_out]).reshape(n_tiles + 1, 1, tm)
    tile = jnp.arange(n_tiles, dtype=jnp.int32)
    last = tile_end[-1] - 1
    tile_c = jnp.minimum(tile, last)
    te = jnp.minimum(jnp.sum((tile_c[:, None] >= tile_end[None, :]).astype(jnp.int32), axis=1), n_exp - 1)
    te_hot = (te[:, None] == jnp.arange(n_exp, dtype=jnp.int32)[None, :]).astype(jnp.int32)
    cnt_t = jnp.sum(te_hot * counts[None, :], axis=1)
    start_t = jnp.sum(te_hot * tile_start[None, :], axis=1)
    nv = jnp.where(tile <= last, jnp.clip(cnt_t - (tile - start_t) * tm, 0, tm), 0).astype(jnp.int32)

    first = ((tile == start_t) & (nv > 0)).astype(jnp.int32)

    def chunk(i, c, nv_ref, first_ref):
        return jnp.where((nv_ref[i] > 0) & (first_ref[i] > 0), c, nc - 1)

    smem_rows = lambda index: pl.BlockSpec((1, 1, tm), index, memory_space=pltpu.SMEM)
    return pl.pallas_call(
        functools.partial(_moe_body, tm=tm, rows_per_step=tm // nc),
        grid_spec=pltpu.PrefetchScalarGridSpec(
            num_scalar_prefetch=4,
            grid=(n_tiles, nc),
            in_specs=[smem_rows(lambda i, c, te_ref, nv_ref, first_ref, last_ref: (0, 0, 0)),
                      smem_rows(lambda i, c, te_ref, nv_ref, first_ref, last_ref: (i + 1, 0, 0)),
                      smem_rows(lambda i, c, te_ref, nv_ref, first_ref, last_ref: (i, 0, 0)),
                      smem_rows(lambda i, c, te_ref, nv_ref, first_ref, last_ref: (i + 1, 0, 0)),
                      pl.BlockSpec(memory_space=pl.ANY),
                      pl.BlockSpec((1, d), lambda i, c, te_ref, nv_ref, first_ref, last_ref: (0, 0)),
                      pl.BlockSpec((None, d, tf), lambda i, c, te_ref, nv_ref, first_ref, last_ref:
                                   (te_ref[i], 0, chunk(i, c, nv_ref, first_ref))),
                      pl.BlockSpec((None, d, tf), lambda i, c, te_ref, nv_ref, first_ref, last_ref:
                                   (te_ref[i], 0, chunk(i, c, nv_ref, first_ref) + nc)),
                      pl.BlockSpec((None, tf, d), lambda i, c, te_ref, nv_ref, first_ref, last_ref:
                                   (te_ref[i], chunk(i, c, nv_ref, first_ref), 0))],
            out_specs=pl.BlockSpec(memory_space=pl.ANY),
            scratch_shapes=[pltpu.VMEM((2, tm, d), F32), pltpu.VMEM((tm, d), BF16), pltpu.VMEM((2, tm, d), F32),
                            pltpu.VMEM((nc, d, tf), BF16), pltpu.VMEM((nc, d, tf), BF16), pltpu.VMEM((nc, tf, d), BF16),
                            pltpu.SemaphoreType.DMA, pltpu.SemaphoreType.DMA]),
        out_shape=jax.ShapeDtypeStruct((n_asg + tm, d), F32),
        compiler_params=_params("arbitrary", "arbitrary"),
        name="moe_experts",
    )(te.astype(jnp.int32), nv, first, last.reshape(1).astype(jnp.int32), rows_in_ext, rows_in_ext, rows_out_ext,
      rows_out_ext, x, g, w_gu, w_gu, w_down)


def _combine_body(x_ref, y0_ref, y1_ref, wts_ref, g_ref, outp_ref, outs_ref, *, npt):
    wts = wts_ref[...]
    moe = wts[:, 0:1] * y0_ref[...] + wts[:, 1:2] * y1_ref[...]
    _store_split(pl.program_id(0) < npt, outp_ref, outs_ref, _rms(x_ref[...] + moe, g_ref[...]))


def _combine(x, y2, wts, g, n_p):
    n, d = x.shape
    n_s = n - n_p
    tm = _pick_tile(math.gcd(n_p, n_s), 512)
    nt = n // tm
    npt = n_p // tm
    row = lambda i: (i, 0)
    rp, rs = _split_maps(npt)
    return pl.pallas_call(
        functools.partial(_combine_body, npt=npt),
        grid=(nt,),
        in_specs=[pl.BlockSpec((tm, d), row), pl.BlockSpec((tm, d), row), pl.BlockSpec((tm, d), lambda i: (i + nt, 0)),
                  pl.BlockSpec((tm, LANES), row), pl.BlockSpec((1, d), lambda i: (0, 0))],
        out_specs=[pl.BlockSpec((tm, d), rp), pl.BlockSpec((tm, d), rs)],
        out_shape=[jax.ShapeDtypeStruct((n_p, d), F32), jax.ShapeDtypeStruct((n_s, d), F32)],
        compiler_params=_params("arbitrary"),
        name="moe_combine",
    )(x, y2, y2, wts, g)


def _rope_tables(pos, hd):
    half = hd // 2
    inv = ROPE_THETA ** (-jnp.arange(half, dtype=F32) / half)
    ang = pos.astype(F32)[:, None] * inv[None, :]
    cos = jnp.cos(ang)
    sin = jnp.sin(ang)
    reps = LANES // hd
    return jnp.tile(jnp.concatenate([cos, cos], axis=1), (1, reps)), jnp.tile(jnp.concatenate([-sin, sin], axis=1), (1, reps))


def kernel(x_prompt, x_sample, cache_k_win, cache_v_win, state_hgrn, norm_mix, norm_ffn, norm_final,
           w_qkv, b_qkv, w_o_attn, b_o_attn, sinks, w_in_hg, hg_lower, hg_norm, w_o_hg,
           w_gu_dense, w_down_dense, w_router, b_router, w_gu_moe, w_down_moe):
    batch, seq, d = x_prompt.shape
    n_seq, t_dec, _ = x_sample.shape
    n_kv, hd = cache_k_win.shape[3], cache_k_win.shape[4]
    n_heads = sinks.shape[1]
    g_per = n_heads // n_kv
    n_q, n_k = n_heads * hd, n_kv * hd
    n_p, n_s = batch * seq, n_seq * t_dec
    n_exp = w_router.shape[2]
    wc = cache_k_win.shape[2]
    assert norm_mix.shape[0] == 2 and hd * 2 == LANES and d % HG_DK == 0 and seq % WINDOW == 0 and wc == WINDOW

    pos = jnp.concatenate([jnp.tile(jnp.arange(seq), batch), jnp.tile(PAST_LEN + jnp.arange(t_dec), n_seq)])
    cos_t, sin_t = _rope_tables(pos, hd)
    x, q, k, v = _qkv_rope(x_prompt.reshape(n_p, d), x_sample.reshape(n_s, d), norm_mix[0:1], w_qkv[0], b_qkv[0:1],
                           cos_t, sin_t, n_q, n_k, hd)
    o = _swa_prompt(q, k, v, sinks[0], jnp.zeros((n_p + n_s, n_q), BF16), batch, seq, n_kv, g_per, hd)
    kc = cache_k_win[0].reshape(n_seq, wc, n_k)
    vc = cache_v_win[0].reshape(n_seq, wc, n_k)
    o = _swa_sample(q, k, v, kc, vc, sinks[0], o, n_p, n_seq, t_dec, n_kv, g_per, hd)
    last_win = lambda a: jnp.stack([a[(b + 1) * seq - wc:(b + 1) * seq] for b in range(batch)]).reshape(batch, wc, n_kv, hd)
    k_win_p, v_win_p = last_win(k), last_win(v)
    k_win_s = jnp.concatenate([kc[:, t_dec:], k[n_p:].reshape(n_seq, t_dec, n_k)], axis=1).reshape(n_seq, wc, n_kv, hd)
    v_win_s = jnp.concatenate([vc[:, t_dec:], v[n_p:].reshape(n_seq, t_dec, n_k)], axis=1).reshape(n_seq, wc, n_kv, hd)
    x = _oproj_ffn(o, x, w_o_attn[0], b_o_attn[0:1], norm_ffn[0:1], w_gu_dense[0], w_down_dense[0])

    lb_sm = jax.nn.softmax(hg_lower.astype(F32), axis=0)
    lb = (jnp.cumsum(lb_sm, axis=0) - lb_sm[0])[1:2]
    in_part = functools.partial(_hg_inproj_part, x, norm_mix[1:2], w_in_hg[0], lb, n_p)
    hq, sq = in_part("q")
    hlf, hk, slf, sk = in_part("f")
    hv, sv = in_part("i")
    hgate, sgate = in_part("g")
    n_hh = d // HG_DK
    c_p = math.gcd(seq, HG_CHUNK)
    as_seq = lambda a: a.reshape(batch, seq, d)
    o_p, s_p = _hg_core(as_seq(hq), as_seq(hk), as_seq(hlf), as_seq(hv), jnp.zeros((batch, n_hh, HG_DK, HG_DK), F32),
                        c_p, 2 if seq % (2 * c_p) == 0 else 1, 2 if batch % 2 == 0 else 1)
    c_s = SUBLANES * ((t_dec + SUBLANES - 1) // SUBLANES)
    pad = lambda a: jnp.pad(a.reshape(n_seq, t_dec, d), ((0, 0), (0, c_s - t_dec), (0, 0)))
    o_s, s_s = _hg_core(pad(sq), pad(sk), pad(slf), pad(sv), state_hgrn[0], c_s, 1, 8 if n_seq % 8 == 0 else 1)
    o_s = o_s[:, :t_dec].reshape(n_s, d)

    wr = jnp.pad(w_router[0], ((0, 0), (0, LANES - n_exp)))
    br = jnp.pad(b_router[0:1], ((0, 0), (0, LANES - n_exp)))
    x3, ids, wts = _hg_out_router(o_p.reshape(n_p, d), o_s, hgate, sgate, x, hg_norm[0:1], w_o_hg[0], norm_ffn[1:2],
                                  wr, br, n_exp)
    y2 = _moe_experts(x3, norm_ffn[1:2], ids[:, :TOP_K], w_gu_moe[0], w_down_moe[0])
    y_p, y_s = _combine(x3, y2, wts, norm_final.reshape(1, d), n_p)

    return (y_p.reshape(batch, seq, d), y_s.reshape(n_seq, t_dec, d),
            k_win_p[None], v_win_p[None], k_win_s[None], v_win_s[None], s_p[None], s_s[None])
```

```python
import functools
import math

import jax
import jax.numpy as jnp
from jax import lax
from jax.experimental import pallas as pl
from jax.experimental.pallas import tpu as pltpu

F32 = jnp.float32
BF16 = jnp.bfloat16

NORM_EPS = 1e-5
WINDOW = 128
PAST_LEN = 16384
ROPE_THETA = 10000.0
HG_DK = 128
HG_CHUNK = 64
HG_SAFE_DECAY = 60.0
HG_SAFE_Q = 1e9
TOP_K = 2
LANES = 128
SUBLANES = 8
VMEM_LIMIT = 56 * 1024 * 1024

NT_DIMS = (((1,), (1,)), ((), ()))
TN_DIMS = (((0,), (0,)), ((), ()))


def _pick_tile(n, target):
    for t in (1536, 1024, 768, 512, 384, 256, 192, 128, 64, 32, 16, 8):
        if t <= target and n % t == 0:
            return t
    raise ValueError(f"no row tile for {n}")


def _params(*sem):
    return pltpu.CompilerParams(dimension_semantics=sem, vmem_limit_bytes=VMEM_LIMIT)


def _rms(x, g):
    return x * lax.rsqrt(jnp.mean(x * x, axis=-1, keepdims=True) + NORM_EPS) * g


def _split_maps(npt):
    return (lambda i, *_: (jnp.minimum(i, npt - 1), 0)), (lambda i, *_: (jnp.maximum(i - npt, 0), 0))


def _store_split(is_prompt, ref_p, ref_s, val):
    @pl.when(is_prompt)
    def _():
        ref_p[...] = val

    @pl.when(jnp.logical_not(is_prompt))
    def _():
        ref_s[...] = val


def _qkv_body(xp_ref, xs_ref, g_ref, w_ref, b_ref, cos_ref, sin_ref, x_ref, q_ref, k_ref, v_ref, wbf_ref,
              *, npt, n_q, n_k, hd):
    i = pl.program_id(0)

    @pl.when(i == 0)
    def _():
        wbf_ref[...] = w_ref[...].astype(BF16)

    x = jnp.where(i < npt, xp_ref[...], xs_ref[...])
    x_ref[...] = x
    h = _rms(x, g_ref[...]).astype(BF16)
    y = jnp.dot(h, wbf_ref[...], preferred_element_type=F32) + b_ref[...]
    cos = cos_ref[...]
    sin = sin_ref[...]
    lane = lax.broadcasted_iota(jnp.int32, cos.shape, 1)
    first = (lane % hd) < (hd // 2)

    def rope(blk):
        partner = jnp.where(first, pltpu.roll(blk, LANES - hd // 2, 1), pltpu.roll(blk, hd // 2, 1))
        return blk * cos + partner * sin

    scale = hd ** -0.5
    for j in range(n_q // LANES):
        q_ref[:, j * LANES:(j + 1) * LANES] = (rope(y[:, j * LANES:(j + 1) * LANES]) * scale).astype(BF16)
    for j in range(n_k // LANES):
        k_ref[:, j * LANES:(j + 1) * LANES] = rope(y[:, n_q + j * LANES:n_q + (j + 1) * LANES])
    v_ref[...] = y[:, n_q + n_k:]


def _qkv_rope(xp, xs, g, w, b, cos_t, sin_t, n_q, n_k, hd):
    (n_p, d), n_s = xp.shape, xs.shape[0]
    n = n_p + n_s
    n_out = w.shape[1]
    tm = _pick_tile(math.gcd(n_p, n_s), 512)
    npt = n_p // tm
    row = lambda i: (i, 0)
    fix = lambda i: (0, 0)
    rp, rs = _split_maps(npt)
    return pl.pallas_call(
        functools.partial(_qkv_body, npt=npt, n_q=n_q, n_k=n_k, hd=hd),
        grid=(n // tm,),
        in_specs=[pl.BlockSpec((tm, d), rp), pl.BlockSpec((tm, d), rs), pl.BlockSpec((1, d), fix),
                  pl.BlockSpec((d, n_out), fix), pl.BlockSpec((1, n_out), fix),
                  pl.BlockSpec((tm, LANES), row), pl.BlockSpec((tm, LANES), row)],
        out_specs=[pl.BlockSpec((tm, d), row), pl.BlockSpec((tm, n_q), row), pl.BlockSpec((tm, n_k), row),
                   pl.BlockSpec((tm, n_k), row)],
        out_shape=[jax.ShapeDtypeStruct((n, d), F32), jax.ShapeDtypeStruct((n, n_q), BF16),
                   jax.ShapeDtypeStruct((n, n_k), F32), jax.ShapeDtypeStruct((n, n_k), F32)],
        scratch_shapes=[pltpu.VMEM((d, n_out), BF16)],
        compiler_params=_params("arbitrary"),
        name="qkv_rope",
    )(xp, xs, g, w, b, cos_t, sin_t)


def _sink_column(sink_ref, kh, g_per, rows_per):
    blk = lax.broadcasted_iota(jnp.int32, (g_per * rows_per, 1), 0) // rows_per
    col = jnp.full((g_per * rows_per, 1), sink_ref[kh * g_per], F32)
    for g in range(1, g_per):
        col = jnp.where(blk == g, sink_ref[kh * g_per + g], col)
    return col


def _stack_heads(q, kh, g_per, hd):
    return jnp.concatenate([q[:, (kh * g_per + g) * hd:(kh * g_per + g + 1) * hd] for g in range(g_per)], axis=0)


def _swa_prompt_body(sink_ref, q_ref, kp_ref, kc_ref, vp_ref, vc_ref, o_all_ref, o_ref, *, n_kv, g_per, hd, w):
    del o_all_ref
    n = pl.program_id(1)
    q = q_ref[...]
    kk = jnp.concatenate([kp_ref[...], kc_ref[...]], axis=0).astype(BF16)
    vv = jnp.concatenate([vp_ref[...], vc_ref[...]], axis=0).astype(BF16)
    i = lax.broadcasted_iota(jnp.int32, (g_per * w, 2 * w), 0) % w
    j = lax.broadcasted_iota(jnp.int32, (g_per * w, 2 * w), 1)
    mask = (j > i) & (j <= i + w) & ((j >= w) | (n > 0))
    outs = []
    for kh in range(n_kv):
        q4 = _stack_heads(q, kh, g_per, hd)
        s = lax.dot_general(q4, kk[:, kh * hd:(kh + 1) * hd], NT_DIMS, preferred_element_type=F32)
        s = jnp.where(mask, s, -jnp.inf)
        sink = _sink_column(sink_ref, kh, g_per, w)
        m = jnp.maximum(jnp.max(s, axis=-1, keepdims=True), sink)
        p = jnp.exp(s - m)
        p = p / (jnp.sum(p, axis=-1, keepdims=True) + jnp.exp(sink - m))
        o4 = jnp.dot(p.astype(BF16), vv[:, kh * hd:(kh + 1) * hd], preferred_element_type=F32)
        outs.append(jnp.concatenate([o4[g * w:(g + 1) * w] for g in range(g_per)], axis=1))
    o_ref[...] = jnp.concatenate(outs, axis=1).astype(BF16)


def _swa_prompt(q, k, v, sinks, o_all, batch, seq, n_kv, g_per, hd):
    w = WINDOW
    nb = seq // w
    dq = q.shape[1]
    dkv = k.shape[1]
    cur = lambda b, n: (b * nb + n, 0)
    prev = lambda b, n: (b * nb + jnp.maximum(n - 1, 0), 0)
    return pl.pallas_call(
        functools.partial(_swa_prompt_body, n_kv=n_kv, g_per=g_per, hd=hd, w=w),
        grid=(batch, nb),
        in_specs=[pl.BlockSpec(memory_space=pltpu.SMEM),
                  pl.BlockSpec((w, dq), cur), pl.BlockSpec((w, dkv), prev), pl.BlockSpec((w, dkv), cur),
                  pl.BlockSpec((w, dkv), prev), pl.BlockSpec((w, dkv), cur),
                  pl.BlockSpec(memory_space=pl.ANY)],
        out_specs=pl.BlockSpec((w, dq), cur),
        out_shape=jax.ShapeDtypeStruct(o_all.shape, o_all.dtype),
        input_output_aliases={6: 0},
        compiler_params=_params("arbitrary", "arbitrary"),
        name="swa_prompt",
    )(sinks, q, k, k, v, v, o_all)


def _swa_sample_body(sink_ref, q_ref, kn_ref, vn_ref, kc_ref, vc_ref, o_all_ref, o_ref, *, n_kv, g_per, hd, bt, t, wc):
    del o_all_ref
    r = bt * t
    q = q_ref[...]
    kn = kn_ref[...].astype(BF16)
    vn = vn_ref[...].astype(BF16)
    kc = kc_ref[...].reshape(bt * wc, n_kv * hd).astype(BF16)
    vc = vc_ref[...].reshape(bt * wc, n_kv * hd).astype(BF16)
    row_c = lax.broadcasted_iota(jnp.int32, (g_per * r, bt * wc), 0) % r
    col_c = lax.broadcasted_iota(jnp.int32, (g_per * r, bt * wc), 1)
    mask_c = (col_c // wc == row_c // t) & (col_c % wc > row_c % t + (wc - WINDOW))
    row_n = lax.broadcasted_iota(jnp.int32, (g_per * r, r), 0) % r
    col_n = lax.broadcasted_iota(jnp.int32, (g_per * r, r), 1)
    mask_n = (col_n // t == row_n // t) & (col_n % t <= row_n % t)
    outs = []
    for kh in range(n_kv):
        hs = slice(kh * hd, (kh + 1) * hd)
        q4 = _stack_heads(q, kh, g_per, hd)
        sc = jnp.where(mask_c, lax.dot_general(q4, kc[:, hs], NT_DIMS, preferred_element_type=F32), -jnp.inf)
        sn = jnp.where(mask_n, lax.dot_general(q4, kn[:, hs], NT_DIMS, preferred_element_type=F32), -jnp.inf)
        sink = _sink_column(sink_ref, kh, g_per, r)
        m = jnp.maximum(jnp.maximum(jnp.max(sc, axis=-1, keepdims=True), jnp.max(sn, axis=-1, keepdims=True)), sink)
        pc = jnp.exp(sc - m)
        pn = jnp.exp(sn - m)
        den = jnp.sum(pc, axis=-1, keepdims=True) + jnp.sum(pn, axis=-1, keepdims=True) + jnp.exp(sink - m)
        o4 = (jnp.dot((pc / den).astype(BF16), vc[:, hs], preferred_element_type=F32)
              + jnp.dot((pn / den).astype(BF16), vn[:, hs], preferred_element_type=F32))
        outs.append(jnp.concatenate([o4[g * r:(g + 1) * r] for g in range(g_per)], axis=1))
    o_ref[...] = jnp.concatenate(outs, axis=1).astype(BF16)


def _swa_sample(q, k, v, k_cache, v_cache, sinks, o_all, row0, n_seq, t, n_kv, g_per, hd):
    wc = k_cache.shape[1]
    dq = q.shape[1]
    dkv = k.shape[1]
    bt = 8 if n_seq % 8 == 0 else n_seq
    r = bt * t
    assert row0 % r == 0
    off = row0 // r
    rows = lambda i: (off + i, 0)
    return pl.pallas_call(
        functools.partial(_swa_sample_body, n_kv=n_kv, g_per=g_per, hd=hd, bt=bt, t=t, wc=wc),
        grid=(n_seq // bt,),
        in_specs=[pl.BlockSpec(memory_space=pltpu.SMEM),
                  pl.BlockSpec((r, dq), rows), pl.BlockSpec((r, dkv), rows), pl.BlockSpec((r, dkv), rows),
                  pl.BlockSpec((bt, wc, dkv), lambda i: (i, 0, 0)), pl.BlockSpec((bt, wc, dkv), lambda i: (i, 0, 0)),
                  pl.BlockSpec(memory_space=pl.ANY)],
        out_specs=pl.BlockSpec((r, dq), rows),
        out_shape=jax.ShapeDtypeStruct(o_all.shape, o_all.dtype),
        input_output_aliases={6: 0},
        compiler_params=_params("arbitrary"),
        name="swa_sample",
    )(sinks, q, k, v, k_cache, v_cache, o_all)


def _swiglu_step(h, wg, wu, wd, acc_ref):
    a = jnp.dot(h, wg, preferred_element_type=F32)
    b = jnp.dot(h, wu, preferred_element_type=F32)
    act = (a * jax.nn.sigmoid(a) * b).astype(BF16)
    acc_ref[...] += jnp.dot(act, wd, preferred_element_type=F32)


def _oproj_ffn_body(o_ref, x_ref, wo_ref, bo_ref, g_ref, wg_ref, wu_ref, wd_ref, out_ref,
                    wo_bf, x1_ref, h_ref, acc_ref):
    i = pl.program_id(0)
    c = pl.program_id(1)

    @pl.when((i == 0) & (c == 0))
    def _():
        wo_bf[...] = wo_ref[...].astype(BF16)

    @pl.when(c == 0)
    def _():
        x1 = x_ref[...] + jnp.dot(o_ref[...], wo_bf[...], preferred_element_type=F32) + bo_ref[...]
        x1_ref[...] = x1
        h_ref[...] = _rms(x1, g_ref[...]).astype(BF16)
        acc_ref[...] = jnp.zeros_like(acc_ref)

    _swiglu_step(h_ref[...], wg_ref[...].astype(BF16), wu_ref[...].astype(BF16), wd_ref[...].astype(BF16), acc_ref)

    @pl.when(c == pl.num_programs(1) - 1)
    def _():
        out_ref[...] = x1_ref[...] + acc_ref[...]


def _ff_chunk(d_ff):
    for tf in (512, 256, 128):
        if d_ff % tf == 0:
            return tf
    raise ValueError(f"d_ff {d_ff} is not a multiple of {LANES}")


def _oproj_ffn(o, x, wo, bo, g, w_gu, w_down):
    n, d = x.shape
    d_ff = w_down.shape[0]
    tf = _ff_chunk(d_ff)
    nc = d_ff // tf
    tm = _pick_tile(n, 768)
    row = lambda i, c: (i, 0)
    fix = lambda i, c: (0, 0)
    return pl.pallas_call(
        _oproj_ffn_body,
        grid=(n // tm, nc),
        in_specs=[pl.BlockSpec((tm, o.shape[1]), row), pl.BlockSpec((tm, d), row),
                  pl.BlockSpec(wo.shape, fix), pl.BlockSpec((1, d), fix), pl.BlockSpec((1, d), fix),
                  pl.BlockSpec((d, tf), lambda i, c: (0, c)), pl.BlockSpec((d, tf), lambda i, c: (0, c + nc)),
                  pl.BlockSpec((tf, d), lambda i, c: (c, 0))],
        out_specs=pl.BlockSpec((tm, d), row),
        out_shape=jax.ShapeDtypeStruct((n, d), F32),
        scratch_shapes=[pltpu.VMEM(wo.shape, BF16), pltpu.VMEM((tm, d), F32), pltpu.VMEM((tm, d), BF16),
                        pltpu.VMEM((tm, d), F32)],
        compiler_params=_params("arbitrary", "arbitrary"),
        name="oproj_ffn",
    )(o, x, wo, bo, g, w_gu, w_gu, w_down)


def _hg_in_body(x_ref, g_ref, w_ref, lb_ref, *refs, npt, part):
    out_refs, (wbf_ref, tmp_ref) = refs[:-2], refs[-2:]
    i = pl.program_id(0)
    is_p = i < npt

    @pl.when(i == 0)
    def _():
        wbf_ref[...] = w_ref[...].astype(BF16)

    h = _rms(x_ref[...], g_ref[...]).astype(BF16)
    z = jnp.dot(h, wbf_ref[...], preferred_element_type=F32)

    def emit(ref_p, ref_s, val):
        tmp_ref[...] = val
        _store_split(is_p, ref_p, ref_s, tmp_ref[...])

    if part == "q":
        emit(out_refs[0], out_refs[1], z * jax.nn.sigmoid(z) * (HG_DK ** -0.5))
    elif part == "f":
        lb = lb_ref[...]
        t = jnp.exp(-jnp.abs(z))
        log_sig = jnp.minimum(z, 0.0) - jnp.log(1.0 + t)
        a = jnp.log(lb)
        b = jnp.log1p(-lb) + log_sig
        emit(out_refs[0], out_refs[2], jnp.maximum(a, b) + jnp.log(1.0 + jnp.exp(-jnp.abs(a - b))))
        emit(out_refs[1], out_refs[3], (1.0 - lb) * (jnp.where(z >= 0.0, t, 1.0) / (1.0 + t)))
    elif part == "i":
        emit(out_refs[0], out_refs[1], z)
    else:
        emit(out_refs[0], out_refs[1], z * jax.nn.sigmoid(z))


def _hg_inproj_part(x, g, w_in, lb, n_p, part):
    n, d = x.shape
    n_s = n - n_p
    col = "qfig".index(part)
    n_res = 2 if part == "f" else 1
    tm = _pick_tile(math.gcd(n_p, n_s), 512)
    npt = n_p // tm
    row = lambda i: (i, 0)
    fix = lambda i: (0, 0)
    rp, rs = _split_maps(npt)
    return pl.pallas_call(
        functools.partial(_hg_in_body, npt=npt, part=part),
        grid=(n // tm,),
        in_specs=[pl.BlockSpec((tm, d), row), pl.BlockSpec((1, d), fix), pl.BlockSpec((d, d), lambda i: (0, col)),
                  pl.BlockSpec((1, d), fix)],
        out_specs=[pl.BlockSpec((tm, d), rp)] * n_res + [pl.BlockSpec((tm, d), rs)] * n_res,
        out_shape=[jax.ShapeDtypeStruct((n_p, d), F32)] * n_res + [jax.ShapeDtypeStruct((n_s, d), F32)] * n_res,
        scratch_shapes=[pltpu.VMEM((d, d), BF16), pltpu.VMEM((tm, d), F32)],
        compiler_params=_params("arbitrary"),
        name=f"hgrn_inproj_{part}",
    )(x, g, w_in, lb)


def _hg_core_body(q_ref, k_ref, lf_ref, v_ref, s0_ref, o_ref, sout_ref, st_ref, g_ref, oi_ref,
                  *, c_len, n_chunk, n_seq, n_heads, carried):
    c = pl.program_id(1)
    dk = HG_DK
    rb = c_len * n_chunk
    units = [(s, j) for s in range(n_seq) for j in range(n_chunk)]

    n_pairs = n_heads // 2
    state_diag = (lax.broadcasted_iota(jnp.int32, (2 * dk, 2 * dk), 0) // dk
                  == lax.broadcasted_iota(jnp.int32, (2 * dk, 2 * dk), 1) // dk)
    rows_diag = (lax.broadcasted_iota(jnp.int32, (2 * c_len, 2 * dk), 0) // c_len
                 == lax.broadcasted_iota(jnp.int32, (2 * c_len, 2 * dk), 1) // dk)

    def pair_tile(x):
        x2 = jnp.concatenate([x, x], axis=0)
        return jnp.where(rows_diag, x2, jnp.zeros_like(x2))

    if carried:
        @pl.when(c == 0)
        def _():
            for s in range(n_seq):
                for p in range(n_pairs):
                    st_ref[s, p] = jnp.zeros((2 * dk, 2 * dk), F32)
                    st_ref[s, p, 0:dk, 0:dk] = s0_ref[s, 2 * p].T
                    st_ref[s, p, dk:2 * dk, dk:2 * dk] = s0_ref[s, 2 * p + 1].T

    r = lax.broadcasted_iota(jnp.int32, (rb, rb), 0)
    cidx = lax.broadcasted_iota(jnp.int32, (rb, rb), 1)
    block_causal = ((r >= cidx) & (r // c_len == cidx // c_len)).astype(BF16)
    causal2 = (lax.broadcasted_iota(jnp.int32, (c_len, 2 * c_len), 0)
               >= lax.broadcasted_iota(jnp.int32, (c_len, 2 * c_len), 1) % c_len)

    def rows(j):
        return slice(j * c_len, (j + 1) * c_len)

    gcum, safe = [], None
    for s in range(n_seq):
        lf = lf_ref[s]
        lf_hi = lf.astype(BF16)
        rest = lf - lf_hi.astype(F32)
        lf_mid = rest.astype(BF16)
        lf_lo = (rest - lf_mid.astype(F32)).astype(BF16)
        gs = ((jnp.dot(block_causal, lf_lo, preferred_element_type=F32)
               + jnp.dot(block_causal, lf_mid, preferred_element_type=F32))
              + jnp.dot(block_causal, lf_hi, preferred_element_type=F32))
        gcum.append(gs)
        for j in range(n_chunk):
            ok = ((jnp.max(-gs[(j + 1) * c_len - 1:(j + 1) * c_len, :]) <= HG_SAFE_DECAY)
                  & (jnp.max(jnp.abs(q_ref[s, rows(j), :])) <= HG_SAFE_Q))
            safe = ok if safe is None else (safe & ok)

    def g_mid(s, j):
        return gcum[s][j * c_len + c_len // 2 - 1:j * c_len + c_len // 2, :]

    def g_last(s, j):
        return gcum[s][(j + 1) * c_len - 1:(j + 1) * c_len, :]

    @pl.when(safe)
    def _():
        for u, (s, j) in enumerate(units):
            g = gcum[s][rows(j), :]
            qi = (q_ref[s, rows(j), :] * jnp.exp(g - g_mid(s, j))).astype(BF16)
            ki = (k_ref[s, rows(j), :] * jnp.exp(g_mid(s, j) - g)).astype(BF16)
            vb = v_ref[s, rows(j), :].astype(BF16)
            for p in range(n_pairs):
                ps = slice(2 * p * dk, 2 * (p + 1) * dk)
                a = lax.dot_general(qi[:, ps], pair_tile(ki[:, ps]), NT_DIMS, preferred_element_type=F32)
                a = jnp.where(causal2, a, 0.0).astype(BF16)
                oi_ref[u, :, ps] = jnp.dot(a, pair_tile(vb[:, ps]), preferred_element_type=F32)

    @pl.when(jnp.logical_not(safe))
    def _():
        lane_h = lax.broadcasted_iota(jnp.int32, (n_heads * dk, n_heads * dk), 0) // dk
        lane_w = lax.broadcasted_iota(jnp.int32, (n_heads * dk, n_heads * dk), 1) // dk
        head_sum = (lane_h == lane_w).astype(BF16)
        t_idx = lax.broadcasted_iota(jnp.int32, (c_len, 1), 0)
        for u, (s, j) in enumerate(units):
            g = gcum[s][rows(j), :]
            g_ref[...] = g
            q = q_ref[s, rows(j), :]

            def key_row(i, acc, s=s, j=j, g=g, q=q):
                gi = g_ref[pl.ds(i, 1), :]
                decay = jnp.exp(jnp.where(t_idx >= i, g - gi, -jnp.inf))
                term = (q * decay * k_ref[s, pl.ds(j * c_len + i, 1), :]).astype(BF16)
                a_i = jnp.dot(term, head_sum, preferred_element_type=F32)
                return acc + a_i * v_ref[s, pl.ds(j * c_len + i, 1), :]

            oi_ref[u] = lax.fori_loop(0, c_len, key_row, jnp.zeros((c_len, n_heads * dk), F32))

    if not carried:
        e_rows = [jnp.exp(g_last(s, 0)) for s in range(n_seq)]
        e_cols = jnp.concatenate(e_rows + [jnp.zeros((LANES - n_seq, n_heads * dk), F32)], axis=0).T
        zero = jnp.zeros((dk, dk), F32)

    for u, (s, j) in enumerate(units):
        g = gcum[s][rows(j), :]
        qs = (q_ref[s, rows(j), :] * jnp.exp(g)).astype(BF16)
        ks = (k_ref[s, rows(j), :] * jnp.exp(g_last(s, j) - g)).astype(BF16)
        vb = v_ref[s, rows(j), :].astype(BF16)
        e_last = jnp.exp(g_last(s, j))
        for p in range(n_pairs):
            ps = slice(2 * p * dk, 2 * (p + 1) * dk)
            if carried:
                st = st_ref[s, p]
                o_ref[s, rows(j), ps] = oi_ref[u, :, ps] + lax.dot_general(qs[:, ps], st.astype(BF16), NT_DIMS,
                                                                           preferred_element_type=F32)
                upd = lax.dot_general(vb[:, ps], ks[:, ps], TN_DIMS, preferred_element_type=F32)
                st_ref[s, p] = jnp.where(state_diag, e_last[:, ps] * st + upd, 0.0)
            else:
                st = jnp.concatenate([jnp.concatenate([s0_ref[s, 2 * p], zero], axis=1),
                                      jnp.concatenate([zero, s0_ref[s, 2 * p + 1]], axis=1)], axis=0)
                o_ref[s, rows(j), ps] = oi_ref[u, :, ps] + jnp.dot(qs[:, ps], st.astype(BF16),
                                                                   preferred_element_type=F32)
                upd = lax.dot_general(ks[:, ps], vb[:, ps], TN_DIMS, preferred_element_type=F32)
                new = e_cols[ps, s:s + 1] * st + upd
                sout_ref[s, 2 * p] = new[0:dk, 0:dk]
                sout_ref[s, 2 * p + 1] = new[dk:2 * dk, dk:2 * dk]

    if carried:
        @pl.when(c == pl.num_programs(1) - 1)
        def _():
            for s in range(n_seq):
                for p in range(n_pairs):
                    sout_ref[s, 2 * p] = st_ref[s, p, 0:dk, 0:dk].T
                    sout_ref[s, 2 * p + 1] = st_ref[s, p, dk:2 * dk, dk:2 * dk].T


def _hg_core(q, k, lf, v, s0, c_len, n_chunk, n_seq):
    batch, t, d = q.shape
    n_heads = d // HG_DK
    rb = c_len * n_chunk
    assert t % rb == 0 and batch % n_seq == 0 and n_heads % 2 == 0 and n_seq <= LANES
    carried = t > rb
    rows = lambda b, c: (b, c, 0)
    state = lambda b, c: (b, 0, 0, 0)
    return pl.pallas_call(
        functools.partial(_hg_core_body, c_len=c_len, n_chunk=n_chunk, n_seq=n_seq, n_heads=n_heads, carried=carried),
        grid=(batch // n_seq, t // rb),
        in_specs=[pl.BlockSpec((n_seq, rb, d), rows)] * 4 + [pl.BlockSpec((n_seq, n_heads, HG_DK, HG_DK), state)],
        out_specs=[pl.BlockSpec((n_seq, rb, d), rows), pl.BlockSpec((n_seq, n_heads, HG_DK, HG_DK), state)],
        out_shape=[jax.ShapeDtypeStruct((batch, t, d), F32),
                   jax.ShapeDtypeStruct((batch, n_heads, HG_DK, HG_DK), F32)],
        scratch_shapes=[pltpu.VMEM((n_seq if carried else 1, n_heads // 2, 2 * HG_DK, 2 * HG_DK), F32),
                        pltpu.VMEM((c_len, d), F32),
                        pltpu.VMEM((n_seq * n_chunk, c_len, d), F32)],
        compiler_params=_params("arbitrary", "arbitrary"),
        name=f"hgrn_core_{c_len}",
    )(q, k, lf, v, s0)


def _hg_out_router_body(op_ref, os_ref, gp_ref, gs_ref, x_ref, gn_ref, wo_ref, g_ref, wr_ref, br_ref,
                        x3_ref, ids_ref, wts_ref, wo_bf, *, npt, n_heads, n_exp):
    i = pl.program_id(0)

    @pl.when(i == 0)
    def _():
        wo_bf[...] = wo_ref[...].astype(BF16)

    dk = HG_DK
    o = jnp.where(i < npt, op_ref[...], os_ref[...])
    gate = jnp.where(i < npt, gp_ref[...], gs_ref[...])
    gn = gn_ref[...]
    normed = jnp.concatenate([_rms(o[:, h * dk:(h + 1) * dk], gn) for h in range(n_heads)], axis=1)
    y = (normed * gate).astype(BF16)
    x3 = x_ref[...] + jnp.dot(y, wo_bf[...], preferred_element_type=F32)
    x3_ref[...] = x3
    h4 = _rms(x3, g_ref[...])
    h_hi = h4.astype(BF16)
    h_lo = (h4 - h_hi.astype(F32)).astype(BF16)
    wr = wr_ref[...]
    w_hi = wr.astype(BF16)
    w_lo = (wr - w_hi.astype(F32)).astype(BF16)
    logits = (jnp.dot(h_hi, w_hi, preferred_element_type=F32)
              + (jnp.dot(h_lo, w_hi, preferred_element_type=F32) + jnp.dot(h_hi, w_lo, preferred_element_type=F32))
              + br_ref[...])
    lane = lax.broadcasted_iota(jnp.int32, logits.shape, 1)
    logits = jnp.where(lane < n_exp, logits, -jnp.inf)
    m1 = jnp.max(logits, axis=-1, keepdims=True)
    i1 = jnp.min(jnp.where(logits == m1, lane, LANES), axis=-1, keepdims=True)
    rest = jnp.where(lane == i1, -jnp.inf, logits)
    m2 = jnp.max(rest, axis=-1, keepdims=True)
    i2 = jnp.min(jnp.where(rest == m2, lane, LANES), axis=-1, keepdims=True)
    e2 = jnp.exp(m2 - m1)
    den = 1.0 + e2
    ids_ref[...] = jnp.where(lane == 0, i1, jnp.where(lane == 1, i2, 0))
    wts_ref[...] = jnp.where(lane == 0, 1.0 / den, jnp.where(lane == 1, e2 / den, 0.0))


def _hg_out_router(o_p, o_s, gate_p, gate_s, x, gn, wo, g, w_router, b_router, n_exp):
    n, d = x.shape
    n_p, n_s = o_p.shape[0], o_s.shape[0]
    tm = _pick_tile(math.gcd(n_p, n_s), 512)
    npt = n_p // tm
    row = lambda i: (i, 0)
    fix = lambda i: (0, 0)
    rp, rs = _split_maps(npt)
    return pl.pallas_call(
        functools.partial(_hg_out_router_body, npt=npt, n_heads=d // HG_DK, n_exp=n_exp),
        grid=(n // tm,),
        in_specs=[pl.BlockSpec((tm, d), rp), pl.BlockSpec((tm, d), rs), pl.BlockSpec((tm, d), rp),
                  pl.BlockSpec((tm, d), rs), pl.BlockSpec((tm, d), row),
                  pl.BlockSpec((1, HG_DK), fix), pl.BlockSpec(wo.shape, fix), pl.BlockSpec((1, d), fix),
                  pl.BlockSpec((d, LANES), fix), pl.BlockSpec((1, LANES), fix)],
        out_specs=[pl.BlockSpec((tm, d), row), pl.BlockSpec((tm, LANES), row), pl.BlockSpec((tm, LANES), row)],
        out_shape=[jax.ShapeDtypeStruct((n, d), F32), jax.ShapeDtypeStruct((n, LANES), jnp.int32),
                   jax.ShapeDtypeStruct((n, LANES), F32)],
        scratch_shapes=[pltpu.VMEM(wo.shape, BF16)],
        compiler_params=_params("arbitrary"),
        name="hgrn_out_router",
    )(o_p, o_s, gate_p, gate_s, x, gn, wo, g, w_router, b_router)


def _moe_body(te_ref, nv_ref, first_ref, last_ref, rin0_ref, rin_next_ref, rout_prev_ref, rout_cur_ref,
              x_hbm, g_ref, wg_ref, wu_ref, wd_ref, y_hbm,
              xg_ref, h_ref, acc_ref, wg_res, wu_res, wd_res, sem_in, sem_out, *, tm, rows_per_step):
    del te_ref
    i = pl.program_id(0)
    c = pl.program_id(1)
    slot = i % 2
    other = 1 - slot

    def row_in(buf, j, token):
        return pltpu.make_async_copy(x_hbm.at[pl.ds(token, 1)], xg_ref.at[buf, pl.ds(j, 1)], sem_in)

    def row_out(buf, j, dst):
        return pltpu.make_async_copy(acc_ref.at[buf, pl.ds(j, 1)], y_hbm.at[pl.ds(dst, 1)], sem_out)

    def all_in(buf):
        return pltpu.make_async_copy(x_hbm.at[pl.ds(0, tm)], xg_ref.at[buf], sem_in)

    def all_out(buf):
        return pltpu.make_async_copy(acc_ref.at[buf], y_hbm.at[pl.ds(0, tm)], sem_out)

    def each_row(fn):
        def body(j, carry):
            fn(j)
            return carry
        lax.fori_loop(0, tm, body, 0, unroll=8)

    @pl.when(nv_ref[i] > 0)
    def _():
        @pl.when((i == 0) & (c == 0))
        def _():
            acc_ref[1] = jnp.zeros(acc_ref.shape[1:], F32)
            each_row(lambda j: row_in(0, j, rin0_ref[0, 0, j]).start())
            all_in(0).wait()

        @pl.when(c == 0)
        def _():
            h_ref[...] = _rms(xg_ref[slot], g_ref[...]).astype(BF16)
            acc_ref[slot] = jnp.zeros(acc_ref.shape[1:], F32)

        def stream_rows():
            for u in range(rows_per_step):
                j = c * rows_per_step + u
                row_in(other, j, rin_next_ref[0, 0, j]).start()
                row_out(other, j, rout_prev_ref[0, 0, j]).start(priority=1)

        @pl.when(first_ref[i] > 0)
        def _():
            stream_rows()
            wg_res[c] = wg_ref[...].astype(BF16)
            wu_res[c] = wu_ref[...].astype(BF16)
            wd_res[c] = wd_ref[...].astype(BF16)
            _swiglu_step(h_ref[...], wg_res[c], wu_res[c], wd_res[c], acc_ref.at[slot])

        @pl.when(first_ref[i] == 0)
        def _():
            stream_rows()
            _swiglu_step(h_ref[...], wg_res[c], wu_res[c], wd_res[c], acc_ref.at[slot])

        @pl.when(c == pl.num_programs(1) - 1)
        def _():
            all_in(other).wait()
            all_out(other).wait()

            @pl.when(i == last_ref[0])
            def _():
                each_row(lambda j: row_out(slot, j, rout_cur_ref[0, 0, j]).start())
                all_out(slot).wait()


MOE_TILE_ROWS = 672
BF16_SUBLANES = 16


def _moe_experts(x, g, ids, w_gu, w_down):
    n, d = x.shape
    n_exp, d_ff = w_down.shape[0], w_down.shape[1]
    tf = _ff_chunk(d_ff)
    nc = d_ff // tf
    unit = BF16_SUBLANES * nc
    tm = unit * max(1, round(MOE_TILE_ROWS / unit))
    n_asg = n * TOP_K
    n_tiles = (n_asg + n_exp * (tm - 1) + tm - 1) // tm

    e_flat = ids.T.reshape(n_asg)
    onehot = (e_flat[:, None] == jnp.arange(n_exp, dtype=jnp.int32)[None, :]).astype(jnp.int32)
    counts = jnp.sum(onehot, axis=0)
    rank = jnp.sum((jnp.cumsum(onehot, axis=0) - onehot) * onehot, axis=1)
    tiles_per = (counts + tm - 1) // tm
    tile_end = jnp.cumsum(tiles_per)
    tile_start = tile_end - tiles_per
    slot = jnp.sum(onehot * tile_start[None, :], axis=1) * tm + rank
    asg = jnp.full((n_tiles * tm,), -1, jnp.int32).at[slot].set(jnp.arange(n_asg, dtype=jnp.int32),
                                                                 unique_indices=True)
    pad_row = n_asg + jnp.arange(n_tiles * tm, dtype=jnp.int32) % tm
    rows_in = jnp.where(asg >= 0, jnp.where(asg >= n, asg - n, asg), 0)
    rows_out = jnp.where(asg >= 0, asg, pad_row)
    rows_in_ext = jnp.concatenate([rows_in, jnp.zeros((tm,), jnp.int32)]).reshape(n_tiles + 1, 1, tm)
    rows_out_ext = jnp.concatenate([pad_row[:tm], rows_out]).reshape(n_tiles + 1, 1, tm)
    tile = jnp.arange(n_tiles, dtype=jnp.int32)
    last = tile_end[-1] - 1
    tile_c = jnp.minimum(tile, last)
    te = jnp.minimum(jnp.sum((tile_c[:, None] >= tile_end[None, :]).astype(jnp.int32), axis=1), n_exp - 1)
    te_hot = (te[:, None] == jnp.arange(n_exp, dtype=jnp.int32)[None, :]).astype(jnp.int32)
    cnt_t = jnp.sum(te_hot * counts[None, :], axis=1)
    start_t = jnp.sum(te_hot * tile_start[None, :], axis=1)
    nv = jnp.where(tile <= last, jnp.clip(cnt_t - (tile - start_t) * tm, 0, tm), 0).astype(jnp.int32)

    first = ((tile == start_t) & (nv > 0)).astype(jnp.int32)

    def chunk(i, c, nv_ref, first_ref):
        return jnp.where((nv_ref[i] > 0) & (first_ref[i] > 0), c, nc - 1)

    smem_rows = lambda index: pl.BlockSpec((1, 1, tm), index, memory_space=pltpu.SMEM)
    return pl.pallas_call(
        functools.partial(_moe_body, tm=tm, rows_per_step=tm // nc),
        grid_spec=pltpu.PrefetchScalarGridSpec(
            num_scalar_prefetch=4,
            grid=(n_tiles, nc),
            in_specs=[smem_rows(lambda i, c, te_ref, nv_ref, first_ref, last_ref: (0, 0, 0)),
                      smem_rows(lambda i, c, te_ref, nv_ref, first_ref, last_ref: (i + 1, 0, 0)),
                      smem_rows(lambda i, c, te_ref, nv_ref, first_ref, last_ref: (i, 0, 0)),
                      smem_rows(lambda i, c, te_ref, nv_ref, first_ref, last_ref: (i + 1, 0, 0)),
                      pl.BlockSpec(memory_space=pl.ANY),
                      pl.BlockSpec((1, d), lambda i, c, te_ref, nv_ref, first_ref, last_ref: (0, 0)),
                      pl.BlockSpec((None, d, tf), lambda i, c, te_ref, nv_ref, first_ref, last_ref:
                                   (te_ref[i], 0, chunk(i, c, nv_ref, first_ref))),
                      pl.BlockSpec((None, d, tf), lambda i, c, te_ref, nv_ref, first_ref, last_ref:
                                   (te_ref[i], 0, chunk(i, c, nv_ref, first_ref) + nc)),
                      pl.BlockSpec((None, tf, d), lambda i, c, te_ref, nv_ref, first_ref, last_ref:
                                   (te_ref[i], chunk(i, c, nv_ref, first_ref), 0))],
            out_specs=pl.BlockSpec(memory_space=pl.ANY),
            scratch_shapes=[pltpu.VMEM((2, tm, d), F32), pltpu.VMEM((tm, d), BF16), pltpu.VMEM((2, tm, d), F32),
                            pltpu.VMEM((nc, d, tf), BF16), pltpu.VMEM((nc, d, tf), BF16), pltpu.VMEM((nc, tf, d), BF16),
                            pltpu.SemaphoreType.DMA, pltpu.SemaphoreType.DMA]),
        out_shape=jax.ShapeDtypeStruct((n_asg + tm, d), F32),
        compiler_params=_params("arbitrary", "arbitrary"),
        name="moe_experts",
    )(te.astype(jnp.int32), nv, first, last.reshape(1).astype(jnp.int32), rows_in_ext, rows_in_ext, rows_out_ext,
      rows_out_ext, x, g, w_gu, w_gu, w_down)


def _combine_body(x_ref, y0_ref, y1_ref, wts_ref, g_ref, outp_ref, outs_ref, *, npt):
    wts = wts_ref[...]
    moe = wts[:, 0:1] * y0_ref[...] + wts[:, 1:2] * y1_ref[...]
    _store_split(pl.program_id(0) < npt, outp_ref, outs_ref, _rms(x_ref[...] + moe, g_ref[...]))


def _combine(x, y2, wts, g, n_p):
    n, d = x.shape
    n_s = n - n_p
    tm = _pick_tile(math.gcd(n_p, n_s), 512)
    nt = n // tm
    npt = n_p // tm
    row = lambda i: (i, 0)
    rp, rs = _split_maps(npt)
    return pl.pallas_call(
        functools.partial(_combine_body, npt=npt),
        grid=(nt,),
        in_specs=[pl.BlockSpec((tm, d), row), pl.BlockSpec((tm, d), row), pl.BlockSpec((tm, d), lambda i: (i + nt, 0)),
                  pl.BlockSpec((tm, LANES), row), pl.BlockSpec((1, d), lambda i: (0, 0))],
        out_specs=[pl.BlockSpec((tm, d), rp), pl.BlockSpec((tm, d), rs)],
        out_shape=[jax.ShapeDtypeStruct((n_p, d), F32), jax.ShapeDtypeStruct((n_s, d), F32)],
        compiler_params=_params("arbitrary"),
        name="moe_combine",
    )(x, y2, y2, wts, g)


def _rope_tables(pos, hd):
    half = hd // 2
    inv = ROPE_THETA ** (-jnp.arange(half, dtype=F32) / half)
    ang = pos.astype(F32)[:, None] * inv[None, :]
    cos = jnp.cos(ang)
    sin = jnp.sin(ang)
    reps = LANES // hd
    return jnp.tile(jnp.concatenate([cos, cos], axis=1), (1, reps)), jnp.tile(jnp.concatenate([-sin, sin], axis=1), (1, reps))


def kernel(x_prompt, x_sample, cache_k_win, cache_v_win, state_hgrn, norm_mix, norm_ffn, norm_final,
           w_qkv, b_qkv, w_o_attn, b_o_attn, sinks, w_in_hg, hg_lower, hg_norm, w_o_hg,
           w_gu_dense, w_down_dense, w_router, b_router, w_gu_moe, w_down_moe):
    batch, seq, d = x_prompt.shape
    n_seq, t_dec, _ = x_sample.shape
    n_kv, hd = cache_k_win.shape[3], cache_k_win.shape[4]
    n_heads = sinks.shape[1]
    g_per = n_heads // n_kv
    n_q, n_k = n_heads * hd, n_kv * hd
    n_p, n_s = batch * seq, n_seq * t_dec
    n_exp = w_router.shape[2]
    wc = cache_k_win.shape[2]
    assert norm_mix.shape[0] == 2 and hd * 2 == LANES and d % HG_DK == 0 and seq % WINDOW == 0 and wc == WINDOW

    pos = jnp.concatenate([jnp.tile(jnp.arange(seq), batch), jnp.tile(PAST_LEN + jnp.arange(t_dec), n_seq)])
    cos_t, sin_t = _rope_tables(pos, hd)
    x, q, k, v = _qkv_rope(x_prompt.reshape(n_p, d), x_sample.reshape(n_s, d), norm_mix[0:1], w_qkv[0], b_qkv[0:1],
                           cos_t, sin_t, n_q, n_k, hd)
    o = _swa_prompt(q, k, v, sinks[0], jnp.zeros((n_p + n_s, n_q), BF16), batch, seq, n_kv, g_per, hd)
    kc = cache_k_win[0].reshape(n_seq, wc, n_k)
    vc = cache_v_win[0].reshape(n_seq, wc, n_k)
    o = _swa_sample(q, k, v, kc, vc, sinks[0], o, n_p, n_seq, t_dec, n_kv, g_per, hd)
    last_win = lambda a: jnp.stack([a[(b + 1) * seq - wc:(b + 1) * seq] for b in range(batch)]).reshape(batch, wc, n_kv, hd)
    k_win_p, v_win_p = last_win(k), last_win(v)
    k_win_s = jnp.concatenate([kc[:, t_dec:], k[n_p:].reshape(n_seq, t_dec, n_k)], axis=1).reshape(n_seq, wc, n_kv, hd)
    v_win_s = jnp.concatenate([vc[:, t_dec:], v[n_p:].reshape(n_seq, t_dec, n_k)], axis=1).reshape(n_seq, wc, n_kv, hd)
    x = _oproj_ffn(o, x, w_o_attn[0], b_o_attn[0:1], norm_ffn[0:1], w_gu_dense[0], w_down_dense[0])

    lb_sm = jax.nn.softmax(hg_lower.astype(F32), axis=0)
    lb = (jnp.cumsum(lb_sm, axis=0) - lb_sm[0])[1:2]
    in_part = functools.partial(_hg_inproj_part, x, norm_mix[1:2], w_in_hg[0], lb, n_p)
    hq, sq = in_part("q")
    hlf, hk, slf, sk = in_part("f")
    hv, sv = in_part("i")
    hgate, sgate = in_part("g")
    n_hh = d // HG_DK
    c_p = math.gcd(seq, HG_CHUNK)
    as_seq = lambda a: a.reshape(batch, seq, d)
    o_p, s_p = _hg_core(as_seq(hq), as_seq(hk), as_seq(hlf), as_seq(hv), jnp.zeros((batch, n_hh, HG_DK, HG_DK), F32),
                        c_p, 2 if seq % (2 * c_p) == 0 else 1, 2 if batch % 2 == 0 else 1)
    c_s = SUBLANES * ((t_dec + SUBLANES - 1) // SUBLANES)
    pad = lambda a: jnp.pad(a.reshape(n_seq, t_dec, d), ((0, 0), (0, c_s - t_dec), (0, 0)))
    o_s, s_s = _hg_core(pad(sq), pad(sk), pad(slf), pad(sv), state_hgrn[0], c_s, 1, 8 if n_seq % 8 == 0 else 1)
    o_s = o_s[:, :t_dec].reshape(n_s, d)

    wr = jnp.pad(w_router[0], ((0, 0), (0, LANES - n_exp)))
    br = jnp.pad(b_router[0:1], ((0, 0), (0, LANES - n_exp)))
    x3, ids, wts = _hg_out_router(o_p.reshape(n_p, d), o_s, hgate, sgate, x, hg_norm[0:1], w_o_hg[0], norm_ffn[1:2],
                                  wr, br, n_exp)
    y2 = _moe_experts(x3, norm_ffn[1:2], ids[:, :TOP_K], w_gu_moe[0], w_down_moe[0])
    y_p, y_s = _combine(x3, y2, wts, norm_final.reshape(1, d), n_p)

    return (y_p.reshape(batch, seq, d), y_s.reshape(n_seq, t_dec, d),
            k_win_p[None], v_win_p[None], k_win_s[None], v_win_s[None], s_p[None], s_s[None])
```

```python
import functools
import math

import jax
import jax.numpy as jnp
from jax import lax
from jax.experimental import pallas as pl
from jax.experimental.pallas import tpu as pltpu

F32 = jnp.float32
BF16 = jnp.bfloat16

NORM_EPS = 1e-5
WINDOW = 128
PAST_LEN = 16384
ROPE_THETA = 10000.0
HG_DK = 128
HG_CHUNK = 64
HG_SAFE_DECAY = 60.0
HG_SAFE_Q = 1e9
TOP_K = 2
LANES = 128
SUBLANES = 8
VMEM_LIMIT = 56 * 1024 * 1024

NT_DIMS = (((1,), (1,)), ((), ()))
TN_DIMS = (((0,), (0,)), ((), ()))


def _pick_tile(n, target):
    for t in (1536, 1024, 768, 512, 384, 256, 192, 128, 64, 32, 16, 8):
        if t <= target and n % t == 0:
            return t
    raise ValueError(f"no row tile for {n}")


def _params(*sem):
    return pltpu.CompilerParams(dimension_semantics=sem, vmem_limit_bytes=VMEM_LIMIT)


def _rms(x, g):
    return x * lax.rsqrt(jnp.mean(x * x, axis=-1, keepdims=True) + NORM_EPS) * g


def _split_maps(npt):
    return (lambda i, *_: (jnp.minimum(i, npt - 1), 0)), (lambda i, *_: (jnp.maximum(i - npt, 0), 0))


def _store_split(is_prompt, ref_p, ref_s, val):
    @pl.when(is_prompt)
    def _():
        ref_p[...] = val

    @pl.when(jnp.logical_not(is_prompt))
    def _():
        ref_s[...] = val


def _qkv_body(xp_ref, xs_ref, g_ref, w_ref, b_ref, cosp_ref, sinp_ref, coss_ref, sins_ref,
              x_ref, q_ref, k_ref, v_ref, wbf_ref,
              *, npt, n_q, n_k, hd):
    i = pl.program_id(0)

    @pl.when(i == 0)
    def _():
        wbf_ref[...] = w_ref[...].astype(BF16)

    x = jnp.where(i < npt, xp_ref[...], xs_ref[...])
    x_ref[...] = x
    h = _rms(x, g_ref[...]).astype(BF16)
    y = jnp.dot(h, wbf_ref[...], preferred_element_type=F32) + b_ref[...]
    cos = jnp.where(i < npt, cosp_ref[...], coss_ref[...])
    sin = jnp.where(i < npt, sinp_ref[...], sins_ref[...])
    lane = lax.broadcasted_iota(jnp.int32, cos.shape, 1)
    first = (lane % hd) < (hd // 2)

    def rope(blk):
        partner = jnp.where(first, pltpu.roll(blk, LANES - hd // 2, 1), pltpu.roll(blk, hd // 2, 1))
        return blk * cos + partner * sin

    scale = hd ** -0.5
    for j in range(n_q // LANES):
        q_ref[:, j * LANES:(j + 1) * LANES] = (rope(y[:, j * LANES:(j + 1) * LANES]) * scale).astype(BF16)
    for j in range(n_k // LANES):
        k_ref[:, j * LANES:(j + 1) * LANES] = rope(y[:, n_q + j * LANES:n_q + (j + 1) * LANES])
    v_ref[...] = y[:, n_q + n_k:]


def _qkv_rope(xp, xs, g, w, b, rope_p, rope_s, n_q, n_k, hd):
    (n_p, d), n_s = xp.shape, xs.shape[0]
    n = n_p + n_s
    n_out = w.shape[1]
    tm = _pick_tile(math.gcd(n_p, n_s), 512)
    npt = n_p // tm
    row = lambda i: (i, 0)
    fix = lambda i: (0, 0)
    rp, rs = _split_maps(npt)
    seq_tiles = rope_p[0].shape[0] // tm
    assert rope_p[0].shape[0] % tm == 0 and rope_s[0].shape[0] == tm
    pos_p = lambda i: (jnp.minimum(i, npt - 1) % seq_tiles, 0)
    return pl.pallas_call(
        functools.partial(_qkv_body, npt=npt, n_q=n_q, n_k=n_k, hd=hd),
        grid=(n // tm,),
        in_specs=[pl.BlockSpec((tm, d), rp), pl.BlockSpec((tm, d), rs), pl.BlockSpec((1, d), fix),
                  pl.BlockSpec((d, n_out), fix), pl.BlockSpec((1, n_out), fix),
                  pl.BlockSpec((tm, LANES), pos_p), pl.BlockSpec((tm, LANES), pos_p),
                  pl.BlockSpec((tm, LANES), fix), pl.BlockSpec((tm, LANES), fix)],
        out_specs=[pl.BlockSpec((tm, d), row), pl.BlockSpec((tm, n_q), row), pl.BlockSpec((tm, n_k), row),
                   pl.BlockSpec((tm, n_k), row)],
        out_shape=[jax.ShapeDtypeStruct((n, d), F32), jax.ShapeDtypeStruct((n, n_q), BF16),
                   jax.ShapeDtypeStruct((n, n_k), F32), jax.ShapeDtypeStruct((n, n_k), F32)],
        scratch_shapes=[pltpu.VMEM((d, n_out), BF16)],
        compiler_params=_params("arbitrary"),
        name="qkv_rope",
    )(xp, xs, g, w, b, *rope_p, *rope_s)


def _sink_column(sink_ref, kh, g_per, rows_per):
    blk = lax.broadcasted_iota(jnp.int32, (g_per * rows_per, 1), 0) // rows_per
    col = jnp.full((g_per * rows_per, 1), sink_ref[kh * g_per], F32)
    for g in range(1, g_per):
        col = jnp.where(blk == g, sink_ref[kh * g_per + g], col)
    return col


def _stack_heads(q, kh, g_per, hd):
    return jnp.concatenate([q[:, (kh * g_per + g) * hd:(kh * g_per + g + 1) * hd] for g in range(g_per)], axis=0)


def _swa_prompt_body(sink_ref, q_ref, kp_ref, kc_ref, vp_ref, vc_ref, o_all_ref, o_ref, *, n_kv, g_per, hd, w):
    del o_all_ref
    n = pl.program_id(1)
    q = q_ref[...]
    kk = jnp.concatenate([kp_ref[...], kc_ref[...]], axis=0).astype(BF16)
    vv = jnp.concatenate([vp_ref[...], vc_ref[...]], axis=0).astype(BF16)
    i = lax.broadcasted_iota(jnp.int32, (g_per * w, 2 * w), 0) % w
    j = lax.broadcasted_iota(jnp.int32, (g_per * w, 2 * w), 1)
    mask = (j > i) & (j <= i + w) & ((j >= w) | (n > 0))
    outs = []
    for kh in range(n_kv):
        q4 = _stack_heads(q, kh, g_per, hd)
        s = lax.dot_general(q4, kk[:, kh * hd:(kh + 1) * hd], NT_DIMS, preferred_element_type=F32)
        s = jnp.where(mask, s, -jnp.inf)
        sink = _sink_column(sink_ref, kh, g_per, w)
        m = jnp.maximum(jnp.max(s, axis=-1, keepdims=True), sink)
        p = jnp.exp(s - m)
        p = p / (jnp.sum(p, axis=-1, keepdims=True) + jnp.exp(sink - m))
        o4 = jnp.dot(p.astype(BF16), vv[:, kh * hd:(kh + 1) * hd], preferred_element_type=F32)
        outs.append(jnp.concatenate([o4[g * w:(g + 1) * w] for g in range(g_per)], axis=1))
    o_ref[...] = jnp.concatenate(outs, axis=1).astype(BF16)


def _swa_prompt(q, k, v, sinks, o_all, batch, seq, n_kv, g_per, hd):
    w = WINDOW
    nb = seq // w
    dq = q.shape[1]
    dkv = k.shape[1]
    cur = lambda b, n: (b * nb + n, 0)
    prev = lambda b, n: (b * nb + jnp.maximum(n - 1, 0), 0)
    return pl.pallas_call(
        functools.partial(_swa_prompt_body, n_kv=n_kv, g_per=g_per, hd=hd, w=w),
        grid=(batch, nb),
        in_specs=[pl.BlockSpec(memory_space=pltpu.SMEM),
                  pl.BlockSpec((w, dq), cur), pl.BlockSpec((w, dkv), prev), pl.BlockSpec((w, dkv), cur),
                  pl.BlockSpec((w, dkv), prev), pl.BlockSpec((w, dkv), cur),
                  pl.BlockSpec(memory_space=pl.ANY)],
        out_specs=pl.BlockSpec((w, dq), cur),
        out_shape=jax.ShapeDtypeStruct(o_all.shape, o_all.dtype),
        input_output_aliases={6: 0},
        compiler_params=_params("arbitrary", "arbitrary"),
        name="swa_prompt",
    )(sinks, q, k, k, v, v, o_all)


def _swa_sample_body(sink_ref, q_ref, kn_ref, vn_ref, kc_ref, vc_ref, o_all_ref, o_ref, *, n_kv, g_per, hd, bt, t, wc):
    del o_all_ref
    r = bt * t
    q = q_ref[...]
    kn = kn_ref[...].astype(BF16)
    vn = vn_ref[...].astype(BF16)
    kc = kc_ref[...].reshape(bt * wc, n_kv * hd).astype(BF16)
    vc = vc_ref[...].reshape(bt * wc, n_kv * hd).astype(BF16)
    row_c = lax.broadcasted_iota(jnp.int32, (g_per * r, bt * wc), 0) % r
    col_c = lax.broadcasted_iota(jnp.int32, (g_per * r, bt * wc), 1)
    mask_c = (col_c // wc == row_c // t) & (col_c % wc > row_c % t + (wc - WINDOW))
    row_n = lax.broadcasted_iota(jnp.int32, (g_per * r, r), 0) % r
    col_n = lax.broadcasted_iota(jnp.int32, (g_per * r, r), 1)
    mask_n = (col_n // t == row_n // t) & (col_n % t <= row_n % t)
    outs = []
    for kh in range(n_kv):
        hs = slice(kh * hd, (kh + 1) * hd)
        q4 = _stack_heads(q, kh, g_per, hd)
        sc = jnp.where(mask_c, lax.dot_general(q4, kc[:, hs], NT_DIMS, preferred_element_type=F32), -jnp.inf)
        sn = jnp.where(mask_n, lax.dot_general(q4, kn[:, hs], NT_DIMS, preferred_element_type=F32), -jnp.inf)
        sink = _sink_column(sink_ref, kh, g_per, r)
        m = jnp.maximum(jnp.maximum(jnp.max(sc, axis=-1, keepdims=True), jnp.max(sn, axis=-1, keepdims=True)), sink)
        pc = jnp.exp(sc - m)
        pn = jnp.exp(sn - m)
        den = jnp.sum(pc, axis=-1, keepdims=True) + jnp.sum(pn, axis=-1, keepdims=True) + jnp.exp(sink - m)
        o4 = (jnp.dot((pc / den).astype(BF16), vc[:, hs], preferred_element_type=F32)
              + jnp.dot((pn / den).astype(BF16), vn[:, hs], preferred_element_type=F32))
        outs.append(jnp.concatenate([o4[g * r:(g + 1) * r] for g in range(g_per)], axis=1))
    o_ref[...] = jnp.concatenate(outs, axis=1).astype(BF16)


def _swa_sample(q, k, v, k_cache, v_cache, sinks, o_all, row0, n_seq, t, n_kv, g_per, hd):
    wc = k_cache.shape[1]
    dq = q.shape[1]
    dkv = k.shape[1]
    bt = 8 if n_seq % 8 == 0 else n_seq
    r = bt * t
    assert row0 % r == 0
    off = row0 // r
    rows = lambda i: (off + i, 0)
    return pl.pallas_call(
        functools.partial(_swa_sample_body, n_kv=n_kv, g_per=g_per, hd=hd, bt=bt, t=t, wc=wc),
        grid=(n_seq // bt,),
        in_specs=[pl.BlockSpec(memory_space=pltpu.SMEM),
                  pl.BlockSpec((r, dq), rows), pl.BlockSpec((r, dkv), rows), pl.BlockSpec((r, dkv), rows),
                  pl.BlockSpec((bt, wc, dkv), lambda i: (i, 0, 0)), pl.BlockSpec((bt, wc, dkv), lambda i: (i, 0, 0)),
                  pl.BlockSpec(memory_space=pl.ANY)],
        out_specs=pl.BlockSpec((r, dq), rows),
        out_shape=jax.ShapeDtypeStruct(o_all.shape, o_all.dtype),
        input_output_aliases={6: 0},
        compiler_params=_params("arbitrary"),
        name="swa_sample",
    )(sinks, q, k, v, k_cache, v_cache, o_all)


def _swiglu_step(h, wg, wu, wd, acc_ref):
    a = jnp.dot(h, wg, preferred_element_type=F32)
    b = jnp.dot(h, wu, preferred_element_type=F32)
    act = (a * jax.nn.sigmoid(a) * b).astype(BF16)
    acc_ref[...] += jnp.dot(act, wd, preferred_element_type=F32)


def _oproj_ffn_body(o_ref, x_ref, wo_ref, bo_ref, g_ref, wg_ref, wu_ref, wd_ref, out_ref,
                    wo_bf, x1_ref, h_ref, acc_ref):
    i = pl.program_id(0)
    c = pl.program_id(1)

    @pl.when((i == 0) & (c == 0))
    def _():
        wo_bf[...] = wo_ref[...].astype(BF16)

    @pl.when(c == 0)
    def _():
        x1 = x_ref[...] + jnp.dot(o_ref[...], wo_bf[...], preferred_element_type=F32) + bo_ref[...]
        x1_ref[...] = x1
        h_ref[...] = _rms(x1, g_ref[...]).astype(BF16)
        acc_ref[...] = jnp.zeros_like(acc_ref)

    _swiglu_step(h_ref[...], wg_ref[...].astype(BF16), wu_ref[...].astype(BF16), wd_ref[...].astype(BF16), acc_ref)

    @pl.when(c == pl.num_programs(1) - 1)
    def _():
        out_ref[...] = x1_ref[...] + acc_ref[...]


def _ff_chunk(d_ff):
    for tf in (512, 256, 128):
        if d_ff % tf == 0:
            return tf
    raise ValueError(f"d_ff {d_ff} is not a multiple of {LANES}")


def _oproj_ffn(o, x, wo, bo, g, w_gu, w_down):
    n, d = x.shape
    d_ff = w_down.shape[0]
    tf = _ff_chunk(d_ff)
    nc = d_ff // tf
    tm = _pick_tile(n, 768)
    row = lambda i, c: (i, 0)
    fix = lambda i, c: (0, 0)
    return pl.pallas_call(
        _oproj_ffn_body,
        grid=(n // tm, nc),
        in_specs=[pl.BlockSpec((tm, o.shape[1]), row), pl.BlockSpec((tm, d), row),
                  pl.BlockSpec(wo.shape, fix), pl.BlockSpec((1, d), fix), pl.BlockSpec((1, d), fix),
                  pl.BlockSpec((d, tf), lambda i, c: (0, c)), pl.BlockSpec((d, tf), lambda i, c: (0, c + nc)),
                  pl.BlockSpec((tf, d), lambda i, c: (c, 0))],
        out_specs=pl.BlockSpec((tm, d), row),
        out_shape=jax.ShapeDtypeStruct((n, d), F32),
        scratch_shapes=[pltpu.VMEM(wo.shape, BF16), pltpu.VMEM((tm, d), F32), pltpu.VMEM((tm, d), BF16),
                        pltpu.VMEM((tm, d), F32)],
        compiler_params=_params("arbitrary", "arbitrary"),
        name="oproj_ffn",
    )(o, x, wo, bo, g, w_gu, w_gu, w_down)


def _hg_in_body(x_ref, g_ref, w_ref, lb_ref, *refs, npt, half):
    out_refs, (wbf_ref, tmp_ref) = refs[:-2], refs[-2:]
    n_res = len(out_refs) // 2
    i = pl.program_id(0)
    is_p = i < npt
    d = x_ref.shape[1]

    @pl.when(i == 0)
    def _():
        wbf_ref[...] = w_ref[...].astype(BF16)

    h = _rms(x_ref[...], g_ref[...]).astype(BF16)
    z = jnp.dot(h, wbf_ref[...], preferred_element_type=F32)
    z0, z1 = z[:, :d], z[:, d:]

    def emit(k, val):
        tmp_ref[...] = val
        _store_split(is_p, out_refs[k], out_refs[n_res + k], tmp_ref[...])

    if half == 0:
        emit(0, z0 * jax.nn.sigmoid(z0) * (HG_DK ** -0.5))
        lb = lb_ref[...]
        t = jnp.exp(-jnp.abs(z1))
        log_sig = jnp.minimum(z1, 0.0) - jnp.log(1.0 + t)
        a = jnp.log(lb)
        b = jnp.log1p(-lb) + log_sig
        emit(1, jnp.maximum(a, b) + jnp.log(1.0 + jnp.exp(-jnp.abs(a - b))))
        emit(2, (1.0 - lb) * (jnp.where(z1 >= 0.0, t, 1.0) / (1.0 + t)))
    else:
        emit(0, z0)
        emit(1, z1 * jax.nn.sigmoid(z1))


def _hg_inproj_half(x, g, w_in, lb, n_p, half):
    n, d = x.shape
    n_s = n - n_p
    n_res = 3 if half == 0 else 2
    tm = _pick_tile(math.gcd(n_p, n_s), 256)
    npt = n_p // tm
    row = lambda i: (i, 0)
    fix = lambda i: (0, 0)
    rp, rs = _split_maps(npt)
    return pl.pallas_call(
        functools.partial(_hg_in_body, npt=npt, half=half),
        grid=(n // tm,),
        in_specs=[pl.BlockSpec((tm, d), row), pl.BlockSpec((1, d), fix), pl.BlockSpec((d, 2 * d), lambda i: (0, half)),
                  pl.BlockSpec((1, d), fix)],
        out_specs=[pl.BlockSpec((tm, d), rp)] * n_res + [pl.BlockSpec((tm, d), rs)] * n_res,
        out_shape=[jax.ShapeDtypeStruct((n_p, d), F32)] * n_res + [jax.ShapeDtypeStruct((n_s, d), F32)] * n_res,
        scratch_shapes=[pltpu.VMEM((d, 2 * d), BF16), pltpu.VMEM((tm, d), F32)],
        compiler_params=_params("arbitrary"),
        name=f"hgrn_inproj_{half}",
    )(x, g, w_in, lb)


def _hg_core_body(q_ref, k_ref, lf_ref, v_ref, s0_ref, o_ref, sout_ref, st_ref, g_ref, oi_ref,
                  *, c_len, n_chunk, n_seq, n_heads, carried):
    c = pl.program_id(1)
    dk = HG_DK
    rb = c_len * n_chunk
    units = [(s, j) for s in range(n_seq) for j in range(n_chunk)]

    n_pairs = n_heads // 2
    state_diag = (lax.broadcasted_iota(jnp.int32, (2 * dk, 2 * dk), 0) // dk
                  == lax.broadcasted_iota(jnp.int32, (2 * dk, 2 * dk), 1) // dk)
    rows_diag = (lax.broadcasted_iota(jnp.int32, (2 * c_len, 2 * dk), 0) // c_len
                 == lax.broadcasted_iota(jnp.int32, (2 * c_len, 2 * dk), 1) // dk)

    def pair_tile(x):
        x2 = jnp.concatenate([x, x], axis=0)
        return jnp.where(rows_diag, x2, jnp.zeros_like(x2))

    if carried:
        @pl.when(c == 0)
        def _():
            for s in range(n_seq):
                for p in range(n_pairs):
                    st_ref[s, p] = jnp.zeros((2 * dk, 2 * dk), F32)
                    st_ref[s, p, 0:dk, 0:dk] = s0_ref[s, 2 * p].T
                    st_ref[s, p, dk:2 * dk, dk:2 * dk] = s0_ref[s, 2 * p + 1].T

    r = lax.broadcasted_iota(jnp.int32, (rb, rb), 0)
    cidx = lax.broadcasted_iota(jnp.int32, (rb, rb), 1)
    block_causal = ((r >= cidx) & (r // c_len == cidx // c_len)).astype(BF16)
    causal2 = (lax.broadcasted_iota(jnp.int32, (c_len, 2 * c_len), 0)
               >= lax.broadcasted_iota(jnp.int32, (c_len, 2 * c_len), 1) % c_len)

    def rows(j):
        return slice(j * c_len, (j + 1) * c_len)

    gcum, safe = [], None
    for s in range(n_seq):
        lf = lf_ref[s]
        lf_hi = lf.astype(BF16)
        rest = lf - lf_hi.astype(F32)
        lf_mid = rest.astype(BF16)
        lf_lo = (rest - lf_mid.astype(F32)).astype(BF16)
        gs = ((jnp.dot(block_causal, lf_lo, preferred_element_type=F32)
               + jnp.dot(block_causal, lf_mid, preferred_element_type=F32))
              + jnp.dot(block_causal, lf_hi, preferred_element_type=F32))
        gcum.append(gs)
        for j in range(n_chunk):
            ok = ((jnp.max(-gs[(j + 1) * c_len - 1:(j + 1) * c_len, :]) <= HG_SAFE_DECAY)
                  & (jnp.max(jnp.abs(q_ref[s, rows(j), :])) <= HG_SAFE_Q))
            safe = ok if safe is None else (safe & ok)

    def g_mid(s, j):
        return gcum[s][j * c_len + c_len // 2 - 1:j * c_len + c_len // 2, :]

    def g_last(s, j):
        return gcum[s][(j + 1) * c_len - 1:(j + 1) * c_len, :]

    @pl.when(safe)
    def _():
        for u, (s, j) in enumerate(units):
            g = gcum[s][rows(j), :]
            qi = (q_ref[s, rows(j), :] * jnp.exp(g - g_mid(s, j))).astype(BF16)
            ki = (k_ref[s, rows(j), :] * jnp.exp(g_mid(s, j) - g)).astype(BF16)
            vb = v_ref[s, rows(j), :].astype(BF16)
            for p in range(n_pairs):
                ps = slice(2 * p * dk, 2 * (p + 1) * dk)
                a = lax.dot_general(qi[:, ps], pair_tile(ki[:, ps]), NT_DIMS, preferred_element_type=F32)
                a = jnp.where(causal2, a, 0.0).astype(BF16)
                oi_ref[u, :, ps] = jnp.dot(a, pair_tile(vb[:, ps]), preferred_element_type=F32)

    @pl.when(jnp.logical_not(safe))
    def _():
        lane_h = lax.broadcasted_iota(jnp.int32, (n_heads * dk, n_heads * dk), 0) // dk
        lane_w = lax.broadcasted_iota(jnp.int32, (n_heads * dk, n_heads * dk), 1) // dk
        head_sum = (lane_h == lane_w).astype(BF16)
        t_idx = lax.broadcasted_iota(jnp.int32, (c_len, 1), 0)
        for u, (s, j) in enumerate(units):
            g = gcum[s][rows(j), :]
            g_ref[...] = g
            q = q_ref[s, rows(j), :]

            def key_row(i, acc, s=s, j=j, g=g, q=q):
                gi = g_ref[pl.ds(i, 1), :]
                decay = jnp.exp(jnp.where(t_idx >= i, g - gi, -jnp.inf))
                term = (q * decay * k_ref[s, pl.ds(j * c_len + i, 1), :]).astype(BF16)
                a_i = jnp.dot(term, head_sum, preferred_element_type=F32)
                return acc + a_i * v_ref[s, pl.ds(j * c_len + i, 1), :]

            oi_ref[u] = lax.fori_loop(0, c_len, key_row, jnp.zeros((c_len, n_heads * dk), F32))

    if not carried:
        e_rows = [jnp.exp(g_last(s, 0)) for s in range(n_seq)]
        e_cols = jnp.concatenate(e_rows + [jnp.zeros((LANES - n_seq, n_heads * dk), F32)], axis=0).T
        zero = jnp.zeros((dk, dk), F32)

    for u, (s, j) in enumerate(units):
        g = gcum[s][rows(j), :]
        qs = (q_ref[s, rows(j), :] * jnp.exp(g)).astype(BF16)
        ks = (k_ref[s, rows(j), :] * jnp.exp(g_last(s, j) - g)).astype(BF16)
        vb = v_ref[s, rows(j), :].astype(BF16)
        e_last = jnp.exp(g_last(s, j))
        for p in range(n_pairs):
            ps = slice(2 * p * dk, 2 * (p + 1) * dk)
            if carried:
                st = st_ref[s, p]
                o_ref[s, rows(j), ps] = oi_ref[u, :, ps] + lax.dot_general(qs[:, ps], st.astype(BF16), NT_DIMS,
                                                                           preferred_element_type=F32)
                upd = lax.dot_general(vb[:, ps], ks[:, ps], TN_DIMS, preferred_element_type=F32)
                st_ref[s, p] = jnp.where(state_diag, e_last[:, ps] * st + upd, 0.0)
            else:
                st = jnp.concatenate([jnp.concatenate([s0_ref[s, 2 * p], zero], axis=1),
                                      jnp.concatenate([zero, s0_ref[s, 2 * p + 1]], axis=1)], axis=0)
                o_ref[s, rows(j), ps] = oi_ref[u, :, ps] + jnp.dot(qs[:, ps], st.astype(BF16),
                                                                   preferred_element_type=F32)
                upd = lax.dot_general(ks[:, ps], vb[:, ps], TN_DIMS, preferred_element_type=F32)
                new = e_cols[ps, s:s + 1] * st + upd
                sout_ref[s, 2 * p] = new[0:dk, 0:dk]
                sout_ref[s, 2 * p + 1] = new[dk:2 * dk, dk:2 * dk]

    if carried:
        @pl.when(c == pl.num_programs(1) - 1)
        def _():
            for s in range(n_seq):
                for p in range(n_pairs):
                    sout_ref[s, 2 * p] = st_ref[s, p, 0:dk, 0:dk].T
                    sout_ref[s, 2 * p + 1] = st_ref[s, p, dk:2 * dk, dk:2 * dk].T


def _hg_core(q, k, lf, v, s0, c_len, n_chunk, n_seq):
    batch, t, d = q.shape
    n_heads = d // HG_DK
    rb = c_len * n_chunk
    assert t % rb == 0 and batch % n_seq == 0 and n_heads % 2 == 0 and n_seq <= LANES
    carried = t > rb
    rows = lambda b, c: (b, c, 0)
    state = lambda b, c: (b, 0, 0, 0)
    return pl.pallas_call(
        functools.partial(_hg_core_body, c_len=c_len, n_chunk=n_chunk, n_seq=n_seq, n_heads=n_heads, carried=carried),
        grid=(batch // n_seq, t // rb),
        in_specs=[pl.BlockSpec((n_seq, rb, d), rows)] * 4 + [pl.BlockSpec((n_seq, n_heads, HG_DK, HG_DK), state)],
        out_specs=[pl.BlockSpec((n_seq, rb, d), rows), pl.BlockSpec((n_seq, n_heads, HG_DK, HG_DK), state)],
        out_shape=[jax.ShapeDtypeStruct((batch, t, d), F32),
                   jax.ShapeDtypeStruct((batch, n_heads, HG_DK, HG_DK), F32)],
        scratch_shapes=[pltpu.VMEM((n_seq if carried else 1, n_heads // 2, 2 * HG_DK, 2 * HG_DK), F32),
                        pltpu.VMEM((c_len, d), F32),
                        pltpu.VMEM((n_seq * n_chunk, c_len, d), F32)],
        compiler_params=_params("arbitrary", "arbitrary"),
        name=f"hgrn_core_{c_len}",
    )(q, k, lf, v, s0)


def _hg_out_router_body(op_ref, os_ref, gp_ref, gs_ref, x_ref, gn_ref, wo_ref, g_ref, wr_ref, br_ref,
                        x3_ref, ids_ref, wts_ref, wo_bf, *, npt, n_heads, n_exp):
    i = pl.program_id(0)

    @pl.when(i == 0)
    def _():
        wo_bf[...] = wo_ref[...].astype(BF16)

    dk = HG_DK
    o = jnp.where(i < npt, op_ref[...], os_ref[...])
    gate = jnp.where(i < npt, gp_ref[...], gs_ref[...])
    gn = gn_ref[...]
    normed = jnp.concatenate([_rms(o[:, h * dk:(h + 1) * dk], gn) for h in range(n_heads)], axis=1)
    y = (normed * gate).astype(BF16)
    x3 = x_ref[...] + jnp.dot(y, wo_bf[...], preferred_element_type=F32)
    x3_ref[...] = x3
    h4 = _rms(x3, g_ref[...])
    h_hi = h4.astype(BF16)
    h_lo = (h4 - h_hi.astype(F32)).astype(BF16)
    wr = wr_ref[...]
    w_hi = wr.astype(BF16)
    w_lo = (wr - w_hi.astype(F32)).astype(BF16)
    logits = (jnp.dot(h_hi, w_hi, preferred_element_type=F32)
              + (jnp.dot(h_lo, w_hi, preferred_element_type=F32) + jnp.dot(h_hi, w_lo, preferred_element_type=F32))
              + br_ref[...])
    lane = lax.broadcasted_iota(jnp.int32, logits.shape, 1)
    logits = jnp.where(lane < n_exp, logits, -jnp.inf)
    m1 = jnp.max(logits, axis=-1, keepdims=True)
    i1 = jnp.min(jnp.where(logits == m1, lane, LANES), axis=-1, keepdims=True)
    rest = jnp.where(lane == i1, -jnp.inf, logits)
    m2 = jnp.max(rest, axis=-1, keepdims=True)
    i2 = jnp.min(jnp.where(rest == m2, lane, LANES), axis=-1, keepdims=True)
    e2 = jnp.exp(m2 - m1)
    den = 1.0 + e2
    ids_ref[...] = jnp.where(lane == 0, i1, jnp.where(lane == 1, i2, 0))
    wts_ref[...] = jnp.where(lane == 0, 1.0 / den, jnp.where(lane == 1, e2 / den, 0.0))


def _hg_out_router(o_p, o_s, gate_p, gate_s, x, gn, wo, g, w_router, b_router, n_exp):
    n, d = x.shape
    n_p, n_s = o_p.shape[0], o_s.shape[0]
    tm = _pick_tile(math.gcd(n_p, n_s), 512)
    npt = n_p // tm
    row = lambda i: (i, 0)
    fix = lambda i: (0, 0)
    rp, rs = _split_maps(npt)
    return pl.pallas_call(
        functools.partial(_hg_out_router_body, npt=npt, n_heads=d // HG_DK, n_exp=n_exp),
        grid=(n // tm,),
        in_specs=[pl.BlockSpec((tm, d), rp), pl.BlockSpec((tm, d), rs), pl.BlockSpec((tm, d), rp),
                  pl.BlockSpec((tm, d), rs), pl.BlockSpec((tm, d), row),
                  pl.BlockSpec((1, HG_DK), fix), pl.BlockSpec(wo.shape, fix), pl.BlockSpec((1, d), fix),
                  pl.BlockSpec((d, LANES), fix), pl.BlockSpec((1, LANES), fix)],
        out_specs=[pl.BlockSpec((tm, d), row), pl.BlockSpec((tm, LANES), row), pl.BlockSpec((tm, LANES), row)],
        out_shape=[jax.ShapeDtypeStruct((n, d), F32), jax.ShapeDtypeStruct((n, LANES), jnp.int32),
                   jax.ShapeDtypeStruct((n, LANES), F32)],
        scratch_shapes=[pltpu.VMEM(wo.shape, BF16)],
        compiler_params=_params("arbitrary"),
        name="hgrn_out_router",
    )(o_p, o_s, gate_p, gate_s, x, gn, wo, g, w_router, b_router)


def _moe_body(te_ref, nv_ref, first_ref, last_ref, rin0_ref, rin_next_ref, rout_prev_ref, rout_cur_ref,
              x_hbm, g_ref, wg_ref, wu_ref, wd_ref, y_hbm,
              xg_ref, h_ref, acc_ref, wg_res, wu_res, wd_res, sem_in, sem_out, *, tm, rows_per_step):
    del te_ref
    i = pl.program_id(0)
    c = pl.program_id(1)
    slot = i % 2
    other = 1 - slot

    def row_in(buf, j, token):
        return pltpu.make_async_copy(x_hbm.at[pl.ds(token, 1)], xg_ref.at[buf, pl.ds(j, 1)], sem_in)

    def row_out(buf, j, dst):
        return pltpu.make_async_copy(acc_ref.at[buf, pl.ds(j, 1)], y_hbm.at[pl.ds(dst, 1)], sem_out)

    def all_in(buf):
        return pltpu.make_async_copy(x_hbm.at[pl.ds(0, tm)], xg_ref.at[buf], sem_in)

    def all_out(buf):
        return pltpu.make_async_copy(acc_ref.at[buf], y_hbm.at[pl.ds(0, tm)], sem_out)

    def each_row(fn):
        def body(j, carry):
            fn(j)
            return carry
        lax.fori_loop(0, tm, body, 0, unroll=8)

    @pl.when(nv_ref[i] > 0)
    def _():
        @pl.when((i == 0) & (c == 0))
        def _():
            acc_ref[1] = jnp.zeros(acc_ref.shape[1:], F32)
            each_row(lambda j: row_in(0, j, rin0_ref[0, 0, j]).start())
            all_in(0).wait()

        @pl.when(c == 0)
        def _():
            h_ref[...] = _rms(xg_ref[slot], g_ref[...]).astype(BF16)
            acc_ref[slot] = jnp.zeros(acc_ref.shape[1:], F32)

        def stream_rows():
            for u in range(rows_per_step):
                j = c * rows_per_step + u
                row_in(other, j, rin_next_ref[0, 0, j]).start()
                row_out(other, j, rout_prev_ref[0, 0, j]).start(priority=1)

        @pl.when(first_ref[i] > 0)
        def _():
            stream_rows()
            wg_res[c] = wg_ref[...].astype(BF16)
            wu_res[c] = wu_ref[...].astype(BF16)
            wd_res[c] = wd_ref[...].astype(BF16)
            _swiglu_step(h_ref[...], wg_res[c], wu_res[c], wd_res[c], acc_ref.at[slot])

        @pl.when(first_ref[i] == 0)
        def _():
            stream_rows()
            _swiglu_step(h_ref[...], wg_res[c], wu_res[c], wd_res[c], acc_ref.at[slot])

        @pl.when(c == pl.num_programs(1) - 1)
        def _():
            all_in(other).wait()
            all_out(other).wait()

            @pl.when(i == last_ref[0])
            def _():
                each_row(lambda j: row_out(slot, j, rout_cur_ref[0, 0, j]).start())
                all_out(slot).wait()


MOE_TILE_ROWS = 672
BF16_SUBLANES = 16


def _moe_experts(x, g, ids, w_gu, w_down):
    n, d = x.shape
    n_exp, d_ff = w_down.shape[0], w_down.shape[1]
    tf = _ff_chunk(d_ff)
    nc = d_ff // tf
    unit = BF16_SUBLANES * nc
    tm = unit * max(1, round(MOE_TILE_ROWS / unit))
    n_asg = n * TOP_K
    n_tiles = (n_asg + n_exp * (tm - 1) + tm - 1) // tm

    e_flat = ids.T.reshape(n_asg)
    onehot = (e_flat[:, None] == jnp.arange(n_exp, dtype=jnp.int32)[None, :]).astype(jnp.int32)
    counts = jnp.sum(onehot, axis=0)
    rank = jnp.sum((jnp.cumsum(onehot, axis=0) - onehot) * onehot, axis=1)
    tiles_per = (counts + tm - 1) // tm
    tile_end = jnp.cumsum(tiles_per)
    tile_start = tile_end - tiles_per
    slot = jnp.sum(onehot * tile_start[None, :], axis=1) * tm + rank
    asg = jnp.full((n_tiles * tm,), -1, jnp.int32).at[slot].set(jnp.arange(n_asg, dtype=jnp.int32),
                                                                 unique_indices=True)
    pad_row = n_asg + jnp.arange(n_tiles * tm, dtype=jnp.int32) % tm
    rows_in = jnp.where(asg >= 0, jnp.where(asg >= n, asg - n, asg), 0)
    rows_out = jnp.where(asg >= 0, asg, pad_row)
    rows_in_ext = jnp.concatenate([rows_in, jnp.zeros((tm,), jnp.int32)]).reshape(n_tiles + 1, 1, tm)
    rows_out_ext = jnp.concatenate([pad_row[:tm], rows_out]).reshape(n_tiles + 1, 1, tm)
    tile = jnp.arange(n_tiles, dtype=jnp.int32)
    last = tile_end[-1] - 1
    tile_c = jnp.minimum(tile, last)
    te = jnp.minimum(jnp.sum((tile_c[:, None] >= tile_end[None, :]).astype(jnp.int32), axis=1), n_exp - 1)
    te_hot = (te[:, None] == jnp.arange(n_exp, dtype=jnp.int32)[None, :]).astype(jnp.int32)
    cnt_t = jnp.sum(te_hot * counts[None, :], axis=1)
    start_t = jnp.sum(te_hot * tile_start[None, :], axis=1)
    nv = jnp.where(tile <= last, jnp.clip(cnt_t - (tile - start_t) * tm, 0, tm), 0).astype(jnp.int32)

    first = ((tile == start_t) & (nv > 0)).astype(jnp.int32)

    def chunk(i, c, nv_ref, first_ref):
        return jnp.where((nv_ref[i] > 0) & (first_ref[i] > 0), c, nc - 1)

    smem_rows = lambda index: pl.BlockSpec((1, 1, tm), index, memory_space=pltpu.SMEM)
    return pl.pallas_call(
        functools.partial(_moe_body, tm=tm, rows_per_step=tm // nc),
        grid_spec=pltpu.PrefetchScalarGridSpec(
            num_scalar_prefetch=4,
            grid=(n_tiles, nc),
            in_specs=[smem_rows(lambda i, c, te_ref, nv_ref, first_ref, last_ref: (0, 0, 0)),
                      smem_rows(lambda i, c, te_ref, nv_ref, first_ref, last_ref: (i + 1, 0, 0)),
                      smem_rows(lambda i, c, te_ref, nv_ref, first_ref, last_ref: (i, 0, 0)),
                      smem_rows(lambda i, c, te_ref, nv_ref, first_ref, last_ref: (i + 1, 0, 0)),
                      pl.BlockSpec(memory_space=pl.ANY),
                      pl.BlockSpec((1, d), lambda i, c, te_ref, nv_ref, first_ref, last_ref: (0, 0)),
                      pl.BlockSpec((None, d, tf), lambda i, c, te_ref, nv_ref, first_ref, last_ref:
                                   (te_ref[i], 0, chunk(i, c, nv_ref, first_ref))),
                      pl.BlockSpec((None, d, tf), lambda i, c, te_ref, nv_ref, first_ref, last_ref:
                                   (te_ref[i], 0, chunk(i, c, nv_ref, first_ref) + nc)),
                      pl.BlockSpec((None, tf, d), lambda i, c, te_ref, nv_ref, first_ref, last_ref:
                                   (te_ref[i], chunk(i, c, nv_ref, first_ref), 0))],
            out_specs=pl.BlockSpec(memory_space=pl.ANY),
            scratch_shapes=[pltpu.VMEM((2, tm, d), F32), pltpu.VMEM((tm, d), BF16), pltpu.VMEM((2, tm, d), F32),
                            pltpu.VMEM((nc, d, tf), BF16), pltpu.VMEM((nc, d, tf), BF16), pltpu.VMEM((nc, tf, d), BF16),
                            pltpu.SemaphoreType.DMA, pltpu.SemaphoreType.DMA]),
        out_shape=jax.ShapeDtypeStruct((n_asg + tm, d), F32),
        compiler_params=_params("arbitrary", "arbitrary"),
        name="moe_experts",
    )(te.astype(jnp.int32), nv, first, last.reshape(1).astype(jnp.int32), rows_in_ext, rows_in_ext, rows_out_ext,
      rows_out_ext, x, g, w_gu, w_gu, w_down)


def _combine_body(x_ref, y0_ref, y1_ref, wts_ref, g_ref, outp_ref, outs_ref, *, npt):
    wts = wts_ref[...]
    moe = wts[:, 0:1] * y0_ref[...] + wts[:, 1:2] * y1_ref[...]
    _store_split(pl.program_id(0) < npt, outp_ref, outs_ref, _rms(x_ref[...] + moe, g_ref[...]))


def _combine(x, y2, wts, g, n_p):
    n, d = x.shape
    n_s = n - n_p
    tm = _pick_tile(math.gcd(n_p, n_s), 512)
    nt = n // tm
    npt = n_p // tm
    row = lambda i: (i, 0)
    rp, rs = _split_maps(npt)
    return pl.pallas_call(
        functools.partial(_combine_body, npt=npt),
        grid=(nt,),
        in_specs=[pl.BlockSpec((tm, d), row), pl.BlockSpec((tm, d), row), pl.BlockSpec((tm, d), lambda i: (i + nt, 0)),
                  pl.BlockSpec((tm, LANES), row), pl.BlockSpec((1, d), lambda i: (0, 0))],
        out_specs=[pl.BlockSpec((tm, d), rp), pl.BlockSpec((tm, d), rs)],
        out_shape=[jax.ShapeDtypeStruct((n_p, d), F32), jax.ShapeDtypeStruct((n_s, d), F32)],
        compiler_params=_params("arbitrary"),
        name="moe_combine",
    )(x, y2, y2, wts, g)


def _rope_tables(pos, hd):
    half = hd // 2
    inv = ROPE_THETA ** (-jnp.arange(half, dtype=F32) / half)
    ang = pos.astype(F32)[:, None] * inv[None, :]
    cos = jnp.cos(ang)
    sin = jnp.sin(ang)
    reps = LANES // hd
    return jnp.tile(jnp.concatenate([cos, cos], axis=1), (1, reps)), jnp.tile(jnp.concatenate([-sin, sin], axis=1), (1, reps))


def kernel(x_prompt, x_sample, cache_k_win, cache_v_win, state_hgrn, norm_mix, norm_ffn, norm_final,
           w_qkv, b_qkv, w_o_attn, b_o_attn, sinks, w_in_hg, hg_lower, hg_norm, w_o_hg,
           w_gu_dense, w_down_dense, w_router, b_router, w_gu_moe, w_down_moe):
    batch, seq, d = x_prompt.shape
    n_seq, t_dec, _ = x_sample.shape
    n_kv, hd = cache_k_win.shape[3], cache_k_win.shape[4]
    n_heads = sinks.shape[1]
    g_per = n_heads // n_kv
    n_q, n_k = n_heads * hd, n_kv * hd
    n_p, n_s = batch * seq, n_seq * t_dec
    n_exp = w_router.shape[2]
    wc = cache_k_win.shape[2]
    assert norm_mix.shape[0] == 2 and hd * 2 == LANES and d % HG_DK == 0 and seq % WINDOW == 0 and wc == WINDOW

    tile_q = _pick_tile(math.gcd(n_p, n_s), 512)
    assert tile_q % t_dec == 0
    rope_p = _rope_tables(jnp.arange(seq), hd)
    rope_s = _rope_tables(jnp.tile(PAST_LEN + jnp.arange(t_dec), tile_q // t_dec), hd)
    x, q, k, v = _qkv_rope(x_prompt.reshape(n_p, d), x_sample.reshape(n_s, d), norm_mix[0:1], w_qkv[0], b_qkv[0:1],
                           rope_p, rope_s, n_q, n_k, hd)
    o = _swa_prompt(q, k, v, sinks[0], jnp.zeros((n_p + n_s, n_q), BF16), batch, seq, n_kv, g_per, hd)
    kc = cache_k_win[0].reshape(n_seq, wc, n_k)
    vc = cache_v_win[0].reshape(n_seq, wc, n_k)
    o = _swa_sample(q, k, v, kc, vc, sinks[0], o, n_p, n_seq, t_dec, n_kv, g_per, hd)
    last_win = lambda a: jnp.stack([a[(b + 1) * seq - wc:(b + 1) * seq] for b in range(batch)]).reshape(batch, wc, n_kv, hd)
    k_win_p, v_win_p = last_win(k), last_win(v)
    k_win_s = jnp.concatenate([cache_k_win[0][:, t_dec:], k[n_p:].reshape(n_seq, t_dec, n_kv, hd)], axis=1)
    v_win_s = jnp.concatenate([cache_v_win[0][:, t_dec:], v[n_p:].reshape(n_seq, t_dec, n_kv, hd)], axis=1)
    x = _oproj_ffn(o, x, w_o_attn[0], b_o_attn[0:1], norm_ffn[0:1], w_gu_dense[0], w_down_dense[0])

    lb_sm = jax.nn.softmax(hg_lower.astype(F32), axis=0)
    lb = (jnp.cumsum(lb_sm, axis=0) - lb_sm[0])[1:2]
    in_half = functools.partial(_hg_inproj_half, x, norm_mix[1:2], w_in_hg[0], lb, n_p)
    hq, hlf, hk, sq, slf, sk = in_half(0)
    hv, hgate, sv, sgate = in_half(1)
    n_hh = d // HG_DK
    c_p = math.gcd(seq, HG_CHUNK)
    as_seq = lambda a: a.reshape(batch, seq, d)
    o_p, s_p = _hg_core(as_seq(hq), as_seq(hk), as_seq(hlf), as_seq(hv), jnp.zeros((batch, n_hh, HG_DK, HG_DK), F32),
                        c_p, 2 if seq % (2 * c_p) == 0 else 1, 2 if batch % 2 == 0 else 1)
    c_s = SUBLANES * ((t_dec + SUBLANES - 1) // SUBLANES)
    pad = lambda a: jnp.pad(a.reshape(n_seq, t_dec, d), ((0, 0), (0, c_s - t_dec), (0, 0)))
    o_s, s_s = _hg_core(pad(sq), pad(sk), pad(slf), pad(sv), state_hgrn[0], c_s, 1, 8 if n_seq % 8 == 0 else 1)
    o_s = o_s[:, :t_dec].reshape(n_s, d)

    wr = jnp.pad(w_router[0], ((0, 0), (0, LANES - n_exp)))
    br = jnp.pad(b_router[0:1], ((0, 0), (0, LANES - n_exp)))
    x3, ids, wts = _hg_out_router(o_p.reshape(n_p, d), o_s, hgate, sgate, x, hg_norm[0:1], w_o_hg[0], norm_ffn[1:2],
                                  wr, br, n_exp)
    y2 = _moe_experts(x3, norm_ffn[1:2], ids[:, :TOP_K], w_gu_moe[0], w_down_moe[0])
    y_p, y_s = _combine(x3, y2, wts, norm_final.reshape(1, d), n_p)

    return (y_p.reshape(batch, seq, d), y_s.reshape(n_seq, t_dec, d),
            k_win_p[None], v_win_p[None], k_win_s[None], v_win_s[None], s_p[None], s_s[None])
```

```python
import functools
import math

import jax
import jax.numpy as jnp
from jax import lax
from jax.experimental import pallas as pl
from jax.experimental.pallas import tpu as pltpu

F32 = jnp.float32
BF16 = jnp.bfloat16

NORM_EPS = 1e-5
WINDOW = 128
PAST_LEN = 16384
ROPE_THETA = 10000.0
HG_DK = 128
HG_CHUNK = 64
HG_SAFE_DECAY = 60.0
HG_SAFE_Q = 1e9
TOP_K = 2
LANES = 128
SUBLANES = 8
VMEM_LIMIT = 56 * 1024 * 1024

NT_DIMS = (((1,), (1,)), ((), ()))
TN_DIMS = (((0,), (0,)), ((), ()))


def _pick_tile(n, target):
    for t in (1536, 1024, 768, 512, 384, 256, 192, 128, 64, 32, 16, 8):
        if t <= target and n % t == 0:
            return t
    raise ValueError(f"no row tile for {n}")


def _params(*sem):
    return pltpu.CompilerParams(dimension_semantics=sem, vmem_limit_bytes=VMEM_LIMIT)


def _rms(x, g):
    return x * lax.rsqrt(jnp.mean(x * x, axis=-1, keepdims=True) + NORM_EPS) * g


def _split_maps(npt):
    return (lambda i, *_: (jnp.minimum(i, npt - 1), 0)), (lambda i, *_: (jnp.maximum(i - npt, 0), 0))


def _store_split(is_prompt, ref_p, ref_s, val):
    @pl.when(is_prompt)
    def _():
        ref_p[...] = val

    @pl.when(jnp.logical_not(is_prompt))
    def _():
        ref_s[...] = val


def _qkv_body(xp_ref, xs_ref, g_ref, w_ref, b_ref, cosp_ref, sinp_ref, coss_ref, sins_ref,
              x_ref, q_ref, k_ref, v_ref, wbf_ref,
              *, npt, n_q, n_k, hd):
    i = pl.program_id(0)

    @pl.when(i == 0)
    def _():
        wbf_ref[...] = w_ref[...].astype(BF16)

    x = jnp.where(i < npt, xp_ref[...], xs_ref[...])
    x_ref[...] = x
    h = _rms(x, g_ref[...]).astype(BF16)
    y = jnp.dot(h, wbf_ref[...], preferred_element_type=F32) + b_ref[...]
    cos = jnp.where(i < npt, cosp_ref[...], coss_ref[...])
    sin = jnp.where(i < npt, sinp_ref[...], sins_ref[...])
    lane = lax.broadcasted_iota(jnp.int32, cos.shape, 1)
    first = (lane % hd) < (hd // 2)

    def rope(blk):
        partner = jnp.where(first, pltpu.roll(blk, LANES - hd // 2, 1), pltpu.roll(blk, hd // 2, 1))
        return blk * cos + partner * sin

    scale = hd ** -0.5
    for j in range(n_q // LANES):
        q_ref[:, j * LANES:(j + 1) * LANES] = (rope(y[:, j * LANES:(j + 1) * LANES]) * scale).astype(BF16)
    for j in range(n_k // LANES):
        k_ref[:, j * LANES:(j + 1) * LANES] = rope(y[:, n_q + j * LANES:n_q + (j + 1) * LANES])
    v_ref[...] = y[:, n_q + n_k:]


def _qkv_rope(xp, xs, g, w, b, rope_p, rope_s, n_q, n_k, hd):
    (n_p, d), n_s = xp.shape, xs.shape[0]
    n = n_p + n_s
    n_out = w.shape[1]
    tm = _pick_tile(math.gcd(n_p, n_s), 512)
    npt = n_p // tm
    row = lambda i: (i, 0)
    fix = lambda i: (0, 0)
    rp, rs = _split_maps(npt)
    seq_tiles = rope_p[0].shape[0] // tm
    assert rope_p[0].shape[0] % tm == 0 and rope_s[0].shape[0] == tm
    pos_p = lambda i: (jnp.minimum(i, npt - 1) % seq_tiles, 0)
    return pl.pallas_call(
        functools.partial(_qkv_body, npt=npt, n_q=n_q, n_k=n_k, hd=hd),
        grid=(n // tm,),
        in_specs=[pl.BlockSpec((tm, d), rp), pl.BlockSpec((tm, d), rs), pl.BlockSpec((1, d), fix),
                  pl.BlockSpec((d, n_out), fix), pl.BlockSpec((1, n_out), fix),
                  pl.BlockSpec((tm, LANES), pos_p), pl.BlockSpec((tm, LANES), pos_p),
                  pl.BlockSpec((tm, LANES), fix), pl.BlockSpec((tm, LANES), fix)],
        out_specs=[pl.BlockSpec((tm, d), row), pl.BlockSpec((tm, n_q), row), pl.BlockSpec((tm, n_k), row),
                   pl.BlockSpec((tm, n_k), row)],
        out_shape=[jax.ShapeDtypeStruct((n, d), F32), jax.ShapeDtypeStruct((n, n_q), BF16),
                   jax.ShapeDtypeStruct((n, n_k), F32), jax.ShapeDtypeStruct((n, n_k), F32)],
        scratch_shapes=[pltpu.VMEM((d, n_out), BF16)],
        compiler_params=_params("arbitrary"),
        name="qkv_rope",
    )(xp, xs, g, w, b, *rope_p, *rope_s)


def _sink_column(sink_ref, kh, g_per, rows_per):
    blk = lax.broadcasted_iota(jnp.int32, (g_per * rows_per, 1), 0) // rows_per
    col = jnp.full((g_per * rows_per, 1), sink_ref[kh * g_per], F32)
    for g in range(1, g_per):
        col = jnp.where(blk == g, sink_ref[kh * g_per + g], col)
    return col


def _stack_heads(q, kh, g_per, hd):
    return jnp.concatenate([q[:, (kh * g_per + g) * hd:(kh * g_per + g + 1) * hd] for g in range(g_per)], axis=0)


def _swa_prompt_body(sink_ref, q_ref, kp_ref, kc_ref, vp_ref, vc_ref, o_all_ref, o_ref, *, n_kv, g_per, hd, w):
    del o_all_ref
    n = pl.program_id(1)
    q = q_ref[...]
    kk = jnp.concatenate([kp_ref[...], kc_ref[...]], axis=0).astype(BF16)
    vv = jnp.concatenate([vp_ref[...], vc_ref[...]], axis=0).astype(BF16)
    i = lax.broadcasted_iota(jnp.int32, (g_per * w, 2 * w), 0) % w
    j = lax.broadcasted_iota(jnp.int32, (g_per * w, 2 * w), 1)
    mask = (j > i) & (j <= i + w) & ((j >= w) | (n > 0))
    outs = []
    for kh in range(n_kv):
        q4 = _stack_heads(q, kh, g_per, hd)
        s = lax.dot_general(q4, kk[:, kh * hd:(kh + 1) * hd], NT_DIMS, preferred_element_type=F32)
        s = jnp.where(mask, s, -jnp.inf)
        sink = _sink_column(sink_ref, kh, g_per, w)
        m = jnp.maximum(jnp.max(s, axis=-1, keepdims=True), sink)
        p = jnp.exp(s - m)
        p = p / (jnp.sum(p, axis=-1, keepdims=True) + jnp.exp(sink - m))
        o4 = jnp.dot(p.astype(BF16), vv[:, kh * hd:(kh + 1) * hd], preferred_element_type=F32)
        outs.append(jnp.concatenate([o4[g * w:(g + 1) * w] for g in range(g_per)], axis=1))
    o_ref[...] = jnp.concatenate(outs, axis=1).astype(BF16)


def _swa_prompt(q, k, v, sinks, o_all, batch, seq, n_kv, g_per, hd):
    w = WINDOW
    nb = seq // w
    dq = q.shape[1]
    dkv = k.shape[1]
    cur = lambda b, n: (b * nb + n, 0)
    prev = lambda b, n: (b * nb + jnp.maximum(n - 1, 0), 0)
    return pl.pallas_call(
        functools.partial(_swa_prompt_body, n_kv=n_kv, g_per=g_per, hd=hd, w=w),
        grid=(batch, nb),
        in_specs=[pl.BlockSpec(memory_space=pltpu.SMEM),
                  pl.BlockSpec((w, dq), cur), pl.BlockSpec((w, dkv), prev), pl.BlockSpec((w, dkv), cur),
                  pl.BlockSpec((w, dkv), prev), pl.BlockSpec((w, dkv), cur),
                  pl.BlockSpec(memory_space=pl.ANY)],
        out_specs=pl.BlockSpec((w, dq), cur),
        out_shape=jax.ShapeDtypeStruct(o_all.shape, o_all.dtype),
        input_output_aliases={6: 0},
        compiler_params=_params("arbitrary", "arbitrary"),
        name="swa_prompt",
    )(sinks, q, k, k, v, v, o_all)


def _swa_sample_body(sink_ref, q_ref, kn_ref, vn_ref, kc_ref, vc_ref, o_all_ref, o_ref, *, n_kv, g_per, hd, bt, t, wc):
    del o_all_ref
    r = bt * t
    q = q_ref[...]
    kn = kn_ref[...].astype(BF16)
    vn = vn_ref[...].astype(BF16)
    kc = kc_ref[...].reshape(bt * wc, n_kv * hd).astype(BF16)
    vc = vc_ref[...].reshape(bt * wc, n_kv * hd).astype(BF16)
    row_c = lax.broadcasted_iota(jnp.int32, (g_per * r, bt * wc), 0) % r
    col_c = lax.broadcasted_iota(jnp.int32, (g_per * r, bt * wc), 1)
    mask_c = (col_c // wc == row_c // t) & (col_c % wc > row_c % t + (wc - WINDOW))
    row_n = lax.broadcasted_iota(jnp.int32, (g_per * r, r), 0) % r
    col_n = lax.broadcasted_iota(jnp.int32, (g_per * r, r), 1)
    mask_n = (col_n // t == row_n // t) & (col_n % t <= row_n % t)
    outs = []
    for kh in range(n_kv):
        hs = slice(kh * hd, (kh + 1) * hd)
        q4 = _stack_heads(q, kh, g_per, hd)
        sc = jnp.where(mask_c, lax.dot_general(q4, kc[:, hs], NT_DIMS, preferred_element_type=F32), -jnp.inf)
        sn = jnp.where(mask_n, lax.dot_general(q4, kn[:, hs], NT_DIMS, preferred_element_type=F32), -jnp.inf)
        sink = _sink_column(sink_ref, kh, g_per, r)
        m = jnp.maximum(jnp.maximum(jnp.max(sc, axis=-1, keepdims=True), jnp.max(sn, axis=-1, keepdims=True)), sink)
        pc = jnp.exp(sc - m)
        pn = jnp.exp(sn - m)
        den = jnp.sum(pc, axis=-1, keepdims=True) + jnp.sum(pn, axis=-1, keepdims=True) + jnp.exp(sink - m)
        o4 = (jnp.dot((pc / den).astype(BF16), vc[:, hs], preferred_element_type=F32)
              + jnp.dot((pn / den).astype(BF16), vn[:, hs], preferred_element_type=F32))
        outs.append(jnp.concatenate([o4[g * r:(g + 1) * r] for g in range(g_per)], axis=1))
    o_ref[...] = jnp.concatenate(outs, axis=1).astype(BF16)


def _swa_sample(q, k, v, k_cache, v_cache, sinks, o_all, row0, n_seq, t, n_kv, g_per, hd):
    wc = k_cache.shape[1]
    dq = q.shape[1]
    dkv = k.shape[1]
    bt = 8 if n_seq % 8 == 0 else n_seq
    r = bt * t
    assert row0 % r == 0
    off = row0 // r
    rows = lambda i: (off + i, 0)
    return pl.pallas_call(
        functools.partial(_swa_sample_body, n_kv=n_kv, g_per=g_per, hd=hd, bt=bt, t=t, wc=wc),
        grid=(n_seq // bt,),
        in_specs=[pl.BlockSpec(memory_space=pltpu.SMEM),
                  pl.BlockSpec((r, dq), rows), pl.BlockSpec((r, dkv), rows), pl.BlockSpec((r, dkv), rows),
                  pl.BlockSpec((bt, wc, dkv), lambda i: (i, 0, 0)), pl.BlockSpec((bt, wc, dkv), lambda i: (i, 0, 0)),
                  pl.BlockSpec(memory_space=pl.ANY)],
        out_specs=pl.BlockSpec((r, dq), rows),
        out_shape=jax.ShapeDtypeStruct(o_all.shape, o_all.dtype),
        input_output_aliases={6: 0},
        compiler_params=_params("arbitrary"),
        name="swa_sample",
    )(sinks, q, k, v, k_cache, v_cache, o_all)


def _swiglu_step(h, wg, wu, wd, acc_ref):
    a = jnp.dot(h, wg, preferred_element_type=F32)
    b = jnp.dot(h, wu, preferred_element_type=F32)
    act = (a * jax.nn.sigmoid(a) * b).astype(BF16)
    acc_ref[...] += jnp.dot(act, wd, preferred_element_type=F32)


def _oproj_ffn_body(o_ref, x_ref, wo_ref, bo_ref, g_ref, wg_ref, wu_ref, wd_ref, out_ref,
                    wo_bf, x1_ref, h_ref, acc_ref):
    i = pl.program_id(0)
    c = pl.program_id(1)

    @pl.when((i == 0) & (c == 0))
    def _():
        wo_bf[...] = wo_ref[...].astype(BF16)

    @pl.when(c == 0)
    def _():
        x1 = x_ref[...] + jnp.dot(o_ref[...], wo_bf[...], preferred_element_type=F32) + bo_ref[...]
        x1_ref[...] = x1
        h_ref[...] = _rms(x1, g_ref[...]).astype(BF16)
        acc_ref[...] = jnp.zeros_like(acc_ref)

    _swiglu_step(h_ref[...], wg_ref[...].astype(BF16), wu_ref[...].astype(BF16), wd_ref[...].astype(BF16), acc_ref)

    @pl.when(c == pl.num_programs(1) - 1)
    def _():
        out_ref[...] = x1_ref[...] + acc_ref[...]


def _ff_chunk(d_ff):
    for tf in (512, 256, 128):
        if d_ff % tf == 0:
            return tf
    raise ValueError(f"d_ff {d_ff} is not a multiple of {LANES}")


def _oproj_ffn(o, x, wo, bo, g, w_gu, w_down):
    n, d = x.shape
    d_ff = w_down.shape[0]
    tf = _ff_chunk(d_ff)
    nc = d_ff // tf
    tm = _pick_tile(n, 768)
    row = lambda i, c: (i, 0)
    fix = lambda i, c: (0, 0)
    return pl.pallas_call(
        _oproj_ffn_body,
        grid=(n // tm, nc),
        in_specs=[pl.BlockSpec((tm, o.shape[1]), row), pl.BlockSpec((tm, d), row),
                  pl.BlockSpec(wo.shape, fix), pl.BlockSpec((1, d), fix), pl.BlockSpec((1, d), fix),
                  pl.BlockSpec((d, tf), lambda i, c: (0, c)), pl.BlockSpec((d, tf), lambda i, c: (0, c + nc)),
                  pl.BlockSpec((tf, d), lambda i, c: (c, 0))],
        out_specs=pl.BlockSpec((tm, d), row),
        out_shape=jax.ShapeDtypeStruct((n, d), F32),
        scratch_shapes=[pltpu.VMEM(wo.shape, BF16), pltpu.VMEM((tm, d), F32), pltpu.VMEM((tm, d), BF16),
                        pltpu.VMEM((tm, d), F32)],
        compiler_params=_params("arbitrary", "arbitrary"),
        name="oproj_ffn",
    )(o, x, wo, bo, g, w_gu, w_gu, w_down)


def _hg_in_body(x_ref, g_ref, w_ref, lb_ref, *refs, half):
    out_refs, wbf_ref = refs[:-1], refs[-1]
    d = x_ref.shape[1]

    @pl.when(pl.program_id(0) == 0)
    def _():
        wbf_ref[...] = w_ref[...].astype(BF16)

    h = _rms(x_ref[...], g_ref[...]).astype(BF16)
    z = jnp.dot(h, wbf_ref[...], preferred_element_type=F32)
    z0, z1 = z[:, :d], z[:, d:]
    if half == 0:
        out_refs[0][...] = z0 * jax.nn.sigmoid(z0) * (HG_DK ** -0.5)
        lb = lb_ref[...]
        t = jnp.exp(-jnp.abs(z1))
        log_sig = jnp.minimum(z1, 0.0) - jnp.log(1.0 + t)
        a = jnp.log(lb)
        b = jnp.log1p(-lb) + log_sig
        out_refs[1][...] = jnp.maximum(a, b) + jnp.log(1.0 + jnp.exp(-jnp.abs(a - b)))
        out_refs[2][...] = (1.0 - lb) * (jnp.where(z1 >= 0.0, t, 1.0) / (1.0 + t))
    else:
        out_refs[0][...] = z0
        out_refs[1][...] = z1 * jax.nn.sigmoid(z1)


def _hg_inproj_half(x, g, w_in, lb, half):
    n, d = x.shape
    n_res = 3 if half == 0 else 2
    tm = _pick_tile(n, 512)
    row = lambda i: (i, 0)
    fix = lambda i: (0, 0)
    return pl.pallas_call(
        functools.partial(_hg_in_body, half=half),
        grid=(n // tm,),
        in_specs=[pl.BlockSpec((tm, d), row), pl.BlockSpec((1, d), fix), pl.BlockSpec((d, 2 * d), lambda i: (0, half)),
                  pl.BlockSpec((1, d), fix)],
        out_specs=[pl.BlockSpec((tm, d), row)] * n_res,
        out_shape=[jax.ShapeDtypeStruct((n, d), F32)] * n_res,
        scratch_shapes=[pltpu.VMEM((d, 2 * d), BF16)],
        compiler_params=_params("arbitrary"),
        name=f"hgrn_inproj_{half}",
    )(x, g, w_in, lb)


def _hg_core_body(*refs, c_len, n_chunk, n_seq, n_heads, carried, per_seq_inputs):
    n_in = n_seq if per_seq_inputs else 1
    in_refs, (s0_ref, o_ref, sout_ref, st_ref, g_ref, oi_ref) = refs[:4 * n_in], refs[4 * n_in:]

    def rd(kind, s, row_slice):
        if per_seq_inputs:
            return in_refs[kind * n_in + s][row_slice, :]
        return in_refs[kind][s, row_slice, :]

    c = pl.program_id(1)
    dk = HG_DK
    rb = c_len * n_chunk
    units = [(s, j) for s in range(n_seq) for j in range(n_chunk)]

    n_pairs = n_heads // 2
    state_diag = (lax.broadcasted_iota(jnp.int32, (2 * dk, 2 * dk), 0) // dk
                  == lax.broadcasted_iota(jnp.int32, (2 * dk, 2 * dk), 1) // dk)
    rows_diag = (lax.broadcasted_iota(jnp.int32, (2 * c_len, 2 * dk), 0) // c_len
                 == lax.broadcasted_iota(jnp.int32, (2 * c_len, 2 * dk), 1) // dk)

    def pair_tile(x):
        x2 = jnp.concatenate([x, x], axis=0)
        return jnp.where(rows_diag, x2, jnp.zeros_like(x2))

    if carried:
        @pl.when(c == 0)
        def _():
            for s in range(n_seq):
                for p in range(n_pairs):
                    st_ref[s, p] = jnp.zeros((2 * dk, 2 * dk), F32)
                    st_ref[s, p, 0:dk, 0:dk] = s0_ref[s, 2 * p].T
                    st_ref[s, p, dk:2 * dk, dk:2 * dk] = s0_ref[s, 2 * p + 1].T

    r = lax.broadcasted_iota(jnp.int32, (rb, rb), 0)
    cidx = lax.broadcasted_iota(jnp.int32, (rb, rb), 1)
    block_causal = ((r >= cidx) & (r // c_len == cidx // c_len)).astype(BF16)
    causal2 = (lax.broadcasted_iota(jnp.int32, (c_len, 2 * c_len), 0)
               >= lax.broadcasted_iota(jnp.int32, (c_len, 2 * c_len), 1) % c_len)

    def rows(j):
        return slice(j * c_len, (j + 1) * c_len)

    gcum, safe = [], None
    for s in range(n_seq):
        lf = rd(2, s, slice(None))
        lf_hi = lf.astype(BF16)
        rest = lf - lf_hi.astype(F32)
        lf_mid = rest.astype(BF16)
        lf_lo = (rest - lf_mid.astype(F32)).astype(BF16)
        gs = ((jnp.dot(block_causal, lf_lo, preferred_element_type=F32)
               + jnp.dot(block_causal, lf_mid, preferred_element_type=F32))
              + jnp.dot(block_causal, lf_hi, preferred_element_type=F32))
        gcum.append(gs)
        for j in range(n_chunk):
            ok = ((jnp.max(-gs[(j + 1) * c_len - 1:(j + 1) * c_len, :]) <= HG_SAFE_DECAY)
                  & (jnp.max(jnp.abs(rd(0, s, rows(j)))) <= HG_SAFE_Q))
            safe = ok if safe is None else (safe & ok)

    def g_mid(s, j):
        return gcum[s][j * c_len + c_len // 2 - 1:j * c_len + c_len // 2, :]

    def g_last(s, j):
        return gcum[s][(j + 1) * c_len - 1:(j + 1) * c_len, :]

    @pl.when(safe)
    def _():
        for u, (s, j) in enumerate(units):
            g = gcum[s][rows(j), :]
            qi = (rd(0, s, rows(j)) * jnp.exp(g - g_mid(s, j))).astype(BF16)
            ki = (rd(1, s, rows(j)) * jnp.exp(g_mid(s, j) - g)).astype(BF16)
            vb = rd(3, s, rows(j)).astype(BF16)
            for p in range(n_pairs):
                ps = slice(2 * p * dk, 2 * (p + 1) * dk)
                a = lax.dot_general(qi[:, ps], pair_tile(ki[:, ps]), NT_DIMS, preferred_element_type=F32)
                a = jnp.where(causal2, a, 0.0).astype(BF16)
                oi_ref[u, :, ps] = jnp.dot(a, pair_tile(vb[:, ps]), preferred_element_type=F32)

    @pl.when(jnp.logical_not(safe))
    def _():
        lane_h = lax.broadcasted_iota(jnp.int32, (n_heads * dk, n_heads * dk), 0) // dk
        lane_w = lax.broadcasted_iota(jnp.int32, (n_heads * dk, n_heads * dk), 1) // dk
        head_sum = (lane_h == lane_w).astype(BF16)
        t_idx = lax.broadcasted_iota(jnp.int32, (c_len, 1), 0)
        for u, (s, j) in enumerate(units):
            g = gcum[s][rows(j), :]
            g_ref[...] = g
            q = rd(0, s, rows(j))

            def key_row(i, acc, s=s, j=j, g=g, q=q):
                gi = g_ref[pl.ds(i, 1), :]
                decay = jnp.exp(jnp.where(t_idx >= i, g - gi, -jnp.inf))
                term = (q * decay * rd(1, s, pl.ds(j * c_len + i, 1))).astype(BF16)
                a_i = jnp.dot(term, head_sum, preferred_element_type=F32)
                return acc + a_i * rd(3, s, pl.ds(j * c_len + i, 1))

            oi_ref[u] = lax.fori_loop(0, c_len, key_row, jnp.zeros((c_len, n_heads * dk), F32))

    if not carried:
        e_rows = [jnp.exp(g_last(s, 0)) for s in range(n_seq)]
        e_cols = jnp.concatenate(e_rows + [jnp.zeros((LANES - n_seq, n_heads * dk), F32)], axis=0).T
        zero = jnp.zeros((dk, dk), F32)

    for u, (s, j) in enumerate(units):
        g = gcum[s][rows(j), :]
        qs = (rd(0, s, rows(j)) * jnp.exp(g)).astype(BF16)
        ks = (rd(1, s, rows(j)) * jnp.exp(g_last(s, j) - g)).astype(BF16)
        vb = rd(3, s, rows(j)).astype(BF16)
        e_last = jnp.exp(g_last(s, j))
        for p in range(n_pairs):
            ps = slice(2 * p * dk, 2 * (p + 1) * dk)
            if carried:
                st = st_ref[s, p]
                o_ref[s, rows(j), ps] = oi_ref[u, :, ps] + lax.dot_general(qs[:, ps], st.astype(BF16), NT_DIMS,
                                                                           preferred_element_type=F32)
                upd = lax.dot_general(vb[:, ps], ks[:, ps], TN_DIMS, preferred_element_type=F32)
                st_ref[s, p] = jnp.where(state_diag, e_last[:, ps] * st + upd, 0.0)
            else:
                st = jnp.concatenate([jnp.concatenate([s0_ref[s, 2 * p], zero], axis=1),
                                      jnp.concatenate([zero, s0_ref[s, 2 * p + 1]], axis=1)], axis=0)
                o_ref[s, rows(j), ps] = oi_ref[u, :, ps] + jnp.dot(qs[:, ps], st.astype(BF16),
                                                                   preferred_element_type=F32)
                upd = lax.dot_general(ks[:, ps], vb[:, ps], TN_DIMS, preferred_element_type=F32)
                new = e_cols[ps, s:s + 1] * st + upd
                sout_ref[s, 2 * p] = new[0:dk, 0:dk]
                sout_ref[s, 2 * p + 1] = new[dk:2 * dk, dk:2 * dk]

    if carried:
        @pl.when(c == pl.num_programs(1) - 1)
        def _():
            for s in range(n_seq):
                for p in range(n_pairs):
                    sout_ref[s, 2 * p] = st_ref[s, p, 0:dk, 0:dk].T
                    sout_ref[s, 2 * p + 1] = st_ref[s, p, dk:2 * dk, dk:2 * dk].T


def _hg_core(q, k, lf, v, s0, c_len, n_chunk, n_seq, t=None):
    batch = s0.shape[0]
    per_seq_inputs = q.ndim == 2
    t = t if per_seq_inputs else q.shape[1]
    d = q.shape[-1]
    n_heads = d // HG_DK
    rb = c_len * n_chunk
    nc = t // rb
    assert t % rb == 0 and batch % n_seq == 0 and n_heads % 2 == 0 and n_seq <= LANES
    carried = t > rb
    rows = lambda b, c: (b, c, 0)
    state = lambda b, c: (b, 0, 0, 0)
    if per_seq_inputs:
        seq_rows = lambda s: (lambda b, c: ((b * n_seq + s) * nc + c, 0))
        in_specs = [pl.BlockSpec((rb, d), seq_rows(s)) for _ in range(4) for s in range(n_seq)]
        inputs = [a for a in (q, k, lf, v) for _ in range(n_seq)]
    else:
        in_specs = [pl.BlockSpec((n_seq, rb, d), rows)] * 4
        inputs = [q, k, lf, v]
    return pl.pallas_call(
        functools.partial(_hg_core_body, c_len=c_len, n_chunk=n_chunk, n_seq=n_seq, n_heads=n_heads, carried=carried,
                          per_seq_inputs=per_seq_inputs),
        grid=(batch // n_seq, nc),
        in_specs=in_specs + [pl.BlockSpec((n_seq, n_heads, HG_DK, HG_DK), state)],
        out_specs=[pl.BlockSpec((n_seq, rb, d), rows), pl.BlockSpec((n_seq, n_heads, HG_DK, HG_DK), state)],
        out_shape=[jax.ShapeDtypeStruct((batch, t, d), F32),
                   jax.ShapeDtypeStruct((batch, n_heads, HG_DK, HG_DK), F32)],
        scratch_shapes=[pltpu.VMEM((n_seq if carried else 1, n_heads // 2, 2 * HG_DK, 2 * HG_DK), F32),
                        pltpu.VMEM((c_len, d), F32),
                        pltpu.VMEM((n_seq * n_chunk, c_len, d), F32)],
        compiler_params=_params("arbitrary", "arbitrary"),
        name=f"hgrn_core_{c_len}",
    )(*inputs, s0)


def _hg_out_router_body(op_ref, os_ref, gate_ref, x_ref, gn_ref, wo_ref, g_ref, wr_ref, br_ref,
                        x3_ref, ids_ref, wts_ref, wo_bf, *, npt, n_heads, n_exp):
    i = pl.program_id(0)

    @pl.when(i == 0)
    def _():
        wo_bf[...] = wo_ref[...].astype(BF16)

    dk = HG_DK
    o = jnp.where(i < npt, op_ref[...], os_ref[...])
    gn = gn_ref[...]
    normed = jnp.concatenate([_rms(o[:, h * dk:(h + 1) * dk], gn) for h in range(n_heads)], axis=1)
    y = (normed * gate_ref[...]).astype(BF16)
    x3 = x_ref[...] + jnp.dot(y, wo_bf[...], preferred_element_type=F32)
    x3_ref[...] = x3
    h4 = _rms(x3, g_ref[...])
    h_hi = h4.astype(BF16)
    h_lo = (h4 - h_hi.astype(F32)).astype(BF16)
    wr = wr_ref[...]
    w_hi = wr.astype(BF16)
    w_lo = (wr - w_hi.astype(F32)).astype(BF16)
    logits = (jnp.dot(h_hi, w_hi, preferred_element_type=F32)
              + (jnp.dot(h_lo, w_hi, preferred_element_type=F32) + jnp.dot(h_hi, w_lo, preferred_element_type=F32))
              + br_ref[...])
    lane = lax.broadcasted_iota(jnp.int32, logits.shape, 1)
    logits = jnp.where(lane < n_exp, logits, -jnp.inf)
    m1 = jnp.max(logits, axis=-1, keepdims=True)
    i1 = jnp.min(jnp.where(logits == m1, lane, LANES), axis=-1, keepdims=True)
    rest = jnp.where(lane == i1, -jnp.inf, logits)
    m2 = jnp.max(rest, axis=-1, keepdims=True)
    i2 = jnp.min(jnp.where(rest == m2, lane, LANES), axis=-1, keepdims=True)
    e2 = jnp.exp(m2 - m1)
    den = 1.0 + e2
    ids_ref[...] = jnp.where(lane == 0, i1, jnp.where(lane == 1, i2, 0))
    wts_ref[...] = jnp.where(lane == 0, 1.0 / den, jnp.where(lane == 1, e2 / den, 0.0))


def _hg_out_router(o_p, o_s, gate, x, gn, wo, g, w_router, b_router, n_exp):
    n, d = x.shape
    n_p, n_s = o_p.shape[0], o_s.shape[0]
    tm = _pick_tile(math.gcd(n_p, n_s), 512)
    npt = n_p // tm
    row = lambda i: (i, 0)
    fix = lambda i: (0, 0)
    rp, rs = _split_maps(npt)
    return pl.pallas_call(
        functools.partial(_hg_out_router_body, npt=npt, n_heads=d // HG_DK, n_exp=n_exp),
        grid=(n // tm,),
        in_specs=[pl.BlockSpec((tm, d), rp), pl.BlockSpec((tm, d), rs), pl.BlockSpec((tm, d), row),
                  pl.BlockSpec((tm, d), row),
                  pl.BlockSpec((1, HG_DK), fix), pl.BlockSpec(wo.shape, fix), pl.BlockSpec((1, d), fix),
                  pl.BlockSpec((d, LANES), fix), pl.BlockSpec((1, LANES), fix)],
        out_specs=[pl.BlockSpec((tm, d), row), pl.BlockSpec((tm, LANES), row), pl.BlockSpec((tm, LANES), row)],
        out_shape=[jax.ShapeDtypeStruct((n, d), F32), jax.ShapeDtypeStruct((n, LANES), jnp.int32),
                   jax.ShapeDtypeStruct((n, LANES), F32)],
        scratch_shapes=[pltpu.VMEM(wo.shape, BF16)],
        compiler_params=_params("arbitrary"),
        name="hgrn_out_router",
    )(o_p, o_s, gate, x, gn, wo, g, w_router, b_router)


def _moe_body(te_ref, nv_ref, first_ref, last_ref, rin0_ref, rin_next_ref, rout_prev_ref, rout_cur_ref,
              x_hbm, g_ref, wg_ref, wu_ref, wd_ref, y_hbm,
              xg_ref, h_ref, acc_ref, wg_res, wu_res, wd_res, sem_in, sem_out, *, tm, rows_per_step):
    del te_ref
    i = pl.program_id(0)
    c = pl.program_id(1)
    slot = i % 2
    other = 1 - slot

    def row_in(buf, j, token):
        return pltpu.make_async_copy(x_hbm.at[pl.ds(token, 1)], xg_ref.at[buf, pl.ds(j, 1)], sem_in)

    def row_out(buf, j, dst):
        return pltpu.make_async_copy(acc_ref.at[buf, pl.ds(j, 1)], y_hbm.at[pl.ds(dst, 1)], sem_out)

    def all_in(buf):
        return pltpu.make_async_copy(x_hbm.at[pl.ds(0, tm)], xg_ref.at[buf], sem_in)

    def all_out(buf):
        return pltpu.make_async_copy(acc_ref.at[buf], y_hbm.at[pl.ds(0, tm)], sem_out)

    def each_row(fn):
        def body(j, carry):
            fn(j)
            return carry
        lax.fori_loop(0, tm, body, 0, unroll=8)

    @pl.when(nv_ref[i] > 0)
    def _():
        @pl.when((i == 0) & (c == 0))
        def _():
            acc_ref[1] = jnp.zeros(acc_ref.shape[1:], F32)
            each_row(lambda j: row_in(0, j, rin0_ref[0, 0, j]).start())
            all_in(0).wait()

        @pl.when(c == 0)
        def _():
            h_ref[...] = _rms(xg_ref[slot], g_ref[...]).astype(BF16)
            acc_ref[slot] = jnp.zeros(acc_ref.shape[1:], F32)

        def stream_rows():
            for u in range(rows_per_step):
                j = c * rows_per_step + u
                row_in(other, j, rin_next_ref[0, 0, j]).start()
                row_out(other, j, rout_prev_ref[0, 0, j]).start(priority=1)

        @pl.when(first_ref[i] > 0)
        def _():
            stream_rows()
            wg_res[c] = wg_ref[...].astype(BF16)
            wu_res[c] = wu_ref[...].astype(BF16)
            wd_res[c] = wd_ref[...].astype(BF16)
            _swiglu_step(h_ref[...], wg_res[c], wu_res[c], wd_res[c], acc_ref.at[slot])

        @pl.when(first_ref[i] == 0)
        def _():
            stream_rows()
            _swiglu_step(h_ref[...], wg_res[c], wu_res[c], wd_res[c], acc_ref.at[slot])

        @pl.when(c == pl.num_programs(1) - 1)
        def _():
            all_in(other).wait()
            all_out(other).wait()

            @pl.when(i == last_ref[0])
            def _():
                each_row(lambda j: row_out(slot, j, rout_cur_ref[0, 0, j]).start())
                all_out(slot).wait()


MOE_TILE_ROWS = 672
BF16_SUBLANES = 16


def _moe_experts(x, g, ids, w_gu, w_down):
    n, d = x.shape
    n_exp, d_ff = w_down.shape[0], w_down.shape[1]
    tf = _ff_chunk(d_ff)
    nc = d_ff // tf
    unit = BF16_SUBLANES * nc
    tm = unit * max(1, round(MOE_TILE_ROWS / unit))
    n_asg = n * TOP_K
    n_tiles = (n_asg + n_exp * (tm - 1) + tm - 1) // tm

    e_flat = ids.T.reshape(n_asg)
    onehot = (e_flat[:, None] == jnp.arange(n_exp, dtype=jnp.int32)[None, :]).astype(jnp.int32)
    counts = jnp.sum(onehot, axis=0)
    rank = jnp.sum((jnp.cumsum(onehot, axis=0) - onehot) * onehot, axis=1)
    tiles_per = (counts + tm - 1) // tm
    tile_end = jnp.cumsum(tiles_per)
    tile_start = tile_end - tiles_per
    slot = jnp.sum(onehot * tile_start[None, :], axis=1) * tm + rank
    asg = jnp.full((n_tiles * tm,), -1, jnp.int32).at[slot].set(jnp.arange(n_asg, dtype=jnp.int32),
                                                                 unique_indices=True)
    pad_row = n_asg + jnp.arange(n_tiles * tm, dtype=jnp.int32) % tm
    rows_in = jnp.where(asg >= 0, jnp.where(asg >= n, asg - n, asg), 0)
    rows_out = jnp.where(asg >= 0, asg, pad_row)
    rows_in_ext = jnp.concatenate([rows_in, jnp.zeros((tm,), jnp.int32)]).reshape(n_tiles + 1, 1, tm)
    rows_out_ext = jnp.concatenate([pad_row[:tm], rows_out]).reshape(n_tiles + 1, 1, tm)
    tile = jnp.arange(n_tiles, dtype=jnp.int32)
    last = tile_end[-1] - 1
    tile_c = jnp.minimum(tile, last)
    te = jnp.minimum(jnp.sum((tile_c[:, None] >= tile_end[None, :]).astype(jnp.int32), axis=1), n_exp - 1)
    te_hot = (te[:, None] == jnp.arange(n_exp, dtype=jnp.int32)[None, :]).astype(jnp.int32)
    cnt_t = jnp.sum(te_hot * counts[None, :], axis=1)
    start_t = jnp.sum(te_hot * tile_start[None, :], axis=1)
    nv = jnp.where(tile <= last, jnp.clip(cnt_t - (tile - start_t) * tm, 0, tm), 0).astype(jnp.int32)

    first = ((tile == start_t) & (nv > 0)).astype(jnp.int32)

    def chunk(i, c, nv_ref, first_ref):
        return jnp.where((nv_ref[i] > 0) & (first_ref[i] > 0), c, nc - 1)

    smem_rows = lambda index: pl.BlockSpec((1, 1, tm), index, memory_space=pltpu.SMEM)
    return pl.pallas_call(
        functools.partial(_moe_body, tm=tm, rows_per_step=tm // nc),
        grid_spec=pltpu.PrefetchScalarGridSpec(
            num_scalar_prefetch=4,
            grid=(n_tiles, nc),
            in_specs=[smem_rows(lambda i, c, te_ref, nv_ref, first_ref, last_ref: (0, 0, 0)),
                      smem_rows(lambda i, c, te_ref, nv_ref, first_ref, last_ref: (i + 1, 0, 0)),
                      smem_rows(lambda i, c, te_ref, nv_ref, first_ref, last_ref: (i, 0, 0)),
                      smem_rows(lambda i, c, te_ref, nv_ref, first_ref, last_ref: (i + 1, 0, 0)),
                      pl.BlockSpec(memory_space=pl.ANY),
                      pl.BlockSpec((1, d), lambda i, c, te_ref, nv_ref, first_ref, last_ref: (0, 0)),
                      pl.BlockSpec((None, d, tf), lambda i, c, te_ref, nv_ref, first_ref, last_ref:
                                   (te_ref[i], 0, chunk(i, c, nv_ref, first_ref))),
                      pl.BlockSpec((None, d, tf), lambda i, c, te_ref, nv_ref, first_ref, last_ref:
                                   (te_ref[i], 0, chunk(i, c, nv_ref, first_ref) + nc)),
                      pl.BlockSpec((None, tf, d), lambda i, c, te_ref, nv_ref, first_ref, last_ref:
                                   (te_ref[i], chunk(i, c, nv_ref, first_ref), 0))],
            out_specs=pl.BlockSpec(memory_space=pl.ANY),
            scratch_shapes=[pltpu.VMEM((2, tm, d), F32), pltpu.VMEM((tm, d), BF16), pltpu.VMEM((2, tm, d), F32),
                            pltpu.VMEM((nc, d, tf), BF16), pltpu.VMEM((nc, d, tf), BF16), pltpu.VMEM((nc, tf, d), BF16),
                            pltpu.SemaphoreType.DMA, pltpu.SemaphoreType.DMA]),
        out_shape=jax.ShapeDtypeStruct((n_asg + tm, d), F32),
        compiler_params=_params("arbitrary", "arbitrary"),
        name="moe_experts",
    )(te.astype(jnp.int32), nv, first, last.reshape(1).astype(jnp.int32), rows_in_ext, rows_in_ext, rows_out_ext,
      rows_out_ext, x, g, w_gu, w_gu, w_down)


def _combine_body(x_ref, y0_ref, y1_ref, wts_ref, g_ref, outp_ref, outs_ref, *, npt):
    wts = wts_ref[...]
    moe = wts[:, 0:1] * y0_ref[...] + wts[:, 1:2] * y1_ref[...]
    _store_split(pl.program_id(0) < npt, outp_ref, outs_ref, _rms(x_ref[...] + moe, g_ref[...]))


def _combine(x, y2, wts, g, n_p):
    n, d = x.shape
    n_s = n - n_p
    tm = _pick_tile(math.gcd(n_p, n_s), 512)
    nt = n // tm
    npt = n_p // tm
    row = lambda i: (i, 0)
    rp, rs = _split_maps(npt)
    return pl.pallas_call(
        functools.partial(_combine_body, npt=npt),
        grid=(nt,),
        in_specs=[pl.BlockSpec((tm, d), row), pl.BlockSpec((tm, d), row), pl.BlockSpec((tm, d), lambda i: (i + nt, 0)),
                  pl.BlockSpec((tm, LANES), row), pl.BlockSpec((1, d), lambda i: (0, 0))],
        out_specs=[pl.BlockSpec((tm, d), rp), pl.BlockSpec((tm, d), rs)],
        out_shape=[jax.ShapeDtypeStruct((n_p, d), F32), jax.ShapeDtypeStruct((n_s, d), F32)],
        compiler_params=_params("arbitrary"),
        name="moe_combine",
    )(x, y2, y2, wts, g)


def _rope_tables(pos, hd):
    half = hd // 2
    inv = ROPE_THETA ** (-jnp.arange(half, dtype=F32) / half)
    ang = pos.astype(F32)[:, None] * inv[None, :]
    cos = jnp.cos(ang)
    sin = jnp.sin(ang)
    reps = LANES // hd
    return jnp.tile(jnp.concatenate([cos, cos], axis=1), (1, reps)), jnp.tile(jnp.concatenate([-sin, sin], axis=1), (1, reps))


def kernel(x_prompt, x_sample, cache_k_win, cache_v_win, state_hgrn, norm_mix, norm_ffn, norm_final,
           w_qkv, b_qkv, w_o_attn, b_o_attn, sinks, w_in_hg, hg_lower, hg_norm, w_o_hg,
           w_gu_dense, w_down_dense, w_router, b_router, w_gu_moe, w_down_moe):
    batch, seq, d = x_prompt.shape
    n_seq, t_dec, _ = x_sample.shape
    n_kv, hd = cache_k_win.shape[3], cache_k_win.shape[4]
    n_heads = sinks.shape[1]
    g_per = n_heads // n_kv
    n_q, n_k = n_heads * hd, n_kv * hd
    n_p, n_s = batch * seq, n_seq * t_dec
    n_exp = w_router.shape[2]
    wc = cache_k_win.shape[2]
    assert norm_mix.shape[0] == 2 and hd * 2 == LANES and d % HG_DK == 0 and seq % WINDOW == 0 and wc == WINDOW

    tile_q = _pick_tile(math.gcd(n_p, n_s), 512)
    assert tile_q % t_dec == 0
    rope_p = _rope_tables(jnp.arange(seq), hd)
    rope_s = _rope_tables(jnp.tile(PAST_LEN + jnp.arange(t_dec), tile_q // t_dec), hd)
    x, q, k, v = _qkv_rope(x_prompt.reshape(n_p, d), x_sample.reshape(n_s, d), norm_mix[0:1], w_qkv[0], b_qkv[0:1],
                           rope_p, rope_s, n_q, n_k, hd)
    o = _swa_prompt(q, k, v, sinks[0], jnp.zeros((n_p + n_s, n_q), BF16), batch, seq, n_kv, g_per, hd)
    kc = cache_k_win[0].reshape(n_seq, wc, n_k)
    vc = cache_v_win[0].reshape(n_seq, wc, n_k)
    o = _swa_sample(q, k, v, kc, vc, sinks[0], o, n_p, n_seq, t_dec, n_kv, g_per, hd)
    last_win = lambda a: jnp.stack([a[(b + 1) * seq - wc:(b + 1) * seq] for b in range(batch)]).reshape(batch, wc, n_kv, hd)
    k_win_p, v_win_p = last_win(k), last_win(v)
    k_win_s = jnp.concatenate([cache_k_win[0][:, t_dec:], k[n_p:].reshape(n_seq, t_dec, n_kv, hd)], axis=1)
    v_win_s = jnp.concatenate([cache_v_win[0][:, t_dec:], v[n_p:].reshape(n_seq, t_dec, n_kv, hd)], axis=1)
    x = _oproj_ffn(o, x, w_o_attn[0], b_o_attn[0:1], norm_ffn[0:1], w_gu_dense[0], w_down_dense[0])

    lb_sm = jax.nn.softmax(hg_lower.astype(F32), axis=0)
    lb = (jnp.cumsum(lb_sm, axis=0) - lb_sm[0])[1:2]
    in_half = functools.partial(_hg_inproj_half, x, norm_mix[1:2], w_in_hg[0], lb)
    hq, hlf, hk = in_half(0)
    hv, hgate = in_half(1)
    n_hh = d // HG_DK
    c_p = math.gcd(seq, HG_CHUNK)
    o_p, s_p = _hg_core(hq, hk, hlf, hv, jnp.zeros((batch, n_hh, HG_DK, HG_DK), F32),
                        c_p, 2 if seq % (2 * c_p) == 0 else 1, 2 if batch % 2 == 0 else 1, t=seq)
    c_s = SUBLANES * ((t_dec + SUBLANES - 1) // SUBLANES)
    pad = lambda a: jnp.pad(a[n_p:].reshape(n_seq, t_dec, d), ((0, 0), (0, c_s - t_dec), (0, 0)))
    o_s, s_s = _hg_core(pad(hq), pad(hk), pad(hlf), pad(hv), state_hgrn[0], c_s, 1, 8 if n_seq % 8 == 0 else 1)
    o_s = o_s[:, :t_dec].reshape(n_s, d)

    wr = jnp.pad(w_router[0], ((0, 0), (0, LANES - n_exp)))
    br = jnp.pad(b_router[0:1], ((0, 0), (0, LANES - n_exp)))
    x3, ids, wts = _hg_out_router(o_p.reshape(n_p, d), o_s, hgate, x, hg_norm[0:1], w_o_hg[0], norm_ffn[1:2],
                                  wr, br, n_exp)
    y2 = _moe_experts(x3, norm_ffn[1:2], ids[:, :TOP_K], w_gu_moe[0], w_down_moe[0])
    y_p, y_s = _combine(x3, y2, wts, norm_final.reshape(1, d), n_p)

    return (y_p.reshape(batch, seq, d), y_s.reshape(n_seq, t_dec, d),
            k_win_p[None], v_win_p[None], k_win_s[None], v_win_s[None], s_p[None], s_s[None])
```

```python
import functools
import math

import jax
import jax.numpy as jnp
from jax import lax
from jax.experimental import pallas as pl
from jax.experimental.pallas import tpu as pltpu

F32 = jnp.float32
BF16 = jnp.bfloat16

NORM_EPS = 1e-5
WINDOW = 128
PAST_LEN = 16384
ROPE_THETA = 10000.0
HG_DK = 128
HG_CHUNK = 128
HG_GROUP = 2
HG_SAFE_DECAY = 60.0
HG_SAFE_Q = 1e9
TOP_K = 2
LANES = 128
SUBLANES = 8
VMEM_LIMIT = 56 * 1024 * 1024

NT_DIMS = (((1,), (1,)), ((), ()))
TN_DIMS = (((0,), (0,)), ((), ()))


def _pick_tile(n, target):
    for t in (1536, 1024, 768, 512, 384, 256, 192, 128, 64, 32, 16, 8):
        if t <= target and n % t == 0:
            return t
    raise ValueError(f"no row tile for {n}")


def _params(*sem):
    return pltpu.CompilerParams(dimension_semantics=sem, vmem_limit_bytes=VMEM_LIMIT)


def _rms(x, g):
    return x * lax.rsqrt(jnp.mean(x * x, axis=-1, keepdims=True) + NORM_EPS) * g


def _split_maps(npt):
    return (lambda i, *_: (jnp.minimum(i, npt - 1), 0)), (lambda i, *_: (jnp.maximum(i - npt, 0), 0))


def _store_split(is_prompt, ref_p, ref_s, val):
    @pl.when(is_prompt)
    def _():
        ref_p[...] = val

    @pl.when(jnp.logical_not(is_prompt))
    def _():
        ref_s[...] = val


def _qkv_body(xp_ref, xs_ref, g_ref, w_ref, b_ref, cosp_ref, sinp_ref, coss_ref, sins_ref,
              x_ref, q_ref, k_ref, v_ref, wbf_ref,
              *, npt, n_q, n_k, hd):
    i = pl.program_id(0)

    @pl.when(i == 0)
    def _():
        wbf_ref[...] = w_ref[...].astype(BF16)

    x = jnp.where(i < npt, xp_ref[...], xs_ref[...])
    x_ref[...] = x
    h = _rms(x, g_ref[...]).astype(BF16)
    y = jnp.dot(h, wbf_ref[...], preferred_element_type=F32) + b_ref[...]
    cos = jnp.where(i < npt, cosp_ref[...], coss_ref[...])
    sin = jnp.where(i < npt, sinp_ref[...], sins_ref[...])
    lane = lax.broadcasted_iota(jnp.int32, cos.shape, 1)
    first = (lane % hd) < (hd // 2)

    def rope(blk):
        partner = jnp.where(first, pltpu.roll(blk, LANES - hd // 2, 1), pltpu.roll(blk, hd // 2, 1))
        return blk * cos + partner * sin

    scale = hd ** -0.5
    for j in range(n_q // LANES):
        q_ref[:, j * LANES:(j + 1) * LANES] = (rope(y[:, j * LANES:(j + 1) * LANES]) * scale).astype(BF16)
    for j in range(n_k // LANES):
        k_ref[:, j * LANES:(j + 1) * LANES] = rope(y[:, n_q + j * LANES:n_q + (j + 1) * LANES])
    v_ref[...] = y[:, n_q + n_k:]


def _qkv_rope(xp, xs, g, w, b, rope_p, rope_s, n_q, n_k, hd):
    (n_p, d), n_s = xp.shape, xs.shape[0]
    n = n_p + n_s
    n_out = w.shape[1]
    tm = _pick_tile(math.gcd(n_p, n_s), 512)
    npt = n_p // tm
    row = lambda i: (i, 0)
    fix = lambda i: (0, 0)
    rp, rs = _split_maps(npt)
    seq_tiles = rope_p[0].shape[0] // tm
    assert rope_p[0].shape[0] % tm == 0 and rope_s[0].shape[0] == tm
    pos_p = lambda i: (jnp.minimum(i, npt - 1) % seq_tiles, 0)
    return pl.pallas_call(
        functools.partial(_qkv_body, npt=npt, n_q=n_q, n_k=n_k, hd=hd),
        grid=(n // tm,),
        in_specs=[pl.BlockSpec((tm, d), rp), pl.BlockSpec((tm, d), rs), pl.BlockSpec((1, d), fix),
                  pl.BlockSpec((d, n_out), fix), pl.BlockSpec((1, n_out), fix),
                  pl.BlockSpec((tm, LANES), pos_p), pl.BlockSpec((tm, LANES), pos_p),
                  pl.BlockSpec((tm, LANES), fix), pl.BlockSpec((tm, LANES), fix)],
        out_specs=[pl.BlockSpec((tm, d), row), pl.BlockSpec((tm, n_q), row), pl.BlockSpec((tm, n_k), row),
                   pl.BlockSpec((tm, n_k), row)],
        out_shape=[jax.ShapeDtypeStruct((n, d), F32), jax.ShapeDtypeStruct((n, n_q), BF16),
                   jax.ShapeDtypeStruct((n, n_k), F32), jax.ShapeDtypeStruct((n, n_k), F32)],
        scratch_shapes=[pltpu.VMEM((d, n_out), BF16)],
        compiler_params=_params("arbitrary"),
        name="qkv_rope",
    )(xp, xs, g, w, b, *rope_p, *rope_s)


def _sink_column(sink_ref, kh, g_per, rows_per):
    blk = lax.broadcasted_iota(jnp.int32, (g_per * rows_per, 1), 0) // rows_per
    col = jnp.full((g_per * rows_per, 1), sink_ref[kh * g_per], F32)
    for g in range(1, g_per):
        col = jnp.where(blk == g, sink_ref[kh * g_per + g], col)
    return col


def _stack_heads(q, kh, g_per, hd):
    return jnp.concatenate([q[:, (kh * g_per + g) * hd:(kh * g_per + g + 1) * hd] for g in range(g_per)], axis=0)


def _swa_prompt_body(sink_ref, q_ref, kp_ref, kc_ref, vp_ref, vc_ref, o_all_ref, o_ref, *, n_kv, g_per, hd, w):
    del o_all_ref
    n = pl.program_id(1)
    q = q_ref[...]
    kk = jnp.concatenate([kp_ref[...], kc_ref[...]], axis=0).astype(BF16)
    vv = jnp.concatenate([vp_ref[...], vc_ref[...]], axis=0).astype(BF16)
    i = lax.broadcasted_iota(jnp.int32, (w, 2 * w), 0)
    j = lax.broadcasted_iota(jnp.int32, (w, 2 * w), 1)
    mask = (j > i) & (j <= i + w) & ((j >= w) | (n > 0))
    outs = []
    for h in range(n_kv * g_per):
        ks = slice((h // g_per) * hd, (h // g_per + 1) * hd)
        s = lax.dot_general(q[:, h * hd:(h + 1) * hd], kk[:, ks], NT_DIMS, preferred_element_type=F32)
        s = jnp.where(mask, s, -jnp.inf)
        sink = sink_ref[h]
        m = jnp.maximum(jnp.max(s, axis=-1, keepdims=True), sink)
        p = jnp.exp(s - m)
        p = p / (jnp.sum(p, axis=-1, keepdims=True) + jnp.exp(sink - m))
        outs.append(jnp.dot(p.astype(BF16), vv[:, ks], preferred_element_type=F32))
    o_ref[...] = jnp.concatenate(outs, axis=1).astype(BF16)


def _swa_prompt(q, k, v, sinks, o_all, batch, seq, n_kv, g_per, hd):
    w = WINDOW
    nb = seq // w
    dq = q.shape[1]
    dkv = k.shape[1]
    cur = lambda b, n: (b * nb + n, 0)
    prev = lambda b, n: (b * nb + jnp.maximum(n - 1, 0), 0)
    return pl.pallas_call(
        functools.partial(_swa_prompt_body, n_kv=n_kv, g_per=g_per, hd=hd, w=w),
        grid=(batch, nb),
        in_specs=[pl.BlockSpec(memory_space=pltpu.SMEM),
                  pl.BlockSpec((w, dq), cur), pl.BlockSpec((w, dkv), prev), pl.BlockSpec((w, dkv), cur),
                  pl.BlockSpec((w, dkv), prev), pl.BlockSpec((w, dkv), cur),
                  pl.BlockSpec(memory_space=pl.ANY)],
        out_specs=pl.BlockSpec((w, dq), cur),
        out_shape=jax.ShapeDtypeStruct(o_all.shape, o_all.dtype),
        input_output_aliases={6: 0},
        compiler_params=_params("arbitrary", "arbitrary"),
        name="swa_prompt",
    )(sinks, q, k, k, v, v, o_all)


def _swa_sample_body(sink_ref, q_ref, kn_ref, vn_ref, kc_ref, vc_ref, o_all_ref, o_ref, *, n_kv, g_per, hd, bt, t, wc):
    del o_all_ref
    r = bt * t
    q = q_ref[...]
    kn = kn_ref[...].astype(BF16)
    vn = vn_ref[...].astype(BF16)
    kc = kc_ref[...].reshape(bt * wc, n_kv * hd).astype(BF16)
    vc = vc_ref[...].reshape(bt * wc, n_kv * hd).astype(BF16)
    row_c = lax.broadcasted_iota(jnp.int32, (g_per * r, bt * wc), 0) % r
    col_c = lax.broadcasted_iota(jnp.int32, (g_per * r, bt * wc), 1)
    mask_c = (col_c // wc == row_c // t) & (col_c % wc > row_c % t + (wc - WINDOW))
    row_n = lax.broadcasted_iota(jnp.int32, (g_per * r, r), 0) % r
    col_n = lax.broadcasted_iota(jnp.int32, (g_per * r, r), 1)
    mask_n = (col_n // t == row_n // t) & (col_n % t <= row_n % t)
    outs = []
    for kh in range(n_kv):
        hs = slice(kh * hd, (kh + 1) * hd)
        q4 = _stack_heads(q, kh, g_per, hd)
        sc = jnp.where(mask_c, lax.dot_general(q4, kc[:, hs], NT_DIMS, preferred_element_type=F32), -jnp.inf)
        sn = jnp.where(mask_n, lax.dot_general(q4, kn[:, hs], NT_DIMS, preferred_element_type=F32), -jnp.inf)
        sink = _sink_column(sink_ref, kh, g_per, r)
        m = jnp.maximum(jnp.maximum(jnp.max(sc, axis=-1, keepdims=True), jnp.max(sn, axis=-1, keepdims=True)), sink)
        pc = jnp.exp(sc - m)
        pn = jnp.exp(sn - m)
        den = jnp.sum(pc, axis=-1, keepdims=True) + jnp.sum(pn, axis=-1, keepdims=True) + jnp.exp(sink - m)
        o4 = (jnp.dot((pc / den).astype(BF16), vc[:, hs], preferred_element_type=F32)
              + jnp.dot((pn / den).astype(BF16), vn[:, hs], preferred_element_type=F32))
        outs.append(jnp.concatenate([o4[g * r:(g + 1) * r] for g in range(g_per)], axis=1))
    o_ref[...] = jnp.concatenate(outs, axis=1).astype(BF16)


def _swa_sample(q, k, v, k_cache, v_cache, sinks, o_all, row0, n_seq, t, n_kv, g_per, hd):
    wc = k_cache.shape[1]
    dq = q.shape[1]
    dkv = k.shape[1]
    bt = 8 if n_seq % 8 == 0 else n_seq
    r = bt * t
    assert row0 % r == 0
    off = row0 // r
    rows = lambda i: (off + i, 0)
    return pl.pallas_call(
        functools.partial(_swa_sample_body, n_kv=n_kv, g_per=g_per, hd=hd, bt=bt, t=t, wc=wc),
        grid=(n_seq // bt,),
        in_specs=[pl.BlockSpec(memory_space=pltpu.SMEM),
                  pl.BlockSpec((r, dq), rows), pl.BlockSpec((r, dkv), rows), pl.BlockSpec((r, dkv), rows),
                  pl.BlockSpec((bt, wc, dkv), lambda i: (i, 0, 0)), pl.BlockSpec((bt, wc, dkv), lambda i: (i, 0, 0)),
                  pl.BlockSpec(memory_space=pl.ANY)],
        out_specs=pl.BlockSpec((r, dq), rows),
        out_shape=jax.ShapeDtypeStruct(o_all.shape, o_all.dtype),
        input_output_aliases={6: 0},
        compiler_params=_params("arbitrary"),
        name="swa_sample",
    )(sinks, q, k, v, k_cache, v_cache, o_all)


def _swiglu_step(h, wg, wu, wd, acc_ref):
    a = jnp.dot(h, wg, preferred_element_type=F32)
    b = jnp.dot(h, wu, preferred_element_type=F32)
    act = (a * jax.nn.sigmoid(a) * b).astype(BF16)
    acc_ref[...] += jnp.dot(act, wd, preferred_element_type=F32)


def _oproj_ffn_body(o_ref, x_ref, wo_ref, bo_ref, g_ref, wg_ref, wu_ref, wd_ref, out_ref,
                    wo_bf, x1_ref, h_ref, acc_ref):
    i = pl.program_id(0)
    c = pl.program_id(1)

    @pl.when((i == 0) & (c == 0))
    def _():
        wo_bf[...] = wo_ref[...].astype(BF16)

    @pl.when(c == 0)
    def _():
        x1 = x_ref[...] + jnp.dot(o_ref[...], wo_bf[...], preferred_element_type=F32) + bo_ref[...]
        x1_ref[...] = x1
        h_ref[...] = _rms(x1, g_ref[...]).astype(BF16)
        acc_ref[...] = jnp.zeros_like(acc_ref)

    _swiglu_step(h_ref[...], wg_ref[...].astype(BF16), wu_ref[...].astype(BF16), wd_ref[...].astype(BF16), acc_ref)

    @pl.when(c == pl.num_programs(1) - 1)
    def _():
        out_ref[...] = x1_ref[...] + acc_ref[...]


def _ff_chunk(d_ff):
    for tf in (512, 256, 128):
        if d_ff % tf == 0:
            return tf
    raise ValueError(f"d_ff {d_ff} is not a multiple of {LANES}")


def _oproj_ffn(o, x, wo, bo, g, w_gu, w_down):
    n, d = x.shape
    d_ff = w_down.shape[0]
    tf = _ff_chunk(d_ff)
    nc = d_ff // tf
    tm = _pick_tile(n, 768)
    row = lambda i, c: (i, 0)
    fix = lambda i, c: (0, 0)
    return pl.pallas_call(
        _oproj_ffn_body,
        grid=(n // tm, nc),
        in_specs=[pl.BlockSpec((tm, o.shape[1]), row), pl.BlockSpec((tm, d), row),
                  pl.BlockSpec(wo.shape, fix), pl.BlockSpec((1, d), fix), pl.BlockSpec((1, d), fix),
                  pl.BlockSpec((d, tf), lambda i, c: (0, c)), pl.BlockSpec((d, tf), lambda i, c: (0, c + nc)),
                  pl.BlockSpec((tf, d), lambda i, c: (c, 0))],
        out_specs=pl.BlockSpec((tm, d), row),
        out_shape=jax.ShapeDtypeStruct((n, d), F32),
        scratch_shapes=[pltpu.VMEM(wo.shape, BF16), pltpu.VMEM((tm, d), F32), pltpu.VMEM((tm, d), BF16),
                        pltpu.VMEM((tm, d), F32)],
        compiler_params=_params("arbitrary", "arbitrary"),
        name="oproj_ffn",
    )(o, x, wo, bo, g, w_gu, w_gu, w_down)


def _hg_in_body(x_ref, g_ref, w_ref, lb_ref, *refs, half):
    out_refs, wbf_ref = refs[:-1], refs[-1]
    d = x_ref.shape[1]

    @pl.when(pl.program_id(0) == 0)
    def _():
        wbf_ref[...] = w_ref[...].astype(BF16)

    h = _rms(x_ref[...], g_ref[...]).astype(BF16)
    z = jnp.dot(h, wbf_ref[...], preferred_element_type=F32)
    z0, z1 = z[:, :d], z[:, d:]
    if half == 0:
        out_refs[0][...] = z0 * jax.nn.sigmoid(z0) * (HG_DK ** -0.5)
        lb = lb_ref[...]
        t = jnp.exp(-jnp.abs(z1))
        log_sig = jnp.minimum(z1, 0.0) - jnp.log(1.0 + t)
        a = jnp.log(lb)
        b = jnp.log1p(-lb) + log_sig
        out_refs[1][...] = jnp.maximum(a, b) + jnp.log(1.0 + jnp.exp(-jnp.abs(a - b)))
        out_refs[2][...] = (1.0 - lb) * (jnp.where(z1 >= 0.0, t, 1.0) / (1.0 + t))
    else:
        out_refs[0][...] = z0
        out_refs[1][...] = z1 * jax.nn.sigmoid(z1)


def _hg_inproj_half(x, g, w_in, lb, half):
    n, d = x.shape
    n_res = 3 if half == 0 else 2
    tm = _pick_tile(n, 512)
    row = lambda i: (i, 0)
    fix = lambda i: (0, 0)
    return pl.pallas_call(
        functools.partial(_hg_in_body, half=half),
        grid=(n // tm,),
        in_specs=[pl.BlockSpec((tm, d), row), pl.BlockSpec((1, d), fix), pl.BlockSpec((d, 2 * d), lambda i: (0, half)),
                  pl.BlockSpec((1, d), fix)],
        out_specs=[pl.BlockSpec((tm, d), row)] * n_res,
        out_shape=[jax.ShapeDtypeStruct((n, d), F32)] * n_res,
        scratch_shapes=[pltpu.VMEM((d, 2 * d), BF16)],
        compiler_params=_params("arbitrary"),
        name=f"hgrn_inproj_{half}",
    )(x, g, w_in, lb)


def _hg_core_body(*refs, c_len, n_chunk, n_seq, n_heads, carried, per_seq_inputs, gsz):
    n_in = n_seq if per_seq_inputs else 1
    in_refs, (s0_ref, o_ref, sout_ref, st_ref, g_ref, oi_ref) = refs[:4 * n_in], refs[4 * n_in:]

    def rd(kind, s, row_slice):
        if per_seq_inputs:
            return in_refs[kind * n_in + s][row_slice, :]
        return in_refs[kind][s, row_slice, :]

    c = pl.program_id(1)
    dk = HG_DK
    rb = c_len * n_chunk
    units = [(s, j) for s in range(n_seq) for j in range(n_chunk)]

    n_pairs = n_heads // gsz
    gw = gsz * dk
    state_diag = (lax.broadcasted_iota(jnp.int32, (gw, gw), 0) // dk == lax.broadcasted_iota(jnp.int32, (gw, gw), 1) // dk)
    rows_diag = (lax.broadcasted_iota(jnp.int32, (gsz * c_len, gw), 0) // c_len
                 == lax.broadcasted_iota(jnp.int32, (gsz * c_len, gw), 1) // dk)

    def pair_tile(x):
        x2 = jnp.concatenate([x] * gsz, axis=0)
        return jnp.where(rows_diag, x2, jnp.zeros_like(x2))

    def diag(g):
        return slice(g * dk, (g + 1) * dk)

    if carried:
        @pl.when(c == 0)
        def _():
            for s in range(n_seq):
                for p in range(n_pairs):
                    st_ref[s, p] = jnp.zeros((gw, gw), F32)
                    for g in range(gsz):
                        st_ref[s, p, diag(g), diag(g)] = s0_ref[s, gsz * p + g].T

    r = lax.broadcasted_iota(jnp.int32, (rb, rb), 0)
    cidx = lax.broadcasted_iota(jnp.int32, (rb, rb), 1)
    block_causal = ((r >= cidx) & (r // c_len == cidx // c_len)).astype(BF16)
    causal2 = (lax.broadcasted_iota(jnp.int32, (c_len, gsz * c_len), 0)
               >= lax.broadcasted_iota(jnp.int32, (c_len, gsz * c_len), 1) % c_len)

    def rows(j):
        return slice(j * c_len, (j + 1) * c_len)

    gcum, safe = [], None
    for s in range(n_seq):
        lf = rd(2, s, slice(None))
        lf_hi = lf.astype(BF16)
        rest = lf - lf_hi.astype(F32)
        lf_mid = rest.astype(BF16)
        lf_lo = (rest - lf_mid.astype(F32)).astype(BF16)
        gs = ((jnp.dot(block_causal, lf_lo, preferred_element_type=F32)
               + jnp.dot(block_causal, lf_mid, preferred_element_type=F32))
              + jnp.dot(block_causal, lf_hi, preferred_element_type=F32))
        gcum.append(gs)
        for j in range(n_chunk):
            mid = gs[j * c_len + c_len // 2 - 1:j * c_len + c_len // 2, :]
            ok = ((jnp.max(gs[j * c_len:j * c_len + 1, :] - mid) <= HG_SAFE_DECAY)
                  & (jnp.max(mid - gs[(j + 1) * c_len - 1:(j + 1) * c_len, :]) <= HG_SAFE_DECAY)
                  & (jnp.max(jnp.abs(rd(0, s, rows(j)))) <= HG_SAFE_Q))
            safe = ok if safe is None else (safe & ok)

    def g_mid(s, j):
        return gcum[s][j * c_len + c_len // 2 - 1:j * c_len + c_len // 2, :]

    def g_last(s, j):
        return gcum[s][(j + 1) * c_len - 1:(j + 1) * c_len, :]

    @pl.when(safe)
    def _():
        for u, (s, j) in enumerate(units):
            g = gcum[s][rows(j), :]
            qi = (rd(0, s, rows(j)) * jnp.exp(g - g_mid(s, j))).astype(BF16)
            ki = (rd(1, s, rows(j)) * jnp.exp(g_mid(s, j) - g)).astype(BF16)
            vb = rd(3, s, rows(j)).astype(BF16)
            for p in range(n_pairs):
                ps = slice(p * gw, (p + 1) * gw)
                a = lax.dot_general(qi[:, ps], pair_tile(ki[:, ps]), NT_DIMS, preferred_element_type=F32)
                a = jnp.where(causal2, a, 0.0).astype(BF16)
                oi_ref[u, :, ps] = jnp.dot(a, pair_tile(vb[:, ps]), preferred_element_type=F32)

    @pl.when(jnp.logical_not(safe))
    def _():
        lane_h = lax.broadcasted_iota(jnp.int32, (n_heads * dk, n_heads * dk), 0) // dk
        lane_w = lax.broadcasted_iota(jnp.int32, (n_heads * dk, n_heads * dk), 1) // dk
        head_sum = (lane_h == lane_w).astype(BF16)
        t_idx = lax.broadcasted_iota(jnp.int32, (c_len, 1), 0)
        for u, (s, j) in enumerate(units):
            g = gcum[s][rows(j), :]
            g_ref[...] = g
            q = rd(0, s, rows(j))

            def key_row(i, acc, s=s, j=j, g=g, q=q):
                gi = g_ref[pl.ds(i, 1), :]
                decay = jnp.exp(jnp.where(t_idx >= i, g - gi, -jnp.inf))
                term = (q * decay * rd(1, s, pl.ds(j * c_len + i, 1))).astype(BF16)
                a_i = jnp.dot(term, head_sum, preferred_element_type=F32)
                return acc + a_i * rd(3, s, pl.ds(j * c_len + i, 1))

            oi_ref[u] = lax.fori_loop(0, c_len, key_row, jnp.zeros((c_len, n_heads * dk), F32))

    if not carried:
        e_rows = [jnp.exp(g_last(s, 0)) for s in range(n_seq)]
        e_cols = jnp.concatenate(e_rows + [jnp.zeros((LANES - n_seq, n_heads * dk), F32)], axis=0).T
        zero = jnp.zeros((dk, dk), F32)

    for u, (s, j) in enumerate(units):
        g = gcum[s][rows(j), :]
        qs = (rd(0, s, rows(j)) * jnp.exp(g)).astype(BF16)
        ks = (rd(1, s, rows(j)) * jnp.exp(g_last(s, j) - g)).astype(BF16)
        vb = rd(3, s, rows(j)).astype(BF16)
        e_last = jnp.exp(g_last(s, j))
        for p in range(n_pairs):
            ps = slice(p * gw, (p + 1) * gw)
            if carried:
                st = st_ref[s, p]
                o_ref[s, rows(j), ps] = oi_ref[u, :, ps] + lax.dot_general(qs[:, ps], st.astype(BF16), NT_DIMS,
                                                                           preferred_element_type=F32)
                upd = lax.dot_general(vb[:, ps], ks[:, ps], TN_DIMS, preferred_element_type=F32)
                st_ref[s, p] = jnp.where(state_diag, e_last[:, ps] * st + upd, 0.0)
            else:
                st = jnp.concatenate([jnp.concatenate([zero] * g + [s0_ref[s, gsz * p + g]] + [zero] * (gsz - 1 - g), axis=1)
                                      for g in range(gsz)], axis=0)
                o_ref[s, rows(j), ps] = oi_ref[u, :, ps] + jnp.dot(qs[:, ps], st.astype(BF16),
                                                                   preferred_element_type=F32)
                upd = lax.dot_general(ks[:, ps], vb[:, ps], TN_DIMS, preferred_element_type=F32)
                new = e_cols[ps, s:s + 1] * st + upd
                for g in range(gsz):
                    sout_ref[s, gsz * p + g] = new[diag(g), diag(g)]

    if carried:
        @pl.when(c == pl.num_programs(1) - 1)
        def _():
            for s in range(n_seq):
                for p in range(n_pairs):
                    for g in range(gsz):
                        sout_ref[s, gsz * p + g] = st_ref[s, p, diag(g), diag(g)].T


def _hg_core(q, k, lf, v, s0, c_len, n_chunk, n_seq, t=None):
    batch = s0.shape[0]
    per_seq_inputs = q.ndim == 2
    t = t if per_seq_inputs else q.shape[1]
    d = q.shape[-1]
    n_heads = d // HG_DK
    rb = c_len * n_chunk
    nc = t // rb
    carried = t > c_len
    gsz = HG_GROUP if carried and n_heads % HG_GROUP == 0 else 2
    assert t % rb == 0 and batch % n_seq == 0 and n_heads % gsz == 0 and n_seq <= LANES
    rows = lambda b, c: (b, c, 0)
    state = lambda b, c: (b, 0, 0, 0)
    if per_seq_inputs:
        seq_rows = lambda s: (lambda b, c: ((b * n_seq + s) * nc + c, 0))
        in_specs = [pl.BlockSpec((rb, d), seq_rows(s)) for _ in range(4) for s in range(n_seq)]
        inputs = [a for a in (q, k, lf, v) for _ in range(n_seq)]
    else:
        in_specs = [pl.BlockSpec((n_seq, rb, d), rows)] * 4
        inputs = [q, k, lf, v]
    return pl.pallas_call(
        functools.partial(_hg_core_body, c_len=c_len, n_chunk=n_chunk, n_seq=n_seq, n_heads=n_heads, carried=carried,
                          per_seq_inputs=per_seq_inputs, gsz=gsz),
        grid=(batch // n_seq, nc),
        in_specs=in_specs + [pl.BlockSpec((n_seq, n_heads, HG_DK, HG_DK), state)],
        out_specs=[pl.BlockSpec((n_seq, rb, d), rows), pl.BlockSpec((n_seq, n_heads, HG_DK, HG_DK), state)],
        out_shape=[jax.ShapeDtypeStruct((batch, t, d), F32),
                   jax.ShapeDtypeStruct((batch, n_heads, HG_DK, HG_DK), F32)],
        scratch_shapes=[pltpu.VMEM((n_seq if carried else 1, n_heads // gsz, gsz * HG_DK, gsz * HG_DK), F32),
                        pltpu.VMEM((c_len, d), F32),
                        pltpu.VMEM((n_seq * n_chunk, c_len, d), F32)],
        compiler_params=_params("arbitrary", "arbitrary"),
        name=f"hgrn_core_{c_len}",
    )(*inputs, s0)


def _hg_out_router_body(op_ref, os_ref, gate_ref, x_ref, gn_ref, wo_ref, g_ref, wr_ref, br_ref,
                        x3_ref, ids_ref, wts_ref, wo_bf, *, npt, n_heads, n_exp):
    i = pl.program_id(0)

    @pl.when(i == 0)
    def _():
        wo_bf[...] = wo_ref[...].astype(BF16)

    dk = HG_DK
    o = jnp.where(i < npt, op_ref[...], os_ref[...])
    gn = gn_ref[...]
    normed = jnp.concatenate([_rms(o[:, h * dk:(h + 1) * dk], gn) for h in range(n_heads)], axis=1)
    y = (normed * gate_ref[...]).astype(BF16)
    x3 = x_ref[...] + jnp.dot(y, wo_bf[...], preferred_element_type=F32)
    x3_ref[...] = x3
    h4 = _rms(x3, g_ref[...])
    h_hi = h4.astype(BF16)
    h_lo = (h4 - h_hi.astype(F32)).astype(BF16)
    wr = wr_ref[...]
    w_hi = wr.astype(BF16)
    w_lo = (wr - w_hi.astype(F32)).astype(BF16)
    logits = (jnp.dot(h_hi, w_hi, preferred_element_type=F32)
              + (jnp.dot(h_lo, w_hi, preferred_element_type=F32) + jnp.dot(h_hi, w_lo, preferred_element_type=F32))
              + br_ref[...])
    lane = lax.broadcasted_iota(jnp.int32, logits.shape, 1)
    logits = jnp.where(lane < n_exp, logits, -jnp.inf)
    m1 = jnp.max(logits, axis=-1, keepdims=True)
    i1 = jnp.min(jnp.where(logits == m1, lane, LANES), axis=-1, keepdims=True)
    rest = jnp.where(lane == i1, -jnp.inf, logits)
    m2 = jnp.max(rest, axis=-1, keepdims=True)
    i2 = jnp.min(jnp.where(rest == m2, lane, LANES), axis=-1, keepdims=True)
    e2 = jnp.exp(m2 - m1)
    den = 1.0 + e2
    ids_ref[...] = jnp.where(lane == 0, i1, jnp.where(lane == 1, i2, 0))
    wts_ref[...] = jnp.where(lane == 0, 1.0 / den, jnp.where(lane == 1, e2 / den, 0.0))


def _hg_out_router(o_p, o_s, gate, x, gn, wo, g, w_router, b_router, n_exp):
    n, d = x.shape
    n_p, n_s = o_p.shape[0], o_s.shape[0]
    tm = _pick_tile(math.gcd(n_p, n_s), 512)
    npt = n_p // tm
    row = lambda i: (i, 0)
    fix = lambda i: (0, 0)
    rp, rs = _split_maps(npt)
    return pl.pallas_call(
        functools.partial(_hg_out_router_body, npt=npt, n_heads=d // HG_DK, n_exp=n_exp),
        grid=(n // tm,),
        in_specs=[pl.BlockSpec((tm, d), rp), pl.BlockSpec((tm, d), rs), pl.BlockSpec((tm, d), row),
                  pl.BlockSpec((tm, d), row),
                  pl.BlockSpec((1, HG_DK), fix), pl.BlockSpec(wo.shape, fix), pl.BlockSpec((1, d), fix),
                  pl.BlockSpec((d, LANES), fix), pl.BlockSpec((1, LANES), fix)],
        out_specs=[pl.BlockSpec((tm, d), row), pl.BlockSpec((tm, LANES), row), pl.BlockSpec((tm, LANES), row)],
        out_shape=[jax.ShapeDtypeStruct((n, d), F32), jax.ShapeDtypeStruct((n, LANES), jnp.int32),
                   jax.ShapeDtypeStruct((n, LANES), F32)],
        scratch_shapes=[pltpu.VMEM(wo.shape, BF16)],
        compiler_params=_params("arbitrary"),
        name="hgrn_out_router",
    )(o_p, o_s, gate, x, gn, wo, g, w_router, b_router)


def _moe_body(te_ref, nv_ref, first_ref, last_ref, rin0_ref, rin_next_ref, rout_prev_ref, rout_cur_ref,
              x_hbm, g_ref, wg_ref, wu_ref, wd_ref, y_hbm,
              xg_ref, h_ref, acc_ref, wg_res, wu_res, wd_res, sem_in, sem_out, *, tm, rows_per_step):
    del te_ref
    i = pl.program_id(0)
    c = pl.program_id(1)
    slot = i % 2
    other = 1 - slot

    def row_in(buf, j, token):
        return pltpu.make_async_copy(x_hbm.at[pl.ds(token, 1)], xg_ref.at[buf, pl.ds(j, 1)], sem_in)

    def row_out(buf, j, dst):
        return pltpu.make_async_copy(acc_ref.at[buf, pl.ds(j, 1)], y_hbm.at[pl.ds(dst, 1)], sem_out)

    def all_in(buf):
        return pltpu.make_async_copy(x_hbm.at[pl.ds(0, tm)], xg_ref.at[buf], sem_in)

    def all_out(buf):
        return pltpu.make_async_copy(acc_ref.at[buf], y_hbm.at[pl.ds(0, tm)], sem_out)

    def each_row(fn):
        def body(j, carry):
            fn(j)
            return carry
        lax.fori_loop(0, tm, body, 0, unroll=8)

    @pl.when(nv_ref[i] > 0)
    def _():
        @pl.when((i == 0) & (c == 0))
        def _():
            acc_ref[1] = jnp.zeros(acc_ref.shape[1:], F32)
            each_row(lambda j: row_in(0, j, rin0_ref[0, 0, j]).start())
            all_in(0).wait()

        @pl.when(c == 0)
        def _():
            h_ref[...] = _rms(xg_ref[slot], g_ref[...]).astype(BF16)
            acc_ref[slot] = jnp.zeros(acc_ref.shape[1:], F32)

        def stream_rows():
            for u in range(rows_per_step):
                j = c * rows_per_step + u
                row_in(other, j, rin_next_ref[0, 0, j]).start()
                row_out(other, j, rout_prev_ref[0, 0, j]).start(priority=1)

        @pl.when(first_ref[i] > 0)
        def _():
            stream_rows()
            wg_res[c] = wg_ref[...].astype(BF16)
            wu_res[c] = wu_ref[...].astype(BF16)
            wd_res[c] = wd_ref[...].astype(BF16)
            _swiglu_step(h_ref[...], wg_res[c], wu_res[c], wd_res[c], acc_ref.at[slot])

        @pl.when(first_ref[i] == 0)
        def _():
            stream_rows()
            _swiglu_step(h_ref[...], wg_res[c], wu_res[c], wd_res[c], acc_ref.at[slot])

        @pl.when(c == pl.num_programs(1) - 1)
        def _():
            all_in(other).wait()
            all_out(other).wait()

            @pl.when(i == last_ref[0])
            def _():
                each_row(lambda j: row_out(slot, j, rout_cur_ref[0, 0, j]).start())
                all_out(slot).wait()


MOE_TILE_ROWS = 672
BF16_SUBLANES = 16


def _moe_experts(x, g, ids, w_gu, w_down):
    n, d = x.shape
    n_exp, d_ff = w_down.shape[0], w_down.shape[1]
    tf = _ff_chunk(d_ff)
    nc = d_ff // tf
    unit = BF16_SUBLANES * nc
    tm = unit * max(1, round(MOE_TILE_ROWS / unit))
    n_asg = n * TOP_K
    n_tiles = (n_asg + n_exp * (tm - 1) + tm - 1) // tm

    e_flat = ids.T.reshape(n_asg)
    onehot = (e_flat[:, None] == jnp.arange(n_exp, dtype=jnp.int32)[None, :]).astype(jnp.int32)
    counts = jnp.sum(onehot, axis=0)
    rank = jnp.sum((jnp.cumsum(onehot, axis=0) - onehot) * onehot, axis=1)
    tiles_per = (counts + tm - 1) // tm
    tile_end = jnp.cumsum(tiles_per)
    tile_start = tile_end - tiles_per
    slot = jnp.sum(onehot * tile_start[None, :], axis=1) * tm + rank
    asg = jnp.full((n_tiles * tm,), -1, jnp.int32).at[slot].set(jnp.arange(n_asg, dtype=jnp.int32),
                                                                 unique_indices=True)
    pad_row = n_asg + jnp.arange(n_tiles * tm, dtype=jnp.int32) % tm
    rows_in = jnp.where(asg >= 0, jnp.where(asg >= n, asg - n, asg), 0)
    rows_out = jnp.where(asg >= 0, asg, pad_row)
    rows_in_ext = jnp.concatenate([rows_in, jnp.zeros((tm,), jnp.int32)]).reshape(n_tiles + 1, 1, tm)
    rows_out_ext = jnp.concatenate([pad_row[:tm], rows_out]).reshape(n_tiles + 1, 1, tm)
    tile = jnp.arange(n_tiles, dtype=jnp.int32)
    last = tile_end[-1] - 1
    tile_c = jnp.minimum(tile, last)
    te = jnp.minimum(jnp.sum((tile_c[:, None] >= tile_end[None, :]).astype(jnp.int32), axis=1), n_exp - 1)
    te_hot = (te[:, None] == jnp.arange(n_exp, dtype=jnp.int32)[None, :]).astype(jnp.int32)
    cnt_t = jnp.sum(te_hot * counts[None, :], axis=1)
    start_t = jnp.sum(te_hot * tile_start[None, :], axis=1)
    nv = jnp.where(tile <= last, jnp.clip(cnt_t - (tile - start_t) * tm, 0, tm), 0).astype(jnp.int32)

    first = ((tile == start_t) & (nv > 0)).astype(jnp.int32)

    def chunk(i, c, nv_ref, first_ref):
        return jnp.where((nv_ref[i] > 0) & (first_ref[i] > 0), c, nc - 1)

    smem_rows = lambda index: pl.BlockSpec((1, 1, tm), index, memory_space=pltpu.SMEM)
    return pl.pallas_call(
        functools.partial(_moe_body, tm=tm, rows_per_step=tm // nc),
        grid_spec=pltpu.PrefetchScalarGridSpec(
            num_scalar_prefetch=4,
            grid=(n_tiles, nc),
            in_specs=[smem_rows(lambda i, c, te_ref, nv_ref, first_ref, last_ref: (0, 0, 0)),
                      smem_rows(lambda i, c, te_ref, nv_ref, first_ref, last_ref: (i + 1, 0, 0)),
                      smem_rows(lambda i, c, te_ref, nv_ref, first_ref, last_ref: (i, 0, 0)),
                      smem_rows(lambda i, c, te_ref, nv_ref, first_ref, last_ref: (i + 1, 0, 0)),
                      pl.BlockSpec(memory_space=pl.ANY),
                      pl.BlockSpec((1, d), lambda i, c, te_ref, nv_ref, first_ref, last_ref: (0, 0)),
                      pl.BlockSpec((None, d, tf), lambda i, c, te_ref, nv_ref, first_ref, last_ref:
                                   (te_ref[i], 0, chunk(i, c, nv_ref, first_ref))),
                      pl.BlockSpec((None, d, tf), lambda i, c, te_ref, nv_ref, first_ref, last_ref:
                                   (te_ref[i], 0, chunk(i, c, nv_ref, first_ref) + nc)),
                      pl.BlockSpec((None, tf, d), lambda i, c, te_ref, nv_ref, first_ref, last_ref:
                                   (te_ref[i], chunk(i, c, nv_ref, first_ref), 0))],
            out_specs=pl.BlockSpec(memory_space=pl.ANY),
            scratch_shapes=[pltpu.VMEM((2, tm, d), F32), pltpu.VMEM((tm, d), BF16), pltpu.VMEM((2, tm, d), F32),
                            pltpu.VMEM((nc, d, tf), BF16), pltpu.VMEM((nc, d, tf), BF16), pltpu.VMEM((nc, tf, d), BF16),
                            pltpu.SemaphoreType.DMA, pltpu.SemaphoreType.DMA]),
        out_shape=jax.ShapeDtypeStruct((n_asg + tm, d), F32),
        compiler_params=_params("arbitrary", "arbitrary"),
        name="moe_experts",
    )(te.astype(jnp.int32), nv, first, last.reshape(1).astype(jnp.int32), rows_in_ext, rows_in_ext, rows_out_ext,
      rows_out_ext, x, g, w_gu, w_gu, w_down)


def _combine_body(x_ref, y0_ref, y1_ref, wts_ref, g_ref, outp_ref, outs_ref, *, npt):
    wts = wts_ref[...]
    moe = wts[:, 0:1] * y0_ref[...] + wts[:, 1:2] * y1_ref[...]
    _store_split(pl.program_id(0) < npt, outp_ref, outs_ref, _rms(x_ref[...] + moe, g_ref[...]))


def _combine(x, y2, wts, g, n_p):
    n, d = x.shape
    n_s = n - n_p
    tm = _pick_tile(math.gcd(n_p, n_s), 512)
    nt = n // tm
    npt = n_p // tm
    row = lambda i: (i, 0)
    rp, rs = _split_maps(npt)
    return pl.pallas_call(
        functools.partial(_combine_body, npt=npt),
        grid=(nt,),
        in_specs=[pl.BlockSpec((tm, d), row), pl.BlockSpec((tm, d), row), pl.BlockSpec((tm, d), lambda i: (i + nt, 0)),
                  pl.BlockSpec((tm, LANES), row), pl.BlockSpec((1, d), lambda i: (0, 0))],
        out_specs=[pl.BlockSpec((tm, d), rp), pl.BlockSpec((tm, d), rs)],
        out_shape=[jax.ShapeDtypeStruct((n_p, d), F32), jax.ShapeDtypeStruct((n_s, d), F32)],
        compiler_params=_params("arbitrary"),
        name="moe_combine",
    )(x, y2, y2, wts, g)


def _rope_tables(pos, hd):
    half = hd // 2
    inv = ROPE_THETA ** (-jnp.arange(half, dtype=F32) / half)
    ang = pos.astype(F32)[:, None] * inv[None, :]
    cos = jnp.cos(ang)
    sin = jnp.sin(ang)
    reps = LANES // hd
    return jnp.tile(jnp.concatenate([cos, cos], axis=1), (1, reps)), jnp.tile(jnp.concatenate([-sin, sin], axis=1), (1, reps))


def kernel(x_prompt, x_sample, cache_k_win, cache_v_win, state_hgrn, norm_mix, norm_ffn, norm_final,
           w_qkv, b_qkv, w_o_attn, b_o_attn, sinks, w_in_hg, hg_lower, hg_norm, w_o_hg,
           w_gu_dense, w_down_dense, w_router, b_router, w_gu_moe, w_down_moe):
    batch, seq, d = x_prompt.shape
    n_seq, t_dec, _ = x_sample.shape
    n_kv, hd = cache_k_win.shape[3], cache_k_win.shape[4]
    n_heads = sinks.shape[1]
    g_per = n_heads // n_kv
    n_q, n_k = n_heads * hd, n_kv * hd
    n_p, n_s = batch * seq, n_seq * t_dec
    n_exp = w_router.shape[2]
    wc = cache_k_win.shape[2]
    assert norm_mix.shape[0] == 2 and hd * 2 == LANES and d % HG_DK == 0 and seq % WINDOW == 0 and wc == WINDOW

    tile_q = _pick_tile(math.gcd(n_p, n_s), 512)
    assert tile_q % t_dec == 0
    rope_p = _rope_tables(jnp.arange(seq), hd)
    rope_s = _rope_tables(jnp.tile(PAST_LEN + jnp.arange(t_dec), tile_q // t_dec), hd)
    x, q, k, v = _qkv_rope(x_prompt.reshape(n_p, d), x_sample.reshape(n_s, d), norm_mix[0:1], w_qkv[0], b_qkv[0:1],
                           rope_p, rope_s, n_q, n_k, hd)
    o = _swa_prompt(q, k, v, sinks[0], jnp.zeros((n_p + n_s, n_q), BF16), batch, seq, n_kv, g_per, hd)
    kc = cache_k_win[0].reshape(n_seq, wc, n_k)
    vc = cache_v_win[0].reshape(n_seq, wc, n_k)
    o = _swa_sample(q, k, v, kc, vc, sinks[0], o, n_p, n_seq, t_dec, n_kv, g_per, hd)
    last_win = lambda a: jnp.stack([a[(b + 1) * seq - wc:(b + 1) * seq] for b in range(batch)]).reshape(batch, wc, n_kv, hd)
    k_win_p, v_win_p = last_win(k), last_win(v)
    k_win_s = jnp.concatenate([cache_k_win[0][:, t_dec:], k[n_p:].reshape(n_seq, t_dec, n_kv, hd)], axis=1)
    v_win_s = jnp.concatenate([cache_v_win[0][:, t_dec:], v[n_p:].reshape(n_seq, t_dec, n_kv, hd)], axis=1)
    x = _oproj_ffn(o, x, w_o_attn[0], b_o_attn[0:1], norm_ffn[0:1], w_gu_dense[0], w_down_dense[0])

    lb_sm = jax.nn.softmax(hg_lower.astype(F32), axis=0)
    lb = (jnp.cumsum(lb_sm, axis=0) - lb_sm[0])[1:2]
    in_half = functools.partial(_hg_inproj_half, x, norm_mix[1:2], w_in_hg[0], lb)
    hq, hlf, hk = in_half(0)
    hv, hgate = in_half(1)
    n_hh = d // HG_DK
    c_p = math.gcd(seq, HG_CHUNK)
    o_p, s_p = _hg_core(hq, hk, hlf, hv, jnp.zeros((batch, n_hh, HG_DK, HG_DK), F32),
                        c_p, 2 if seq % (2 * c_p) == 0 else 1, 2 if batch % 2 == 0 else 1, t=seq)
    c_s = SUBLANES * ((t_dec + SUBLANES - 1) // SUBLANES)
    pad = lambda a: jnp.pad(a[n_p:].reshape(n_seq, t_dec, d), ((0, 0), (0, c_s - t_dec), (0, 0)))
    o_s, s_s = _hg_core(pad(hq), pad(hk), pad(hlf), pad(hv), state_hgrn[0], c_s, 1, 8 if n_seq % 8 == 0 else 1)
    o_s = o_s[:, :t_dec].reshape(n_s, d)

    wr = jnp.pad(w_router[0], ((0, 0), (0, LANES - n_exp)))
    br = jnp.pad(b_router[0:1], ((0, 0), (0, LANES - n_exp)))
    x3, ids, wts = _hg_out_router(o_p.reshape(n_p, d), o_s, hgate, x, hg_norm[0:1], w_o_hg[0], norm_ffn[1:2],
                                  wr, br, n_exp)
    y2 = _moe_experts(x3, norm_ffn[1:2], ids[:, :TOP_K], w_gu_moe[0], w_down_moe[0])
    y_p, y_s = _combine(x3, y2, wts, norm_final.reshape(1, d), n_p)

    return (y_p.reshape(batch, seq, d), y_s.reshape(n_seq, t_dec, d),
            k_win_p[None], v_win_p[None], k_win_s[None], v_win_s[None], s_p[None], s_s[None])
```

```python
import functools
import math

import jax
import jax.numpy as jnp
from jax import lax
from jax.experimental import pallas as pl
from jax.experimental.pallas import tpu as pltpu

F32 = jnp.float32
BF16 = jnp.bfloat16

NORM_EPS = 1e-5
WINDOW = 128
PAST_LEN = 16384
ROPE_THETA = 10000.0
HG_DK = 128
HG_CHUNK = 256
HG_GROUP = 2
HG_SAFE_DECAY = 60.0
HG_SAFE_Q = 1e9
TOP_K = 2
LANES = 128
SUBLANES = 8
VMEM_LIMIT = 56 * 1024 * 1024

NT_DIMS = (((1,), (1,)), ((), ()))
TN_DIMS = (((0,), (0,)), ((), ()))


def _pick_tile(n, target):
    for t in (1536, 1024, 768, 512, 384, 256, 192, 128, 64, 32, 16, 8):
        if t <= target and n % t == 0:
            return t
    raise ValueError(f"no row tile for {n}")


def _params(*sem):
    return pltpu.CompilerParams(dimension_semantics=sem, vmem_limit_bytes=VMEM_LIMIT)


def _rms(x, g):
    return x * lax.rsqrt(jnp.mean(x * x, axis=-1, keepdims=True) + NORM_EPS) * g


def _split_maps(npt):
    return (lambda i, *_: (jnp.minimum(i, npt - 1), 0)), (lambda i, *_: (jnp.maximum(i - npt, 0), 0))


def _store_split(is_prompt, ref_p, ref_s, val):
    @pl.when(is_prompt)
    def _():
        ref_p[...] = val

    @pl.when(jnp.logical_not(is_prompt))
    def _():
        ref_s[...] = val


def _qkv_body(xp_ref, xs_ref, g_ref, w_ref, b_ref, cosp_ref, sinp_ref, coss_ref, sins_ref,
              x_ref, q_ref, k_ref, v_ref, wbf_ref,
              *, npt, n_q, n_k, hd):
    i = pl.program_id(0)

    @pl.when(i == 0)
    def _():
        wbf_ref[...] = w_ref[...].astype(BF16)

    x = jnp.where(i < npt, xp_ref[...], xs_ref[...])
    x_ref[...] = x
    h = _rms(x, g_ref[...]).astype(BF16)
    y = jnp.dot(h, wbf_ref[...], preferred_element_type=F32) + b_ref[...]
    cos = jnp.where(i < npt, cosp_ref[...], coss_ref[...])
    sin = jnp.where(i < npt, sinp_ref[...], sins_ref[...])
    lane = lax.broadcasted_iota(jnp.int32, cos.shape, 1)
    first = (lane % hd) < (hd // 2)

    def rope(blk):
        partner = jnp.where(first, pltpu.roll(blk, LANES - hd // 2, 1), pltpu.roll(blk, hd // 2, 1))
        return blk * cos + partner * sin

    scale = hd ** -0.5
    for j in range(n_q // LANES):
        q_ref[:, j * LANES:(j + 1) * LANES] = (rope(y[:, j * LANES:(j + 1) * LANES]) * scale).astype(BF16)
    for j in range(n_k // LANES):
        k_ref[:, j * LANES:(j + 1) * LANES] = rope(y[:, n_q + j * LANES:n_q + (j + 1) * LANES])
    v_ref[...] = y[:, n_q + n_k:]


def _qkv_rope(xp, xs, g, w, b, rope_p, rope_s, n_q, n_k, hd):
    (n_p, d), n_s = xp.shape, xs.shape[0]
    n = n_p + n_s
    n_out = w.shape[1]
    tm = _pick_tile(math.gcd(n_p, n_s), 512)
    npt = n_p // tm
    row = lambda i: (i, 0)
    fix = lambda i: (0, 0)
    rp, rs = _split_maps(npt)
    seq_tiles = rope_p[0].shape[0] // tm
    assert rope_p[0].shape[0] % tm == 0 and rope_s[0].shape[0] == tm
    pos_p = lambda i: (jnp.minimum(i, npt - 1) % seq_tiles, 0)
    return pl.pallas_call(
        functools.partial(_qkv_body, npt=npt, n_q=n_q, n_k=n_k, hd=hd),
        grid=(n // tm,),
        in_specs=[pl.BlockSpec((tm, d), rp), pl.BlockSpec((tm, d), rs), pl.BlockSpec((1, d), fix),
                  pl.BlockSpec((d, n_out), fix), pl.BlockSpec((1, n_out), fix),
                  pl.BlockSpec((tm, LANES), pos_p), pl.BlockSpec((tm, LANES), pos_p),
                  pl.BlockSpec((tm, LANES), fix), pl.BlockSpec((tm, LANES), fix)],
        out_specs=[pl.BlockSpec((tm, d), row), pl.BlockSpec((tm, n_q), row), pl.BlockSpec((tm, n_k), row),
                   pl.BlockSpec((tm, n_k), row)],
        out_shape=[jax.ShapeDtypeStruct((n, d), F32), jax.ShapeDtypeStruct((n, n_q), BF16),
                   jax.ShapeDtypeStruct((n, n_k), F32), jax.ShapeDtypeStruct((n, n_k), F32)],
        scratch_shapes=[pltpu.VMEM((d, n_out), BF16)],
        compiler_params=_params("arbitrary"),
        name="qkv_rope",
    )(xp, xs, g, w, b, *rope_p, *rope_s)


def _sink_column(sink_ref, kh, g_per, rows_per):
    blk = lax.broadcasted_iota(jnp.int32, (g_per * rows_per, 1), 0) // rows_per
    col = jnp.full((g_per * rows_per, 1), sink_ref[kh * g_per], F32)
    for g in range(1, g_per):
        col = jnp.where(blk == g, sink_ref[kh * g_per + g], col)
    return col


def _stack_heads(q, kh, g_per, hd):
    return jnp.concatenate([q[:, (kh * g_per + g) * hd:(kh * g_per + g + 1) * hd] for g in range(g_per)], axis=0)


def _swa_prompt_body(sink_ref, q_ref, kp_ref, kc_ref, vp_ref, vc_ref, o_all_ref, o_ref, *, n_kv, g_per, hd, w):
    del o_all_ref
    n = pl.program_id(1)
    q = q_ref[...]
    kk = jnp.concatenate([kp_ref[...], kc_ref[...]], axis=0).astype(BF16)
    vv = jnp.concatenate([vp_ref[...], vc_ref[...]], axis=0).astype(BF16)
    i = lax.broadcasted_iota(jnp.int32, (w, 2 * w), 0)
    j = lax.broadcasted_iota(jnp.int32, (w, 2 * w), 1)
    mask = (j > i) & (j <= i + w) & ((j >= w) | (n > 0))
    outs = []
    for h in range(n_kv * g_per):
        ks = slice((h // g_per) * hd, (h // g_per + 1) * hd)
        s = lax.dot_general(q[:, h * hd:(h + 1) * hd], kk[:, ks], NT_DIMS, preferred_element_type=F32)
        s = jnp.where(mask, s, -jnp.inf)
        sink = sink_ref[h]
        m = jnp.maximum(jnp.max(s, axis=-1, keepdims=True), sink)
        p = jnp.exp(s - m)
        p = p / (jnp.sum(p, axis=-1, keepdims=True) + jnp.exp(sink - m))
        outs.append(jnp.dot(p.astype(BF16), vv[:, ks], preferred_element_type=F32))
    o_ref[...] = jnp.concatenate(outs, axis=1).astype(BF16)


def _swa_prompt(q, k, v, sinks, o_all, batch, seq, n_kv, g_per, hd):
    w = WINDOW
    nb = seq // w
    dq = q.shape[1]
    dkv = k.shape[1]
    cur = lambda b, n: (b * nb + n, 0)
    prev = lambda b, n: (b * nb + jnp.maximum(n - 1, 0), 0)
    return pl.pallas_call(
        functools.partial(_swa_prompt_body, n_kv=n_kv, g_per=g_per, hd=hd, w=w),
        grid=(batch, nb),
        in_specs=[pl.BlockSpec(memory_space=pltpu.SMEM),
                  pl.BlockSpec((w, dq), cur), pl.BlockSpec((w, dkv), prev), pl.BlockSpec((w, dkv), cur),
                  pl.BlockSpec((w, dkv), prev), pl.BlockSpec((w, dkv), cur),
                  pl.BlockSpec(memory_space=pl.ANY)],
        out_specs=pl.BlockSpec((w, dq), cur),
        out_shape=jax.ShapeDtypeStruct(o_all.shape, o_all.dtype),
        input_output_aliases={6: 0},
        compiler_params=_params("arbitrary", "arbitrary"),
        name="swa_prompt",
    )(sinks, q, k, k, v, v, o_all)


def _swa_sample_body(sink_ref, q_ref, kn_ref, vn_ref, kc_ref, vc_ref, o_all_ref, o_ref, *, n_kv, g_per, hd, bt, t, wc):
    del o_all_ref
    r = bt * t
    q = q_ref[...]
    kn = kn_ref[...].astype(BF16)
    vn = vn_ref[...].astype(BF16)
    kc = kc_ref[...].reshape(bt * wc, n_kv * hd).astype(BF16)
    vc = vc_ref[...].reshape(bt * wc, n_kv * hd).astype(BF16)
    row_c = lax.broadcasted_iota(jnp.int32, (g_per * r, bt * wc), 0) % r
    col_c = lax.broadcasted_iota(jnp.int32, (g_per * r, bt * wc), 1)
    mask_c = (col_c // wc == row_c // t) & (col_c % wc > row_c % t + (wc - WINDOW))
    row_n = lax.broadcasted_iota(jnp.int32, (g_per * r, r), 0) % r
    col_n = lax.broadcasted_iota(jnp.int32, (g_per * r, r), 1)
    mask_n = (col_n // t == row_n // t) & (col_n % t <= row_n % t)
    outs = []
    for kh in range(n_kv):
        hs = slice(kh * hd, (kh + 1) * hd)
        q4 = _stack_heads(q, kh, g_per, hd)
        sc = jnp.where(mask_c, lax.dot_general(q4, kc[:, hs], NT_DIMS, preferred_element_type=F32), -jnp.inf)
        sn = jnp.where(mask_n, lax.dot_general(q4, kn[:, hs], NT_DIMS, preferred_element_type=F32), -jnp.inf)
        sink = _sink_column(sink_ref, kh, g_per, r)
        m = jnp.maximum(jnp.maximum(jnp.max(sc, axis=-1, keepdims=True), jnp.max(sn, axis=-1, keepdims=True)), sink)
        pc = jnp.exp(sc - m)
        pn = jnp.exp(sn - m)
        den = jnp.sum(pc, axis=-1, keepdims=True) + jnp.sum(pn, axis=-1, keepdims=True) + jnp.exp(sink - m)
        o4 = (jnp.dot((pc / den).astype(BF16), vc[:, hs], preferred_element_type=F32)
              + jnp.dot((pn / den).astype(BF16), vn[:, hs], preferred_element_type=F32))
        outs.append(jnp.concatenate([o4[g * r:(g + 1) * r] for g in range(g_per)], axis=1))
    o_ref[...] = jnp.concatenate(outs, axis=1).astype(BF16)


def _swa_sample(q, k, v, k_cache, v_cache, sinks, o_all, row0, n_seq, t, n_kv, g_per, hd):
    wc = k_cache.shape[1]
    dq = q.shape[1]
    dkv = k.shape[1]
    bt = 8 if n_seq % 8 == 0 else n_seq
    r = bt * t
    assert row0 % r == 0
    off = row0 // r
    rows = lambda i: (off + i, 0)
    return pl.pallas_call(
        functools.partial(_swa_sample_body, n_kv=n_kv, g_per=g_per, hd=hd, bt=bt, t=t, wc=wc),
        grid=(n_seq // bt,),
        in_specs=[pl.BlockSpec(memory_space=pltpu.SMEM),
                  pl.BlockSpec((r, dq), rows), pl.BlockSpec((r, dkv), rows), pl.BlockSpec((r, dkv), rows),
                  pl.BlockSpec((bt, wc, dkv), lambda i: (i, 0, 0)), pl.BlockSpec((bt, wc, dkv), lambda i: (i, 0, 0)),
                  pl.BlockSpec(memory_space=pl.ANY)],
        out_specs=pl.BlockSpec((r, dq), rows),
        out_shape=jax.ShapeDtypeStruct(o_all.shape, o_all.dtype),
        input_output_aliases={6: 0},
        compiler_params=_params("arbitrary"),
        name="swa_sample",
    )(sinks, q, k, v, k_cache, v_cache, o_all)


def _swiglu_step(h, wg, wu, wd, acc_ref):
    a = jnp.dot(h, wg, preferred_element_type=F32)
    b = jnp.dot(h, wu, preferred_element_type=F32)
    act = (a * jax.nn.sigmoid(a) * b).astype(BF16)
    acc_ref[...] += jnp.dot(act, wd, preferred_element_type=F32)


def _oproj_ffn_body(o_ref, x_ref, wo_ref, bo_ref, g_ref, wg_ref, wu_ref, wd_ref, out_ref,
                    wo_bf, x1_ref, h_ref, acc_ref):
    i = pl.program_id(0)
    c = pl.program_id(1)

    @pl.when((i == 0) & (c == 0))
    def _():
        wo_bf[...] = wo_ref[...].astype(BF16)

    @pl.when(c == 0)
    def _():
        x1 = x_ref[...] + jnp.dot(o_ref[...], wo_bf[...], preferred_element_type=F32) + bo_ref[...]
        x1_ref[...] = x1
        h_ref[...] = _rms(x1, g_ref[...]).astype(BF16)
        acc_ref[...] = jnp.zeros_like(acc_ref)

    _swiglu_step(h_ref[...], wg_ref[...].astype(BF16), wu_ref[...].astype(BF16), wd_ref[...].astype(BF16), acc_ref)

    @pl.when(c == pl.num_programs(1) - 1)
    def _():
        out_ref[...] = x1_ref[...] + acc_ref[...]


def _ff_chunk(d_ff):
    for tf in (512, 256, 128):
        if d_ff % tf == 0:
            return tf
    raise ValueError(f"d_ff {d_ff} is not a multiple of {LANES}")


def _oproj_ffn(o, x, wo, bo, g, w_gu, w_down):
    n, d = x.shape
    d_ff = w_down.shape[0]
    tf = _ff_chunk(d_ff)
    nc = d_ff // tf
    tm = _pick_tile(n, 768)
    row = lambda i, c: (i, 0)
    fix = lambda i, c: (0, 0)
    return pl.pallas_call(
        _oproj_ffn_body,
        grid=(n // tm, nc),
        in_specs=[pl.BlockSpec((tm, o.shape[1]), row), pl.BlockSpec((tm, d), row),
                  pl.BlockSpec(wo.shape, fix), pl.BlockSpec((1, d), fix), pl.BlockSpec((1, d), fix),
                  pl.BlockSpec((d, tf), lambda i, c: (0, c)), pl.BlockSpec((d, tf), lambda i, c: (0, c + nc)),
                  pl.BlockSpec((tf, d), lambda i, c: (c, 0))],
        out_specs=pl.BlockSpec((tm, d), row),
        out_shape=jax.ShapeDtypeStruct((n, d), F32),
        scratch_shapes=[pltpu.VMEM(wo.shape, BF16), pltpu.VMEM((tm, d), F32), pltpu.VMEM((tm, d), BF16),
                        pltpu.VMEM((tm, d), F32)],
        compiler_params=_params("arbitrary", "arbitrary"),
        name="oproj_ffn",
    )(o, x, wo, bo, g, w_gu, w_gu, w_down)


def _hg_in_body(x_ref, g_ref, w_ref, lb_ref, *refs, half):
    out_refs, wbf_ref = refs[:-1], refs[-1]
    d = x_ref.shape[1]

    @pl.when(pl.program_id(0) == 0)
    def _():
        wbf_ref[...] = w_ref[...].astype(BF16)

    h = _rms(x_ref[...], g_ref[...]).astype(BF16)
    z = jnp.dot(h, wbf_ref[...], preferred_element_type=F32)
    z0, z1 = z[:, :d], z[:, d:]
    if half == 0:
        out_refs[0][...] = z0 * jax.nn.sigmoid(z0) * (HG_DK ** -0.5)
        lb = lb_ref[...]
        t = jnp.exp(-jnp.abs(z1))
        log_sig = jnp.minimum(z1, 0.0) - jnp.log(1.0 + t)
        a = jnp.log(lb)
        b = jnp.log1p(-lb) + log_sig
        out_refs[1][...] = jnp.maximum(a, b) + jnp.log(1.0 + jnp.exp(-jnp.abs(a - b)))
        out_refs[2][...] = (1.0 - lb) * (jnp.where(z1 >= 0.0, t, 1.0) / (1.0 + t))
    else:
        out_refs[0][...] = z0
        out_refs[1][...] = z1 * jax.nn.sigmoid(z1)


def _hg_inproj_half(x, g, w_in, lb, half):
    n, d = x.shape
    n_res = 3 if half == 0 else 2
    tm = _pick_tile(n, 512)
    row = lambda i: (i, 0)
    fix = lambda i: (0, 0)
    return pl.pallas_call(
        functools.partial(_hg_in_body, half=half),
        grid=(n // tm,),
        in_specs=[pl.BlockSpec((tm, d), row), pl.BlockSpec((1, d), fix), pl.BlockSpec((d, 2 * d), lambda i: (0, half)),
                  pl.BlockSpec((1, d), fix)],
        out_specs=[pl.BlockSpec((tm, d), row)] * n_res,
        out_shape=[jax.ShapeDtypeStruct((n, d), F32)] * n_res,
        scratch_shapes=[pltpu.VMEM((d, 2 * d), BF16)],
        compiler_params=_params("arbitrary"),
        name=f"hgrn_inproj_{half}",
    )(x, g, w_in, lb)


def _hg_core_body(*refs, c_len, n_chunk, n_seq, n_heads, carried, per_seq_inputs, gsz):
    n_in = n_seq if per_seq_inputs else 1
    in_refs, (s0_ref, o_ref, sout_ref, st_ref, g_ref, oi_ref) = refs[:4 * n_in], refs[4 * n_in:]

    def rd(kind, s, row_slice):
        if per_seq_inputs:
            return in_refs[kind * n_in + s][row_slice, :]
        return in_refs[kind][s, row_slice, :]

    c = pl.program_id(1)
    dk = HG_DK
    rb = c_len * n_chunk
    units = [(s, j) for s in range(n_seq) for j in range(n_chunk)]

    n_pairs = n_heads // gsz
    gw = gsz * dk
    state_diag = (lax.broadcasted_iota(jnp.int32, (gw, gw), 0) // dk == lax.broadcasted_iota(jnp.int32, (gw, gw), 1) // dk)
    rows_diag = (lax.broadcasted_iota(jnp.int32, (gsz * c_len, gw), 0) // c_len
                 == lax.broadcasted_iota(jnp.int32, (gsz * c_len, gw), 1) // dk)

    def pair_tile(x):
        x2 = jnp.concatenate([x] * gsz, axis=0)
        return jnp.where(rows_diag, x2, jnp.zeros_like(x2))

    def diag(g):
        return slice(g * dk, (g + 1) * dk)

    if carried:
        @pl.when(c == 0)
        def _():
            for s in range(n_seq):
                for p in range(n_pairs):
                    st_ref[s, p] = jnp.zeros((gw, gw), F32)
                    for g in range(gsz):
                        st_ref[s, p, diag(g), diag(g)] = s0_ref[s, gsz * p + g].T

    r = lax.broadcasted_iota(jnp.int32, (rb, rb), 0)
    cidx = lax.broadcasted_iota(jnp.int32, (rb, rb), 1)
    block_causal = ((r >= cidx) & (r // c_len == cidx // c_len)).astype(BF16)
    causal2 = (lax.broadcasted_iota(jnp.int32, (c_len, gsz * c_len), 0)
               >= lax.broadcasted_iota(jnp.int32, (c_len, gsz * c_len), 1) % c_len)

    def rows(j):
        return slice(j * c_len, (j + 1) * c_len)

    gcum, safe = [], None
    for s in range(n_seq):
        lf = rd(2, s, slice(None))
        lf_hi = lf.astype(BF16)
        rest = lf - lf_hi.astype(F32)
        lf_mid = rest.astype(BF16)
        lf_lo = (rest - lf_mid.astype(F32)).astype(BF16)
        gs = ((jnp.dot(block_causal, lf_lo, preferred_element_type=F32)
               + jnp.dot(block_causal, lf_mid, preferred_element_type=F32))
              + jnp.dot(block_causal, lf_hi, preferred_element_type=F32))
        gcum.append(gs)
        for j in range(n_chunk):
            mid = gs[j * c_len + c_len // 2 - 1:j * c_len + c_len // 2, :]
            ok = ((jnp.max(gs[j * c_len:j * c_len + 1, :] - mid) <= HG_SAFE_DECAY)
                  & (jnp.max(mid - gs[(j + 1) * c_len - 1:(j + 1) * c_len, :]) <= HG_SAFE_DECAY)
                  & (jnp.max(jnp.abs(rd(0, s, rows(j)))) <= HG_SAFE_Q))
            safe = ok if safe is None else (safe & ok)

    def g_mid(s, j):
        return gcum[s][j * c_len + c_len // 2 - 1:j * c_len + c_len // 2, :]

    def g_last(s, j):
        return gcum[s][(j + 1) * c_len - 1:(j + 1) * c_len, :]

    @pl.when(safe)
    def _():
        for u, (s, j) in enumerate(units):
            g = gcum[s][rows(j), :]
            qi = (rd(0, s, rows(j)) * jnp.exp(g - g_mid(s, j))).astype(BF16)
            ki = (rd(1, s, rows(j)) * jnp.exp(g_mid(s, j) - g)).astype(BF16)
            vb = rd(3, s, rows(j)).astype(BF16)
            for p in range(n_pairs):
                ps = slice(p * gw, (p + 1) * gw)
                a = lax.dot_general(qi[:, ps], pair_tile(ki[:, ps]), NT_DIMS, preferred_element_type=F32)
                a = jnp.where(causal2, a, 0.0).astype(BF16)
                oi_ref[u, :, ps] = jnp.dot(a, pair_tile(vb[:, ps]), preferred_element_type=F32)

    @pl.when(jnp.logical_not(safe))
    def _():
        lane_h = lax.broadcasted_iota(jnp.int32, (n_heads * dk, n_heads * dk), 0) // dk
        lane_w = lax.broadcasted_iota(jnp.int32, (n_heads * dk, n_heads * dk), 1) // dk
        head_sum = (lane_h == lane_w).astype(BF16)
        t_idx = lax.broadcasted_iota(jnp.int32, (c_len, 1), 0)
        for u, (s, j) in enumerate(units):
            g = gcum[s][rows(j), :]
            g_ref[...] = g
            q = rd(0, s, rows(j))

            def key_row(i, acc, s=s, j=j, g=g, q=q):
                gi = g_ref[pl.ds(i, 1), :]
                decay = jnp.exp(jnp.where(t_idx >= i, g - gi, -jnp.inf))
                term = (q * decay * rd(1, s, pl.ds(j * c_len + i, 1))).astype(BF16)
                a_i = jnp.dot(term, head_sum, preferred_element_type=F32)
                return acc + a_i * rd(3, s, pl.ds(j * c_len + i, 1))

            oi_ref[u] = lax.fori_loop(0, c_len, key_row, jnp.zeros((c_len, n_heads * dk), F32))

    if not carried:
        e_rows = [jnp.exp(g_last(s, 0)) for s in range(n_seq)]
        e_cols = jnp.concatenate(e_rows + [jnp.zeros((LANES - n_seq, n_heads * dk), F32)], axis=0).T
        zero = jnp.zeros((dk, dk), F32)

    for u, (s, j) in enumerate(units):
        g = gcum[s][rows(j), :]
        qs = (rd(0, s, rows(j)) * jnp.exp(g)).astype(BF16)
        ks = (rd(1, s, rows(j)) * jnp.exp(g_last(s, j) - g)).astype(BF16)
        vb = rd(3, s, rows(j)).astype(BF16)
        e_last = jnp.exp(g_last(s, j))
        for p in range(n_pairs):
            ps = slice(p * gw, (p + 1) * gw)
            if carried:
                st = st_ref[s, p]
                o_ref[s, rows(j), ps] = oi_ref[u, :, ps] + lax.dot_general(qs[:, ps], st.astype(BF16), NT_DIMS,
                                                                           preferred_element_type=F32)
                upd = lax.dot_general(vb[:, ps], ks[:, ps], TN_DIMS, preferred_element_type=F32)
                st_ref[s, p] = jnp.where(state_diag, e_last[:, ps] * st + upd, 0.0)
            else:
                st = jnp.concatenate([jnp.concatenate([zero] * g + [s0_ref[s, gsz * p + g]] + [zero] * (gsz - 1 - g), axis=1)
                                      for g in range(gsz)], axis=0)
                o_ref[s, rows(j), ps] = oi_ref[u, :, ps] + jnp.dot(qs[:, ps], st.astype(BF16),
                                                                   preferred_element_type=F32)
                upd = lax.dot_general(ks[:, ps], vb[:, ps], TN_DIMS, preferred_element_type=F32)
                new = e_cols[ps, s:s + 1] * st + upd
                for g in range(gsz):
                    sout_ref[s, gsz * p + g] = new[diag(g), diag(g)]

    if carried:
        @pl.when(c == pl.num_programs(1) - 1)
        def _():
            for s in range(n_seq):
                for p in range(n_pairs):
                    for g in range(gsz):
                        sout_ref[s, gsz * p + g] = st_ref[s, p, diag(g), diag(g)].T


def _hg_core(q, k, lf, v, s0, c_len, n_chunk, n_seq, t=None):
    batch = s0.shape[0]
    per_seq_inputs = q.ndim == 2
    t = t if per_seq_inputs else q.shape[1]
    d = q.shape[-1]
    n_heads = d // HG_DK
    rb = c_len * n_chunk
    nc = t // rb
    carried = t > c_len
    gsz = HG_GROUP if carried and n_heads % HG_GROUP == 0 else 2
    assert t % rb == 0 and batch % n_seq == 0 and n_heads % gsz == 0 and n_seq <= LANES
    rows = lambda b, c: (b, c, 0)
    state = lambda b, c: (b, 0, 0, 0)
    if per_seq_inputs:
        seq_rows = lambda s: (lambda b, c: ((b * n_seq + s) * nc + c, 0))
        in_specs = [pl.BlockSpec((rb, d), seq_rows(s)) for _ in range(4) for s in range(n_seq)]
        inputs = [a for a in (q, k, lf, v) for _ in range(n_seq)]
    else:
        in_specs = [pl.BlockSpec((n_seq, rb, d), rows)] * 4
        inputs = [q, k, lf, v]
    return pl.pallas_call(
        functools.partial(_hg_core_body, c_len=c_len, n_chunk=n_chunk, n_seq=n_seq, n_heads=n_heads, carried=carried,
                          per_seq_inputs=per_seq_inputs, gsz=gsz),
        grid=(batch // n_seq, nc),
        in_specs=in_specs + [pl.BlockSpec((n_seq, n_heads, HG_DK, HG_DK), state)],
        out_specs=[pl.BlockSpec((n_seq, rb, d), rows), pl.BlockSpec((n_seq, n_heads, HG_DK, HG_DK), state)],
        out_shape=[jax.ShapeDtypeStruct((batch, t, d), F32),
                   jax.ShapeDtypeStruct((batch, n_heads, HG_DK, HG_DK), F32)],
        scratch_shapes=[pltpu.VMEM((n_seq if carried else 1, n_heads // gsz, gsz * HG_DK, gsz * HG_DK), F32),
                        pltpu.VMEM((c_len, d), F32),
                        pltpu.VMEM((n_seq * n_chunk, c_len, d), F32)],
        compiler_params=_params("arbitrary", "arbitrary"),
        name=f"hgrn_core_{c_len}",
    )(*inputs, s0)


def _hg_out_router_body(op_ref, os_ref, gate_ref, x_ref, gn_ref, wo_ref, g_ref, wr_ref, br_ref,
                        x3_ref, ids_ref, wts_ref, wo_bf, *, npt, n_heads, n_exp):
    i = pl.program_id(0)

    @pl.when(i == 0)
    def _():
        wo_bf[...] = wo_ref[...].astype(BF16)

    dk = HG_DK
    o = jnp.where(i < npt, op_ref[...], os_ref[...])
    gn = gn_ref[...]
    normed = jnp.concatenate([_rms(o[:, h * dk:(h + 1) * dk], gn) for h in range(n_heads)], axis=1)
    y = (normed * gate_ref[...]).astype(BF16)
    x3 = x_ref[...] + jnp.dot(y, wo_bf[...], preferred_element_type=F32)
    x3_ref[...] = x3
    h4 = _rms(x3, g_ref[...])
    h_hi = h4.astype(BF16)
    h_lo = (h4 - h_hi.astype(F32)).astype(BF16)
    wr = wr_ref[...]
    w_hi = wr.astype(BF16)
    w_lo = (wr - w_hi.astype(F32)).astype(BF16)
    logits = (jnp.dot(h_hi, w_hi, preferred_element_type=F32)
              + (jnp.dot(h_lo, w_hi, preferred_element_type=F32) + jnp.dot(h_hi, w_lo, preferred_element_type=F32))
              + br_ref[...])
    lane = lax.broadcasted_iota(jnp.int32, logits.shape, 1)
    logits = jnp.where(lane < n_exp, logits, -jnp.inf)
    m1 = jnp.max(logits, axis=-1, keepdims=True)
    i1 = jnp.min(jnp.where(logits == m1, lane, LANES), axis=-1, keepdims=True)
    rest = jnp.where(lane == i1, -jnp.inf, logits)
    m2 = jnp.max(rest, axis=-1, keepdims=True)
    i2 = jnp.min(jnp.where(rest == m2, lane, LANES), axis=-1, keepdims=True)
    e2 = jnp.exp(m2 - m1)
    den = 1.0 + e2
    ids_ref[...] = jnp.where(lane == 0, i1, jnp.where(lane == 1, i2, 0))
    wts_ref[...] = jnp.where(lane == 0, 1.0 / den, jnp.where(lane == 1, e2 / den, 0.0))


def _hg_out_router(o_p, o_s, gate, x, gn, wo, g, w_router, b_router, n_exp):
    n, d = x.shape
    n_p, n_s = o_p.shape[0], o_s.shape[0]
    tm = _pick_tile(math.gcd(n_p, n_s), 512)
    npt = n_p // tm
    row = lambda i: (i, 0)
    fix = lambda i: (0, 0)
    rp, rs = _split_maps(npt)
    return pl.pallas_call(
        functools.partial(_hg_out_router_body, npt=npt, n_heads=d // HG_DK, n_exp=n_exp),
        grid=(n // tm,),
        in_specs=[pl.BlockSpec((tm, d), rp), pl.BlockSpec((tm, d), rs), pl.BlockSpec((tm, d), row),
                  pl.BlockSpec((tm, d), row),
                  pl.BlockSpec((1, HG_DK), fix), pl.BlockSpec(wo.shape, fix), pl.BlockSpec((1, d), fix),
                  pl.BlockSpec((d, LANES), fix), pl.BlockSpec((1, LANES), fix)],
        out_specs=[pl.BlockSpec((tm, d), row), pl.BlockSpec((tm, LANES), row), pl.BlockSpec((tm, LANES), row)],
        out_shape=[jax.ShapeDtypeStruct((n, d), F32), jax.ShapeDtypeStruct((n, LANES), jnp.int32),
                   jax.ShapeDtypeStruct((n, LANES), F32)],
        scratch_shapes=[pltpu.VMEM(wo.shape, BF16)],
        compiler_params=_params("arbitrary"),
        name="hgrn_out_router",
    )(o_p, o_s, gate, x, gn, wo, g, w_router, b_router)


def _moe_body(te_ref, nv_ref, first_ref, last_ref, rin0_ref, rin_next_ref, rout_prev_ref, rout_cur_ref,
              x_hbm, g_ref, wg_ref, wu_ref, wd_ref, y_hbm,
              xg_ref, h_ref, acc_ref, wg_res, wu_res, wd_res, sem_in, sem_out, *, tm, rows_per_step):
    del te_ref
    i = pl.program_id(0)
    c = pl.program_id(1)
    slot = i % 2
    other = 1 - slot

    def row_in(buf, j, token):
        return pltpu.make_async_copy(x_hbm.at[pl.ds(token, 1)], xg_ref.at[buf, pl.ds(j, 1)], sem_in)

    def row_out(buf, j, dst):
        return pltpu.make_async_copy(acc_ref.at[buf, pl.ds(j, 1)], y_hbm.at[pl.ds(dst, 1)], sem_out)

    def all_in(buf):
        return pltpu.make_async_copy(x_hbm.at[pl.ds(0, tm)], xg_ref.at[buf], sem_in)

    def all_out(buf):
        return pltpu.make_async_copy(acc_ref.at[buf], y_hbm.at[pl.ds(0, tm)], sem_out)

    def each_row(fn):
        def body(j, carry):
            fn(j)
            return carry
        lax.fori_loop(0, tm, body, 0, unroll=8)

    @pl.when(nv_ref[i] > 0)
    def _():
        @pl.when((i == 0) & (c == 0))
        def _():
            acc_ref[1] = jnp.zeros(acc_ref.shape[1:], F32)
            each_row(lambda j: row_in(0, j, rin0_ref[0, 0, j]).start())
            all_in(0).wait()

        @pl.when(c == 0)
        def _():
            h_ref[...] = _rms(xg_ref[slot], g_ref[...]).astype(BF16)
            acc_ref[slot] = jnp.zeros(acc_ref.shape[1:], F32)

        def stream_rows():
            for u in range(rows_per_step):
                j = c * rows_per_step + u
                row_in(other, j, rin_next_ref[0, 0, j]).start()
                row_out(other, j, rout_prev_ref[0, 0, j]).start(priority=1)

        @pl.when(first_ref[i] > 0)
        def _():
            stream_rows()
            wg_res[c] = wg_ref[...].astype(BF16)
            wu_res[c] = wu_ref[...].astype(BF16)
            wd_res[c] = wd_ref[...].astype(BF16)
            _swiglu_step(h_ref[...], wg_res[c], wu_res[c], wd_res[c], acc_ref.at[slot])

        @pl.when(first_ref[i] == 0)
        def _():
            stream_rows()
            _swiglu_step(h_ref[...], wg_res[c], wu_res[c], wd_res[c], acc_ref.at[slot])

        @pl.when(c == pl.num_programs(1) - 1)
        def _():
            all_in(other).wait()
            all_out(other).wait()

            @pl.when(i == last_ref[0])
            def _():
                each_row(lambda j: row_out(slot, j, rout_cur_ref[0, 0, j]).start())
                all_out(slot).wait()


MOE_TILE_ROWS = 672
BF16_SUBLANES = 16


def _moe_experts(x, g, ids, w_gu, w_down):
    n, d = x.shape
    n_exp, d_ff = w_down.shape[0], w_down.shape[1]
    tf = _ff_chunk(d_ff)
    nc = d_ff // tf
    unit = BF16_SUBLANES * nc
    tm = unit * max(1, round(MOE_TILE_ROWS / unit))
    n_asg = n * TOP_K
    n_tiles = (n_asg + n_exp * (tm - 1) + tm - 1) // tm

    e_flat = ids.T.reshape(n_asg)
    onehot = (e_flat[:, None] == jnp.arange(n_exp, dtype=jnp.int32)[None, :]).astype(jnp.int32)
    counts = jnp.sum(onehot, axis=0)
    rank = jnp.sum((jnp.cumsum(onehot, axis=0) - onehot) * onehot, axis=1)
    tiles_per = (counts + tm - 1) // tm
    tile_end = jnp.cumsum(tiles_per)
    tile_start = tile_end - tiles_per
    slot = jnp.sum(onehot * tile_start[None, :], axis=1) * tm + rank
    asg = jnp.full((n_tiles * tm,), -1, jnp.int32).at[slot].set(jnp.arange(n_asg, dtype=jnp.int32),
                                                                 unique_indices=True)
    pad_row = n_asg + jnp.arange(n_tiles * tm, dtype=jnp.int32) % tm
    rows_in = jnp.where(asg >= 0, jnp.where(asg >= n, asg - n, asg), 0)
    rows_out = jnp.where(asg >= 0, asg, pad_row)
    rows_in_ext = jnp.concatenate([rows_in, jnp.zeros((tm,), jnp.int32)]).reshape(n_tiles + 1, 1, tm)
    rows_out_ext = jnp.concatenate([pad_row[:tm], rows_out]).reshape(n_tiles + 1, 1, tm)
    tile = jnp.arange(n_tiles, dtype=jnp.int32)
    last = tile_end[-1] - 1
    tile_c = jnp.minimum(tile, last)
    te = jnp.minimum(jnp.sum((tile_c[:, None] >= tile_end[None, :]).astype(jnp.int32), axis=1), n_exp - 1)
    te_hot = (te[:, None] == jnp.arange(n_exp, dtype=jnp.int32)[None, :]).astype(jnp.int32)
    cnt_t = jnp.sum(te_hot * counts[None, :], axis=1)
    start_t = jnp.sum(te_hot * tile_start[None, :], axis=1)
    nv = jnp.where(tile <= last, jnp.clip(cnt_t - (tile - start_t) * tm, 0, tm), 0).astype(jnp.int32)

    first = ((tile == start_t) & (nv > 0)).astype(jnp.int32)

    def chunk(i, c, nv_ref, first_ref):
        return jnp.where((nv_ref[i] > 0) & (first_ref[i] > 0), c, nc - 1)

    smem_rows = lambda index: pl.BlockSpec((1, 1, tm), index, memory_space=pltpu.SMEM)
    return pl.pallas_call(
        functools.partial(_moe_body, tm=tm, rows_per_step=tm // nc),
        grid_spec=pltpu.PrefetchScalarGridSpec(
            num_scalar_prefetch=4,
            grid=(n_tiles, nc),
            in_specs=[smem_rows(lambda i, c, te_ref, nv_ref, first_ref, last_ref: (0, 0, 0)),
                      smem_rows(lambda i, c, te_ref, nv_ref, first_ref, last_ref: (i + 1, 0, 0)),
                      smem_rows(lambda i, c, te_ref, nv_ref, first_ref, last_ref: (i, 0, 0)),
                      smem_rows(lambda i, c, te_ref, nv_ref, first_ref, last_ref: (i + 1, 0, 0)),
                      pl.BlockSpec(memory_space=pl.ANY),
                      pl.BlockSpec((1, d), lambda i, c, te_ref, nv_ref, first_ref, last_ref: (0, 0)),
                      pl.BlockSpec((None, d, tf), lambda i, c, te_ref, nv_ref, first_ref, last_ref:
                                   (te_ref[i], 0, chunk(i, c, nv_ref, first_ref))),
                      pl.BlockSpec((None, d, tf), lambda i, c, te_ref, nv_ref, first_ref, last_ref:
                                   (te_ref[i], 0, chunk(i, c, nv_ref, first_ref) + nc)),
                      pl.BlockSpec((None, tf, d), lambda i, c, te_ref, nv_ref, first_ref, last_ref:
                                   (te_ref[i], chunk(i, c, nv_ref, first_ref), 0))],
            out_specs=pl.BlockSpec(memory_space=pl.ANY),
            scratch_shapes=[pltpu.VMEM((2, tm, d), F32), pltpu.VMEM((tm, d), BF16), pltpu.VMEM((2, tm, d), F32),
                            pltpu.VMEM((nc, d, tf), BF16), pltpu.VMEM((nc, d, tf), BF16), pltpu.VMEM((nc, tf, d), BF16),
                            pltpu.SemaphoreType.DMA, pltpu.SemaphoreType.DMA]),
        out_shape=jax.ShapeDtypeStruct((n_asg + tm, d), F32),
        compiler_params=_params("arbitrary", "arbitrary"),
        name="moe_experts",
    )(te.astype(jnp.int32), nv, first, last.reshape(1).astype(jnp.int32), rows_in_ext, rows_in_ext, rows_out_ext,
      rows_out_ext, x, g, w_gu, w_gu, w_down)


def _combine_body(x_ref, y0_ref, y1_ref, wts_ref, g_ref, outp_ref, outs_ref, *, npt):
    wts = wts_ref[...]
    moe = wts[:, 0:1] * y0_ref[...] + wts[:, 1:2] * y1_ref[...]
    _store_split(pl.program_id(0) < npt, outp_ref, outs_ref, _rms(x_ref[...] + moe, g_ref[...]))


def _combine(x, y2, wts, g, n_p):
    n, d = x.shape
    n_s = n - n_p
    tm = _pick_tile(math.gcd(n_p, n_s), 512)
    nt = n // tm
    npt = n_p // tm
    row = lambda i: (i, 0)
    rp, rs = _split_maps(npt)
    return pl.pallas_call(
        functools.partial(_combine_body, npt=npt),
        grid=(nt,),
        in_specs=[pl.BlockSpec((tm, d), row), pl.BlockSpec((tm, d), row), pl.BlockSpec((tm, d), lambda i: (i + nt, 0)),
                  pl.BlockSpec((tm, LANES), row), pl.BlockSpec((1, d), lambda i: (0, 0))],
        out_specs=[pl.BlockSpec((tm, d), rp), pl.BlockSpec((tm, d), rs)],
        out_shape=[jax.ShapeDtypeStruct((n_p, d), F32), jax.ShapeDtypeStruct((n_s, d), F32)],
        compiler_params=_params("arbitrary"),
        name="moe_combine",
    )(x, y2, y2, wts, g)


def _rope_tables(pos, hd):
    half = hd // 2
    inv = ROPE_THETA ** (-jnp.arange(half, dtype=F32) / half)
    ang = pos.astype(F32)[:, None] * inv[None, :]
    cos = jnp.cos(ang)
    sin = jnp.sin(ang)
    reps = LANES // hd
    return jnp.tile(jnp.concatenate([cos, cos], axis=1), (1, reps)), jnp.tile(jnp.concatenate([-sin, sin], axis=1), (1, reps))


def kernel(x_prompt, x_sample, cache_k_win, cache_v_win, state_hgrn, norm_mix, norm_ffn, norm_final,
           w_qkv, b_qkv, w_o_attn, b_o_attn, sinks, w_in_hg, hg_lower, hg_norm, w_o_hg,
           w_gu_dense, w_down_dense, w_router, b_router, w_gu_moe, w_down_moe):
    batch, seq, d = x_prompt.shape
    n_seq, t_dec, _ = x_sample.shape
    n_kv, hd = cache_k_win.shape[3], cache_k_win.shape[4]
    n_heads = sinks.shape[1]
    g_per = n_heads // n_kv
    n_q, n_k = n_heads * hd, n_kv * hd
    n_p, n_s = batch * seq, n_seq * t_dec
    n_exp = w_router.shape[2]
    wc = cache_k_win.shape[2]
    assert norm_mix.shape[0] == 2 and hd * 2 == LANES and d % HG_DK == 0 and seq % WINDOW == 0 and wc == WINDOW

    tile_q = _pick_tile(math.gcd(n_p, n_s), 512)
    assert tile_q % t_dec == 0
    rope_p = _rope_tables(jnp.arange(seq), hd)
    rope_s = _rope_tables(jnp.tile(PAST_LEN + jnp.arange(t_dec), tile_q // t_dec), hd)
    x, q, k, v = _qkv_rope(x_prompt.reshape(n_p, d), x_sample.reshape(n_s, d), norm_mix[0:1], w_qkv[0], b_qkv[0:1],
                           rope_p, rope_s, n_q, n_k, hd)
    o = _swa_prompt(q, k, v, sinks[0], jnp.zeros((n_p + n_s, n_q), BF16), batch, seq, n_kv, g_per, hd)
    kc = cache_k_win[0].reshape(n_seq, wc, n_k)
    vc = cache_v_win[0].reshape(n_seq, wc, n_k)
    o = _swa_sample(q, k, v, kc, vc, sinks[0], o, n_p, n_seq, t_dec, n_kv, g_per, hd)
    last_win = lambda a: jnp.stack([a[(b + 1) * seq - wc:(b + 1) * seq] for b in range(batch)]).reshape(batch, wc, n_kv, hd)
    k_win_p, v_win_p = last_win(k), last_win(v)
    k_win_s = jnp.concatenate([cache_k_win[0][:, t_dec:], k[n_p:].reshape(n_seq, t_dec, n_kv, hd)], axis=1)
    v_win_s = jnp.concatenate([cache_v_win[0][:, t_dec:], v[n_p:].reshape(n_seq, t_dec, n_kv, hd)], axis=1)
    x = _oproj_ffn(o, x, w_o_attn[0], b_o_attn[0:1], norm_ffn[0:1], w_gu_dense[0], w_down_dense[0])

    lb_sm = jax.nn.softmax(hg_lower.astype(F32), axis=0)
    lb = (jnp.cumsum(lb_sm, axis=0) - lb_sm[0])[1:2]
    in_half = functools.partial(_hg_inproj_half, x, norm_mix[1:2], w_in_hg[0], lb)
    hq, hlf, hk = in_half(0)
    hv, hgate = in_half(1)
    n_hh = d // HG_DK
    c_p = math.gcd(seq, HG_CHUNK)
    o_p, s_p = _hg_core(hq, hk, hlf, hv, jnp.zeros((batch, n_hh, HG_DK, HG_DK), F32),
                        c_p, 1, 2 if batch % 2 == 0 else 1, t=seq)
    c_s = SUBLANES * ((t_dec + SUBLANES - 1) // SUBLANES)
    pad = lambda a: jnp.pad(a[n_p:].reshape(n_seq, t_dec, d), ((0, 0), (0, c_s - t_dec), (0, 0)))
    o_s, s_s = _hg_core(pad(hq), pad(hk), pad(hlf), pad(hv), state_hgrn[0], c_s, 1, 8 if n_seq % 8 == 0 else 1)
    o_s = o_s[:, :t_dec].reshape(n_s, d)

    wr = jnp.pad(w_router[0], ((0, 0), (0, LANES - n_exp)))
    br = jnp.pad(b_router[0:1], ((0, 0), (0, LANES - n_exp)))
    x3, ids, wts = _hg_out_router(o_p.reshape(n_p, d), o_s, hgate, x, hg_norm[0:1], w_o_hg[0], norm_ffn[1:2],
                                  wr, br, n_exp)
    y2 = _moe_experts(x3, norm_ffn[1:2], ids[:, :TOP_K], w_gu_moe[0], w_down_moe[0])
    y_p, y_s = _combine(x3, y2, wts, norm_final.reshape(1, d), n_p)

    return (y_p.reshape(batch, seq, d), y_s.reshape(n_seq, t_dec, d),
            k_win_p[None], v_win_p[None], k_win_s[None], v_win_s[None], s_p[None], s_s[None])
```

```python
import functools
import math

import jax
import jax.numpy as jnp
from jax import lax
from jax.experimental import pallas as pl
from jax.experimental.pallas import tpu as pltpu

F32 = jnp.float32
BF16 = jnp.bfloat16

NORM_EPS = 1e-5
WINDOW = 128
PAST_LEN = 16384
ROPE_THETA = 10000.0
HG_DK = 128
HG_CHUNK = 256
HG_GROUP = 2
HG_SAFE_DECAY = 60.0
HG_SAFE_Q = 1e9
TOP_K = 2
LANES = 128
SUBLANES = 8
VMEM_LIMIT = 56 * 1024 * 1024

NT_DIMS = (((1,), (1,)), ((), ()))
TN_DIMS = (((0,), (0,)), ((), ()))


def _pick_tile(n, target):
    for t in (1536, 1024, 768, 512, 384, 256, 192, 128, 64, 32, 16, 8):
        if t <= target and n % t == 0:
            return t
    raise ValueError(f"no row tile for {n}")


def _params(*sem):
    return pltpu.CompilerParams(dimension_semantics=sem, vmem_limit_bytes=VMEM_LIMIT)


def _rms(x, g):
    return x * lax.rsqrt(jnp.mean(x * x, axis=-1, keepdims=True) + NORM_EPS) * g


def _split_maps(npt):
    return (lambda i, *_: (jnp.minimum(i, npt - 1), 0)), (lambda i, *_: (jnp.maximum(i - npt, 0), 0))


def _store_split(is_prompt, ref_p, ref_s, val):
    @pl.when(is_prompt)
    def _():
        ref_p[...] = val

    @pl.when(jnp.logical_not(is_prompt))
    def _():
        ref_s[...] = val


def _qkv_body(xp_ref, xs_ref, g_ref, w_ref, b_ref, cosp_ref, sinp_ref, coss_ref, sins_ref,
              x_ref, q_ref, k_ref, v_ref, wbf_ref,
              *, npt, n_q, n_k, hd):
    i = pl.program_id(0)

    @pl.when(i == 0)
    def _():
        wbf_ref[...] = w_ref[...].astype(BF16)

    x = jnp.where(i < npt, xp_ref[...], xs_ref[...])
    x_ref[...] = x
    h = _rms(x, g_ref[...]).astype(BF16)
    y = jnp.dot(h, wbf_ref[...], preferred_element_type=F32) + b_ref[...]
    cos = jnp.where(i < npt, cosp_ref[...], coss_ref[...])
    sin = jnp.where(i < npt, sinp_ref[...], sins_ref[...])
    lane = lax.broadcasted_iota(jnp.int32, cos.shape, 1)
    first = (lane % hd) < (hd // 2)

    def rope(blk):
        partner = jnp.where(first, pltpu.roll(blk, LANES - hd // 2, 1), pltpu.roll(blk, hd // 2, 1))
        return blk * cos + partner * sin

    scale = hd ** -0.5
    for j in range(n_q // LANES):
        q_ref[:, j * LANES:(j + 1) * LANES] = (rope(y[:, j * LANES:(j + 1) * LANES]) * scale).astype(BF16)
    for j in range(n_k // LANES):
        k_ref[:, j * LANES:(j + 1) * LANES] = rope(y[:, n_q + j * LANES:n_q + (j + 1) * LANES])
    v_ref[...] = y[:, n_q + n_k:]


def _qkv_rope(xp, xs, g, w, b, rope_p, rope_s, n_q, n_k, hd):
    (n_p, d), n_s = xp.shape, xs.shape[0]
    n = n_p + n_s
    n_out = w.shape[1]
    tm = _pick_tile(math.gcd(n_p, n_s), 512)
    npt = n_p // tm
    row = lambda i: (i, 0)
    fix = lambda i: (0, 0)
    rp, rs = _split_maps(npt)
    seq_tiles = rope_p[0].shape[0] // tm
    assert rope_p[0].shape[0] % tm == 0 and rope_s[0].shape[0] == tm
    pos_p = lambda i: (jnp.minimum(i, npt - 1) % seq_tiles, 0)
    return pl.pallas_call(
        functools.partial(_qkv_body, npt=npt, n_q=n_q, n_k=n_k, hd=hd),
        grid=(n // tm,),
        in_specs=[pl.BlockSpec((tm, d), rp), pl.BlockSpec((tm, d), rs), pl.BlockSpec((1, d), fix),
                  pl.BlockSpec((d, n_out), fix), pl.BlockSpec((1, n_out), fix),
                  pl.BlockSpec((tm, LANES), pos_p), pl.BlockSpec((tm, LANES), pos_p),
                  pl.BlockSpec((tm, LANES), fix), pl.BlockSpec((tm, LANES), fix)],
        out_specs=[pl.BlockSpec((tm, d), row), pl.BlockSpec((tm, n_q), row), pl.BlockSpec((tm, n_k), row),
                   pl.BlockSpec((tm, n_k), row)],
        out_shape=[jax.ShapeDtypeStruct((n, d), F32), jax.ShapeDtypeStruct((n, n_q), BF16),
                   jax.ShapeDtypeStruct((n, n_k), F32), jax.ShapeDtypeStruct((n, n_k), F32)],
        scratch_shapes=[pltpu.VMEM((d, n_out), BF16)],
        compiler_params=_params("arbitrary"),
        name="qkv_rope",
    )(xp, xs, g, w, b, *rope_p, *rope_s)


def _sink_column(sink_ref, kh, g_per, rows_per):
    blk = lax.broadcasted_iota(jnp.int32, (g_per * rows_per, 1), 0) // rows_per
    col = jnp.full((g_per * rows_per, 1), sink_ref[kh * g_per], F32)
    for g in range(1, g_per):
        col = jnp.where(blk == g, sink_ref[kh * g_per + g], col)
    return col


def _stack_heads(q, kh, g_per, hd):
    return jnp.concatenate([q[:, (kh * g_per + g) * hd:(kh * g_per + g + 1) * hd] for g in range(g_per)], axis=0)


def _swa_prompt_body(sink_ref, q_ref, kp_ref, kc_ref, vp_ref, vc_ref, o_ref, *, n_kv, g_per, hd, w, nb, n_blocks):
    blk = pl.program_id(0)

    @pl.when(blk >= n_blocks)
    def _():
        o_ref[...] = jnp.zeros(o_ref.shape, o_ref.dtype)

    @pl.when(blk < n_blocks)
    def _():
        _swa_prompt_block(sink_ref, q_ref, kp_ref, kc_ref, vp_ref, vc_ref, o_ref, blk % nb,
                          n_kv=n_kv, g_per=g_per, hd=hd, w=w)


def _swa_prompt_block(sink_ref, q_ref, kp_ref, kc_ref, vp_ref, vc_ref, o_ref, n, *, n_kv, g_per, hd, w):
    q = q_ref[...]
    kk = jnp.concatenate([kp_ref[...], kc_ref[...]], axis=0).astype(BF16)
    vv = jnp.concatenate([vp_ref[...], vc_ref[...]], axis=0).astype(BF16)
    i = lax.broadcasted_iota(jnp.int32, (w, 2 * w), 0)
    j = lax.broadcasted_iota(jnp.int32, (w, 2 * w), 1)
    mask = (j > i) & (j <= i + w) & ((j >= w) | (n > 0))
    outs = []
    for h in range(n_kv * g_per):
        ks = slice((h // g_per) * hd, (h // g_per + 1) * hd)
        s = lax.dot_general(q[:, h * hd:(h + 1) * hd], kk[:, ks], NT_DIMS, preferred_element_type=F32)
        s = jnp.where(mask, s, -jnp.inf)
        sink = sink_ref[h]
        m = jnp.maximum(jnp.max(s, axis=-1, keepdims=True), sink)
        p = jnp.exp(s - m)
        p = p / (jnp.sum(p, axis=-1, keepdims=True) + jnp.exp(sink - m))
        outs.append(jnp.dot(p.astype(BF16), vv[:, ks], preferred_element_type=F32))
    o_ref[...] = jnp.concatenate(outs, axis=1).astype(BF16)


def _swa_prompt(q, k, v, sinks, batch, seq, n_kv, g_per, hd):
    w = WINDOW
    nb = seq // w
    n, dq = q.shape
    dkv = k.shape[1]
    cur = lambda i: (i, 0)
    prev = lambda i: (jnp.maximum(i - 1, 0), 0)
    return pl.pallas_call(
        functools.partial(_swa_prompt_body, n_kv=n_kv, g_per=g_per, hd=hd, w=w, nb=nb, n_blocks=batch * nb),
        grid=(pl.cdiv(n, w),),
        in_specs=[pl.BlockSpec(memory_space=pltpu.SMEM),
                  pl.BlockSpec((w, dq), cur), pl.BlockSpec((w, dkv), prev), pl.BlockSpec((w, dkv), cur),
                  pl.BlockSpec((w, dkv), prev), pl.BlockSpec((w, dkv), cur)],
        out_specs=pl.BlockSpec((w, dq), cur),
        out_shape=jax.ShapeDtypeStruct((n, dq), BF16),
        compiler_params=_params("arbitrary"),
        name="swa_prompt",
    )(sinks, q, k, k, v, v)


def _swa_sample_body(sink_ref, q_ref, kn_ref, vn_ref, kc_ref, vc_ref, o_all_ref, o_ref, *, n_kv, g_per, hd, bt, t, wc):
    del o_all_ref
    r = bt * t
    q = q_ref[...]
    kn = kn_ref[...].astype(BF16)
    vn = vn_ref[...].astype(BF16)
    kc = kc_ref[...].reshape(bt * wc, n_kv * hd).astype(BF16)
    vc = vc_ref[...].reshape(bt * wc, n_kv * hd).astype(BF16)
    row_c = lax.broadcasted_iota(jnp.int32, (g_per * r, bt * wc), 0) % r
    col_c = lax.broadcasted_iota(jnp.int32, (g_per * r, bt * wc), 1)
    mask_c = (col_c // wc == row_c // t) & (col_c % wc > row_c % t + (wc - WINDOW))
    row_n = lax.broadcasted_iota(jnp.int32, (g_per * r, r), 0) % r
    col_n = lax.broadcasted_iota(jnp.int32, (g_per * r, r), 1)
    mask_n = (col_n // t == row_n // t) & (col_n % t <= row_n % t)
    outs = []
    for kh in range(n_kv):
        hs = slice(kh * hd, (kh + 1) * hd)
        q4 = _stack_heads(q, kh, g_per, hd)
        sc = jnp.where(mask_c, lax.dot_general(q4, kc[:, hs], NT_DIMS, preferred_element_type=F32), -jnp.inf)
        sn = jnp.where(mask_n, lax.dot_general(q4, kn[:, hs], NT_DIMS, preferred_element_type=F32), -jnp.inf)
        sink = _sink_column(sink_ref, kh, g_per, r)
        m = jnp.maximum(jnp.maximum(jnp.max(sc, axis=-1, keepdims=True), jnp.max(sn, axis=-1, keepdims=True)), sink)
        pc = jnp.exp(sc - m)
        pn = jnp.exp(sn - m)
        den = jnp.sum(pc, axis=-1, keepdims=True) + jnp.sum(pn, axis=-1, keepdims=True) + jnp.exp(sink - m)
        o4 = (jnp.dot((pc / den).astype(BF16), vc[:, hs], preferred_element_type=F32)
              + jnp.dot((pn / den).astype(BF16), vn[:, hs], preferred_element_type=F32))
        outs.append(jnp.concatenate([o4[g * r:(g + 1) * r] for g in range(g_per)], axis=1))
    o_ref[...] = jnp.concatenate(outs, axis=1).astype(BF16)


def _swa_sample(q, k, v, k_cache, v_cache, sinks, o_all, row0, n_seq, t, n_kv, g_per, hd):
    wc = k_cache.shape[1]
    dq = q.shape[1]
    dkv = k.shape[1]
    bt = 8 if n_seq % 8 == 0 else n_seq
    r = bt * t
    assert row0 % r == 0
    off = row0 // r
    rows = lambda i: (off + i, 0)
    return pl.pallas_call(
        functools.partial(_swa_sample_body, n_kv=n_kv, g_per=g_per, hd=hd, bt=bt, t=t, wc=wc),
        grid=(n_seq // bt,),
        in_specs=[pl.BlockSpec(memory_space=pltpu.SMEM),
                  pl.BlockSpec((r, dq), rows), pl.BlockSpec((r, dkv), rows), pl.BlockSpec((r, dkv), rows),
                  pl.BlockSpec((bt, wc, dkv), lambda i: (i, 0, 0)), pl.BlockSpec((bt, wc, dkv), lambda i: (i, 0, 0)),
                  pl.BlockSpec(memory_space=pl.ANY)],
        out_specs=pl.BlockSpec((r, dq), rows),
        out_shape=jax.ShapeDtypeStruct(o_all.shape, o_all.dtype),
        input_output_aliases={6: 0},
        compiler_params=_params("arbitrary"),
        name="swa_sample",
    )(sinks, q, k, v, k_cache, v_cache, o_all)


def _swiglu_step(h, wg, wu, wd, acc_ref):
    a = jnp.dot(h, wg, preferred_element_type=F32)
    b = jnp.dot(h, wu, preferred_element_type=F32)
    act = (a * jax.nn.sigmoid(a) * b).astype(BF16)
    acc_ref[...] += jnp.dot(act, wd, preferred_element_type=F32)


def _oproj_ffn_body(o_ref, x_ref, wo_ref, bo_ref, g_ref, wg_ref, wu_ref, wd_ref, out_ref,
                    wo_bf, x1_ref, h_ref, acc_ref):
    i = pl.program_id(0)
    c = pl.program_id(1)

    @pl.when((i == 0) & (c == 0))
    def _():
        wo_bf[...] = wo_ref[...].astype(BF16)

    @pl.when(c == 0)
    def _():
        x1 = x_ref[...] + jnp.dot(o_ref[...], wo_bf[...], preferred_element_type=F32) + bo_ref[...]
        x1_ref[...] = x1
        h_ref[...] = _rms(x1, g_ref[...]).astype(BF16)
        acc_ref[...] = jnp.zeros_like(acc_ref)

    _swiglu_step(h_ref[...], wg_ref[...].astype(BF16), wu_ref[...].astype(BF16), wd_ref[...].astype(BF16), acc_ref)

    @pl.when(c == pl.num_programs(1) - 1)
    def _():
        out_ref[...] = x1_ref[...] + acc_ref[...]


def _ff_chunk(d_ff):
    for tf in (512, 256, 128):
        if d_ff % tf == 0:
            return tf
    raise ValueError(f"d_ff {d_ff} is not a multiple of {LANES}")


def _oproj_ffn(o, x, wo, bo, g, w_gu, w_down):
    n, d = x.shape
    d_ff = w_down.shape[0]
    tf = _ff_chunk(d_ff)
    nc = d_ff // tf
    tm = _pick_tile(n, 768)
    row = lambda i, c: (i, 0)
    fix = lambda i, c: (0, 0)
    return pl.pallas_call(
        _oproj_ffn_body,
        grid=(n // tm, nc),
        in_specs=[pl.BlockSpec((tm, o.shape[1]), row), pl.BlockSpec((tm, d), row),
                  pl.BlockSpec(wo.shape, fix), pl.BlockSpec((1, d), fix), pl.BlockSpec((1, d), fix),
                  pl.BlockSpec((d, tf), lambda i, c: (0, c)), pl.BlockSpec((d, tf), lambda i, c: (0, c + nc)),
                  pl.BlockSpec((tf, d), lambda i, c: (c, 0))],
        out_specs=pl.BlockSpec((tm, d), row),
        out_shape=jax.ShapeDtypeStruct((n, d), F32),
        scratch_shapes=[pltpu.VMEM(wo.shape, BF16), pltpu.VMEM((tm, d), F32), pltpu.VMEM((tm, d), BF16),
                        pltpu.VMEM((tm, d), F32)],
        compiler_params=_params("arbitrary", "arbitrary"),
        name="oproj_ffn",
    )(o, x, wo, bo, g, w_gu, w_gu, w_down)


def _hg_in_body(x_ref, g_ref, w_ref, lb_ref, *refs, half):
    out_refs, wbf_ref = refs[:-1], refs[-1]
    d = x_ref.shape[1]

    @pl.when(pl.program_id(0) == 0)
    def _():
        wbf_ref[...] = w_ref[...].astype(BF16)

    h = _rms(x_ref[...], g_ref[...]).astype(BF16)
    z = jnp.dot(h, wbf_ref[...], preferred_element_type=F32)
    z0, z1 = z[:, :d], z[:, d:]
    if half == 0:
        out_refs[0][...] = z0 * jax.nn.sigmoid(z0) * (HG_DK ** -0.5)
        lb = lb_ref[...]
        t = jnp.exp(-jnp.abs(z1))
        log_sig = jnp.minimum(z1, 0.0) - jnp.log(1.0 + t)
        a = jnp.log(lb)
        b = jnp.log1p(-lb) + log_sig
        out_refs[1][...] = jnp.maximum(a, b) + jnp.log(1.0 + jnp.exp(-jnp.abs(a - b)))
        out_refs[2][...] = (1.0 - lb) * (jnp.where(z1 >= 0.0, t, 1.0) / (1.0 + t))
    else:
        out_refs[0][...] = z0
        out_refs[1][...] = z1 * jax.nn.sigmoid(z1)


def _hg_inproj_half(x, g, w_in, lb, half):
    n, d = x.shape
    n_res = 3 if half == 0 else 2
    tm = _pick_tile(n, 512)
    row = lambda i: (i, 0)
    fix = lambda i: (0, 0)
    return pl.pallas_call(
        functools.partial(_hg_in_body, half=half),
        grid=(n // tm,),
        in_specs=[pl.BlockSpec((tm, d), row), pl.BlockSpec((1, d), fix), pl.BlockSpec((d, 2 * d), lambda i: (0, half)),
                  pl.BlockSpec((1, d), fix)],
        out_specs=[pl.BlockSpec((tm, d), row)] * n_res,
        out_shape=[jax.ShapeDtypeStruct((n, d), F32)] * n_res,
        scratch_shapes=[pltpu.VMEM((d, 2 * d), BF16)],
        compiler_params=_params("arbitrary"),
        name=f"hgrn_inproj_{half}",
    )(x, g, w_in, lb)


def _hg_core_body(*refs, c_len, n_chunk, n_seq, n_heads, carried, per_seq_inputs, gsz):
    n_in = n_seq if per_seq_inputs else 1
    in_refs, (s0_ref, o_ref, sout_ref, st_ref, g_ref, oi_ref) = refs[:4 * n_in], refs[4 * n_in:]

    def rd(kind, s, row_slice):
        if per_seq_inputs:
            return in_refs[kind * n_in + s][row_slice, :]
        return in_refs[kind][s, row_slice, :]

    c = pl.program_id(1)
    dk = HG_DK
    rb = c_len * n_chunk
    units = [(s, j) for s in range(n_seq) for j in range(n_chunk)]

    n_pairs = n_heads // gsz
    gw = gsz * dk
    state_diag = (lax.broadcasted_iota(jnp.int32, (gw, gw), 0) // dk == lax.broadcasted_iota(jnp.int32, (gw, gw), 1) // dk)
    rows_diag = (lax.broadcasted_iota(jnp.int32, (gsz * c_len, gw), 0) // c_len
                 == lax.broadcasted_iota(jnp.int32, (gsz * c_len, gw), 1) // dk)

    def pair_tile(x):
        x2 = jnp.concatenate([x] * gsz, axis=0)
        return jnp.where(rows_diag, x2, jnp.zeros_like(x2))

    def diag(g):
        return slice(g * dk, (g + 1) * dk)

    if carried:
        @pl.when(c == 0)
        def _():
            for s in range(n_seq):
                for p in range(n_pairs):
                    st_ref[s, p] = jnp.zeros((gw, gw), F32)
                    for g in range(gsz):
                        st_ref[s, p, diag(g), diag(g)] = s0_ref[s, gsz * p + g].T

    r = lax.broadcasted_iota(jnp.int32, (rb, rb), 0)
    cidx = lax.broadcasted_iota(jnp.int32, (rb, rb), 1)
    block_causal = ((r >= cidx) & (r // c_len == cidx // c_len)).astype(BF16)
    causal2 = (lax.broadcasted_iota(jnp.int32, (c_len, gsz * c_len), 0)
               >= lax.broadcasted_iota(jnp.int32, (c_len, gsz * c_len), 1) % c_len)

    def rows(j):
        return slice(j * c_len, (j + 1) * c_len)

    gcum, safe = [], None
    for s in range(n_seq):
        lf = rd(2, s, slice(None))
        lf_hi = lf.astype(BF16)
        rest = lf - lf_hi.astype(F32)
        lf_mid = rest.astype(BF16)
        lf_lo = (rest - lf_mid.astype(F32)).astype(BF16)
        gs = ((jnp.dot(block_causal, lf_lo, preferred_element_type=F32)
               + jnp.dot(block_causal, lf_mid, preferred_element_type=F32))
              + jnp.dot(block_causal, lf_hi, preferred_element_type=F32))
        gcum.append(gs)
        for j in range(n_chunk):
            mid = gs[j * c_len + c_len // 2 - 1:j * c_len + c_len // 2, :]
            ok = ((jnp.max(gs[j * c_len:j * c_len + 1, :] - mid) <= HG_SAFE_DECAY)
                  & (jnp.max(mid - gs[(j + 1) * c_len - 1:(j + 1) * c_len, :]) <= HG_SAFE_DECAY)
                  & (jnp.max(jnp.abs(rd(0, s, rows(j)))) <= HG_SAFE_Q))
            safe = ok if safe is None else (safe & ok)

    def g_mid(s, j):
        return gcum[s][j * c_len + c_len // 2 - 1:j * c_len + c_len // 2, :]

    def g_last(s, j):
        return gcum[s][(j + 1) * c_len - 1:(j + 1) * c_len, :]

    @pl.when(safe)
    def _():
        for u, (s, j) in enumerate(units):
            g = gcum[s][rows(j), :]
            qi = (rd(0, s, rows(j)) * jnp.exp(g - g_mid(s, j))).astype(BF16)
            ki = (rd(1, s, rows(j)) * jnp.exp(g_mid(s, j) - g)).astype(BF16)
            vb = rd(3, s, rows(j)).astype(BF16)
            for p in range(n_pairs):
                ps = slice(p * gw, (p + 1) * gw)
                a = lax.dot_general(qi[:, ps], pair_tile(ki[:, ps]), NT_DIMS, preferred_element_type=F32)
                a = jnp.where(causal2, a, 0.0).astype(BF16)
                oi_ref[u, :, ps] = jnp.dot(a, pair_tile(vb[:, ps]), preferred_element_type=F32)

    @pl.when(jnp.logical_not(safe))
    def _():
        lane_h = lax.broadcasted_iota(jnp.int32, (n_heads * dk, n_heads * dk), 0) // dk
        lane_w = lax.broadcasted_iota(jnp.int32, (n_heads * dk, n_heads * dk), 1) // dk
        head_sum = (lane_h == lane_w).astype(BF16)
        t_idx = lax.broadcasted_iota(jnp.int32, (c_len, 1), 0)
        for u, (s, j) in enumerate(units):
            g = gcum[s][rows(j), :]
            g_ref[...] = g
            q = rd(0, s, rows(j))

            def key_row(i, acc, s=s, j=j, g=g, q=q):
                gi = g_ref[pl.ds(i, 1), :]
                decay = jnp.exp(jnp.where(t_idx >= i, g - gi, -jnp.inf))
                term = (q * decay * rd(1, s, pl.ds(j * c_len + i, 1))).astype(BF16)
                a_i = jnp.dot(term, head_sum, preferred_element_type=F32)
                return acc + a_i * rd(3, s, pl.ds(j * c_len + i, 1))

            oi_ref[u] = lax.fori_loop(0, c_len, key_row, jnp.zeros((c_len, n_heads * dk), F32))

    if not carried:
        e_rows = [jnp.exp(g_last(s, 0)) for s in range(n_seq)]
        e_cols = jnp.concatenate(e_rows + [jnp.zeros((LANES - n_seq, n_heads * dk), F32)], axis=0).T
        zero = jnp.zeros((dk, dk), F32)

    for u, (s, j) in enumerate(units):
        g = gcum[s][rows(j), :]
        qs = (rd(0, s, rows(j)) * jnp.exp(g)).astype(BF16)
        ks = (rd(1, s, rows(j)) * jnp.exp(g_last(s, j) - g)).astype(BF16)
        vb = rd(3, s, rows(j)).astype(BF16)
        e_last = jnp.exp(g_last(s, j))
        for p in range(n_pairs):
            ps = slice(p * gw, (p + 1) * gw)
            if carried:
                st = st_ref[s, p]
                o_ref[s, rows(j), ps] = oi_ref[u, :, ps] + lax.dot_general(qs[:, ps], st.astype(BF16), NT_DIMS,
                                                                           preferred_element_type=F32)
                upd = lax.dot_general(vb[:, ps], ks[:, ps], TN_DIMS, preferred_element_type=F32)
                st_ref[s, p] = jnp.where(state_diag, e_last[:, ps] * st + upd, 0.0)
            else:
                st = jnp.concatenate([jnp.concatenate([zero] * g + [s0_ref[s, gsz * p + g]] + [zero] * (gsz - 1 - g), axis=1)
                                      for g in range(gsz)], axis=0)
                o_ref[s, rows(j), ps] = oi_ref[u, :, ps] + jnp.dot(qs[:, ps], st.astype(BF16),
                                                                   preferred_element_type=F32)
                upd = lax.dot_general(ks[:, ps], vb[:, ps], TN_DIMS, preferred_element_type=F32)
                new = e_cols[ps, s:s + 1] * st + upd
                for g in range(gsz):
                    sout_ref[s, gsz * p + g] = new[diag(g), diag(g)]

    if carried:
        @pl.when(c == pl.num_programs(1) - 1)
        def _():
            for s in range(n_seq):
                for p in range(n_pairs):
                    for g in range(gsz):
                        sout_ref[s, gsz * p + g] = st_ref[s, p, diag(g), diag(g)].T


def _hg_core(q, k, lf, v, s0, c_len, n_chunk, n_seq, t=None):
    batch = s0.shape[0]
    per_seq_inputs = q.ndim == 2
    t = t if per_seq_inputs else q.shape[1]
    d = q.shape[-1]
    n_heads = d // HG_DK
    rb = c_len * n_chunk
    nc = t // rb
    carried = t > c_len
    gsz = HG_GROUP if carried and n_heads % HG_GROUP == 0 else 2
    assert t % rb == 0 and batch % n_seq == 0 and n_heads % gsz == 0 and n_seq <= LANES
    rows = lambda b, c: (b, c, 0)
    state = lambda b, c: (b, 0, 0, 0)
    if per_seq_inputs:
        seq_rows = lambda s: (lambda b, c: ((b * n_seq + s) * nc + c, 0))
        in_specs = [pl.BlockSpec((rb, d), seq_rows(s)) for _ in range(4) for s in range(n_seq)]
        inputs = [a for a in (q, k, lf, v) for _ in range(n_seq)]
    else:
        in_specs = [pl.BlockSpec((n_seq, rb, d), rows)] * 4
        inputs = [q, k, lf, v]
    return pl.pallas_call(
        functools.partial(_hg_core_body, c_len=c_len, n_chunk=n_chunk, n_seq=n_seq, n_heads=n_heads, carried=carried,
                          per_seq_inputs=per_seq_inputs, gsz=gsz),
        grid=(batch // n_seq, nc),
        in_specs=in_specs + [pl.BlockSpec((n_seq, n_heads, HG_DK, HG_DK), state)],
        out_specs=[pl.BlockSpec((n_seq, rb, d), rows), pl.BlockSpec((n_seq, n_heads, HG_DK, HG_DK), state)],
        out_shape=[jax.ShapeDtypeStruct((batch, t, d), F32),
                   jax.ShapeDtypeStruct((batch, n_heads, HG_DK, HG_DK), F32)],
        scratch_shapes=[pltpu.VMEM((n_seq if carried else 1, n_heads // gsz, gsz * HG_DK, gsz * HG_DK), F32),
                        pltpu.VMEM((c_len, d), F32),
                        pltpu.VMEM((n_seq * n_chunk, c_len, d), F32)],
        compiler_params=_params("arbitrary", "arbitrary"),
        name=f"hgrn_core_{c_len}",
    )(*inputs, s0)


def _hg_out_router_body(op_ref, os_ref, gate_ref, x_ref, gn_ref, wo_ref, g_ref, wr_ref, br_ref,
                        x3_ref, ids_ref, wts_ref, wo_bf, *, npt, n_heads, n_exp):
    i = pl.program_id(0)

    @pl.when(i == 0)
    def _():
        wo_bf[...] = wo_ref[...].astype(BF16)

    dk = HG_DK
    o = jnp.where(i < npt, op_ref[...], os_ref[...])
    gn = gn_ref[...]
    normed = jnp.concatenate([_rms(o[:, h * dk:(h + 1) * dk], gn) for h in range(n_heads)], axis=1)
    y = (normed * gate_ref[...]).astype(BF16)
    x3 = x_ref[...] + jnp.dot(y, wo_bf[...], preferred_element_type=F32)
    x3_ref[...] = x3
    h4 = _rms(x3, g_ref[...])
    h_hi = h4.astype(BF16)
    h_lo = (h4 - h_hi.astype(F32)).astype(BF16)
    wr = wr_ref[...]
    w_hi = wr.astype(BF16)
    w_lo = (wr - w_hi.astype(F32)).astype(BF16)
    logits = (jnp.dot(h_hi, w_hi, preferred_element_type=F32)
              + (jnp.dot(h_lo, w_hi, preferred_element_type=F32) + jnp.dot(h_hi, w_lo, preferred_element_type=F32))
              + br_ref[...])
    lane = lax.broadcasted_iota(jnp.int32, logits.shape, 1)
    logits = jnp.where(lane < n_exp, logits, -jnp.inf)
    m1 = jnp.max(logits, axis=-1, keepdims=True)
    i1 = jnp.min(jnp.where(logits == m1, lane, LANES), axis=-1, keepdims=True)
    rest = jnp.where(lane == i1, -jnp.inf, logits)
    m2 = jnp.max(rest, axis=-1, keepdims=True)
    i2 = jnp.min(jnp.where(rest == m2, lane, LANES), axis=-1, keepdims=True)
    e2 = jnp.exp(m2 - m1)
    den = 1.0 + e2
    ids_ref[...] = jnp.where(lane == 0, i1, jnp.where(lane == 1, i2, 0))
    wts_ref[...] = jnp.where(lane == 0, 1.0 / den, jnp.where(lane == 1, e2 / den, 0.0))


def _hg_out_router(o_p, o_s, gate, x, gn, wo, g, w_router, b_router, n_exp):
    n, d = x.shape
    n_p, n_s = o_p.shape[0], o_s.shape[0]
    tm = _pick_tile(math.gcd(n_p, n_s), 512)
    npt = n_p // tm
    row = lambda i: (i, 0)
    fix = lambda i: (0, 0)
    rp, rs = _split_maps(npt)
    return pl.pallas_call(
        functools.partial(_hg_out_router_body, npt=npt, n_heads=d // HG_DK, n_exp=n_exp),
        grid=(n // tm,),
        in_specs=[pl.BlockSpec((tm, d), rp), pl.BlockSpec((tm, d), rs), pl.BlockSpec((tm, d), row),
                  pl.BlockSpec((tm, d), row),
                  pl.BlockSpec((1, HG_DK), fix), pl.BlockSpec(wo.shape, fix), pl.BlockSpec((1, d), fix),
                  pl.BlockSpec((d, LANES), fix), pl.BlockSpec((1, LANES), fix)],
        out_specs=[pl.BlockSpec((tm, d), row), pl.BlockSpec((tm, LANES), row), pl.BlockSpec((tm, LANES), row)],
        out_shape=[jax.ShapeDtypeStruct((n, d), F32), jax.ShapeDtypeStruct((n, LANES), jnp.int32),
                   jax.ShapeDtypeStruct((n, LANES), F32)],
        scratch_shapes=[pltpu.VMEM(wo.shape, BF16)],
        compiler_params=_params("arbitrary"),
        name="hgrn_out_router",
    )(o_p, o_s, gate, x, gn, wo, g, w_router, b_router)


def _moe_body(te_ref, nv_ref, first_ref, last_ref, rin0_ref, rin_next_ref, rout_prev_ref, rout_cur_ref,
              x_hbm, g_ref, wg_ref, wu_ref, wd_ref, y_hbm,
              xg_ref, h_ref, acc_ref, wg_res, wu_res, wd_res, sem_in, sem_out, *, tm, rows_per_step):
    del te_ref
    i = pl.program_id(0)
    c = pl.program_id(1)
    slot = i % 2
    other = 1 - slot

    def row_in(buf, j, token):
        return pltpu.make_async_copy(x_hbm.at[pl.ds(token, 1)], xg_ref.at[buf, pl.ds(j, 1)], sem_in)

    def row_out(buf, j, dst):
        return pltpu.make_async_copy(acc_ref.at[buf, pl.ds(j, 1)], y_hbm.at[pl.ds(dst, 1)], sem_out)

    def all_in(buf):
        return pltpu.make_async_copy(x_hbm.at[pl.ds(0, tm)], xg_ref.at[buf], sem_in)

    def all_out(buf):
        return pltpu.make_async_copy(acc_ref.at[buf], y_hbm.at[pl.ds(0, tm)], sem_out)

    def each_row(fn):
        def body(j, carry):
            fn(j)
            return carry
        lax.fori_loop(0, tm, body, 0, unroll=8)

    @pl.when(nv_ref[i] > 0)
    def _():
        @pl.when((i == 0) & (c == 0))
        def _():
            acc_ref[1] = jnp.zeros(acc_ref.shape[1:], F32)
            each_row(lambda j: row_in(0, j, rin0_ref[0, 0, j]).start())
            all_in(0).wait()

        @pl.when(c == 0)
        def _():
            h_ref[...] = _rms(xg_ref[slot], g_ref[...]).astype(BF16)
            acc_ref[slot] = jnp.zeros(acc_ref.shape[1:], F32)

        def stream_rows():
            for u in range(rows_per_step):
                j = c * rows_per_step + u
                row_in(other, j, rin_next_ref[0, 0, j]).start()
                row_out(other, j, rout_prev_ref[0, 0, j]).start(priority=1)

        @pl.when(first_ref[i] > 0)
        def _():
            stream_rows()
            wg_res[c] = wg_ref[...].astype(BF16)
            wu_res[c] = wu_ref[...].astype(BF16)
            wd_res[c] = wd_ref[...].astype(BF16)
            _swiglu_step(h_ref[...], wg_res[c], wu_res[c], wd_res[c], acc_ref.at[slot])

        @pl.when(first_ref[i] == 0)
        def _():
            stream_rows()
            _swiglu_step(h_ref[...], wg_res[c], wu_res[c], wd_res[c], acc_ref.at[slot])

        @pl.when(c == pl.num_programs(1) - 1)
        def _():
            all_in(other).wait()
            all_out(other).wait()

            @pl.when(i == last_ref[0])
            def _():
                each_row(lambda j: row_out(slot, j, rout_cur_ref[0, 0, j]).start())
                all_out(slot).wait()


MOE_TILE_ROWS = 672
BF16_SUBLANES = 16


def _moe_experts(x, g, ids, w_gu, w_down):
    n, d = x.shape
    n_exp, d_ff = w_down.shape[0], w_down.shape[1]
    tf = _ff_chunk(d_ff)
    nc = d_ff // tf
    unit = BF16_SUBLANES * nc
    tm = unit * max(1, round(MOE_TILE_ROWS / unit))
    n_asg = n * TOP_K
    n_tiles = (n_asg + n_exp * (tm - 1) + tm - 1) // tm

    e_flat = ids.T.reshape(n_asg)
    onehot = (e_flat[:, None] == jnp.arange(n_exp, dtype=jnp.int32)[None, :]).astype(jnp.int32)
    counts = jnp.sum(onehot, axis=0)
    rank = jnp.sum((jnp.cumsum(onehot, axis=0) - onehot) * onehot, axis=1)
    tiles_per = (counts + tm - 1) // tm
    tile_end = jnp.cumsum(tiles_per)
    tile_start = tile_end - tiles_per
    slot = jnp.sum(onehot * tile_start[None, :], axis=1) * tm + rank
    asg = jnp.full((n_tiles * tm,), -1, jnp.int32).at[slot].set(jnp.arange(n_asg, dtype=jnp.int32),
                                                                 unique_indices=True)
    pad_row = n_asg + jnp.arange(n_tiles * tm, dtype=jnp.int32) % tm
    rows_in = jnp.where(asg >= 0, jnp.where(asg >= n, asg - n, asg), 0)
    rows_out = jnp.where(asg >= 0, asg, pad_row)
    rows_in_ext = jnp.concatenate([rows_in, jnp.zeros((tm,), jnp.int32)]).reshape(n_tiles + 1, 1, tm)
    rows_out_ext = jnp.concatenate([pad_row[:tm], rows_out]).reshape(n_tiles + 1, 1, tm)
    tile = jnp.arange(n_tiles, dtype=jnp.int32)
    last = tile_end[-1] - 1
    tile_c = jnp.minimum(tile, last)
    te = jnp.minimum(jnp.sum((tile_c[:, None] >= tile_end[None, :]).astype(jnp.int32), axis=1), n_exp - 1)
    te_hot = (te[:, None] == jnp.arange(n_exp, dtype=jnp.int32)[None, :]).astype(jnp.int32)
    cnt_t = jnp.sum(te_hot * counts[None, :], axis=1)
    start_t = jnp.sum(te_hot * tile_start[None, :], axis=1)
    nv = jnp.where(tile <= last, jnp.clip(cnt_t - (tile - start_t) * tm, 0, tm), 0).astype(jnp.int32)

    first = ((tile == start_t) & (nv > 0)).astype(jnp.int32)

    def chunk(i, c, nv_ref, first_ref):
        return jnp.where((nv_ref[i] > 0) & (first_ref[i] > 0), c, nc - 1)

    smem_rows = lambda index: pl.BlockSpec((1, 1, tm), index, memory_space=pltpu.SMEM)
    return pl.pallas_call(
        functools.partial(_moe_body, tm=tm, rows_per_step=tm // nc),
        grid_spec=pltpu.PrefetchScalarGridSpec(
            num_scalar_prefetch=4,
            grid=(n_tiles, nc),
            in_specs=[smem_rows(lambda i, c, te_ref, nv_ref, first_ref, last_ref: (0, 0, 0)),
                      smem_rows(lambda i, c, te_ref, nv_ref, first_ref, last_ref: (i + 1, 0, 0)),
                      smem_rows(lambda i, c, te_ref, nv_ref, first_ref, last_ref: (i, 0, 0)),
                      smem_rows(lambda i, c, te_ref, nv_ref, first_ref, last_ref: (i + 1, 0, 0)),
                      pl.BlockSpec(memory_space=pl.ANY),
                      pl.BlockSpec((1, d), lambda i, c, te_ref, nv_ref, first_ref, last_ref: (0, 0)),
                      pl.BlockSpec((None, d, tf), lambda i, c, te_ref, nv_ref, first_ref, last_ref:
                                   (te_ref[i], 0, chunk(i, c, nv_ref, first_ref))),
                      pl.BlockSpec((None, d, tf), lambda i, c, te_ref, nv_ref, first_ref, last_ref:
                                   (te_ref[i], 0, chunk(i, c, nv_ref, first_ref) + nc)),
                      pl.BlockSpec((None, tf, d), lambda i, c, te_ref, nv_ref, first_ref, last_ref:
                                   (te_ref[i], chunk(i, c, nv_ref, first_ref), 0))],
            out_specs=pl.BlockSpec(memory_space=pl.ANY),
            scratch_shapes=[pltpu.VMEM((2, tm, d), F32), pltpu.VMEM((tm, d), BF16), pltpu.VMEM((2, tm, d), F32),
                            pltpu.VMEM((nc, d, tf), BF16), pltpu.VMEM((nc, d, tf), BF16), pltpu.VMEM((nc, tf, d), BF16),
                            pltpu.SemaphoreType.DMA, pltpu.SemaphoreType.DMA]),
        out_shape=jax.ShapeDtypeStruct((n_asg + tm, d), F32),
        compiler_params=_params("arbitrary", "arbitrary"),
        name="moe_experts",
    )(te.astype(jnp.int32), nv, first, last.reshape(1).astype(jnp.int32), rows_in_ext, rows_in_ext, rows_out_ext,
      rows_out_ext, x, g, w_gu, w_gu, w_down)


def _combine_body(x_ref, y0_ref, y1_ref, wts_ref, g_ref, outp_ref, outs_ref, *, npt):
    wts = wts_ref[...]
    moe = wts[:, 0:1] * y0_ref[...] + wts[:, 1:2] * y1_ref[...]
    _store_split(pl.program_id(0) < npt, outp_ref, outs_ref, _rms(x_ref[...] + moe, g_ref[...]))


def _combine(x, y2, wts, g, n_p):
    n, d = x.shape
    n_s = n - n_p
    tm = _pick_tile(math.gcd(n_p, n_s), 512)
    nt = n // tm
    npt = n_p // tm
    row = lambda i: (i, 0)
    rp, rs = _split_maps(npt)
    return pl.pallas_call(
        functools.partial(_combine_body, npt=npt),
        grid=(nt,),
        in_specs=[pl.BlockSpec((tm, d), row), pl.BlockSpec((tm, d), row), pl.BlockSpec((tm, d), lambda i: (i + nt, 0)),
                  pl.BlockSpec((tm, LANES), row), pl.BlockSpec((1, d), lambda i: (0, 0))],
        out_specs=[pl.BlockSpec((tm, d), rp), pl.BlockSpec((tm, d), rs)],
        out_shape=[jax.ShapeDtypeStruct((n_p, d), F32), jax.ShapeDtypeStruct((n_s, d), F32)],
        compiler_params=_params("arbitrary"),
        name="moe_combine",
    )(x, y2, y2, wts, g)


def _rope_tables(pos, hd):
    half = hd // 2
    inv = ROPE_THETA ** (-jnp.arange(half, dtype=F32) / half)
    ang = pos.astype(F32)[:, None] * inv[None, :]
    cos = jnp.cos(ang)
    sin = jnp.sin(ang)
    reps = LANES // hd
    return jnp.tile(jnp.concatenate([cos, cos], axis=1), (1, reps)), jnp.tile(jnp.concatenate([-sin, sin], axis=1), (1, reps))


def kernel(x_prompt, x_sample, cache_k_win, cache_v_win, state_hgrn, norm_mix, norm_ffn, norm_final,
           w_qkv, b_qkv, w_o_attn, b_o_attn, sinks, w_in_hg, hg_lower, hg_norm, w_o_hg,
           w_gu_dense, w_down_dense, w_router, b_router, w_gu_moe, w_down_moe):
    batch, seq, d = x_prompt.shape
    n_seq, t_dec, _ = x_sample.shape
    n_kv, hd = cache_k_win.shape[3], cache_k_win.shape[4]
    n_heads = sinks.shape[1]
    g_per = n_heads // n_kv
    n_q, n_k = n_heads * hd, n_kv * hd
    n_p, n_s = batch * seq, n_seq * t_dec
    n_exp = w_router.shape[2]
    wc = cache_k_win.shape[2]
    assert norm_mix.shape[0] == 2 and hd * 2 == LANES and d % HG_DK == 0 and seq % WINDOW == 0 and wc == WINDOW

    tile_q = _pick_tile(math.gcd(n_p, n_s), 512)
    assert tile_q % t_dec == 0
    rope_p = _rope_tables(jnp.arange(seq), hd)
    rope_s = _rope_tables(jnp.tile(PAST_LEN + jnp.arange(t_dec), tile_q // t_dec), hd)
    x, q, k, v = _qkv_rope(x_prompt.reshape(n_p, d), x_sample.reshape(n_s, d), norm_mix[0:1], w_qkv[0], b_qkv[0:1],
                           rope_p, rope_s, n_q, n_k, hd)
    o = _swa_prompt(q, k, v, sinks[0], batch, seq, n_kv, g_per, hd)
    kc = cache_k_win[0].reshape(n_seq, wc, n_k)
    vc = cache_v_win[0].reshape(n_seq, wc, n_k)
    o = _swa_sample(q, k, v, kc, vc, sinks[0], o, n_p, n_seq, t_dec, n_kv, g_per, hd)
    last_win = lambda a: jnp.stack([a[(b + 1) * seq - wc:(b + 1) * seq] for b in range(batch)]).reshape(batch, wc, n_kv, hd)
    k_win_p, v_win_p = last_win(k), last_win(v)
    k_win_s = jnp.concatenate([cache_k_win[0][:, t_dec:], k[n_p:].reshape(n_seq, t_dec, n_kv, hd)], axis=1)
    v_win_s = jnp.concatenate([cache_v_win[0][:, t_dec:], v[n_p:].reshape(n_seq, t_dec, n_kv, hd)], axis=1)
    x = _oproj_ffn(o, x, w_o_attn[0], b_o_attn[0:1], norm_ffn[0:1], w_gu_dense[0], w_down_dense[0])

    lb_sm = jax.nn.softmax(hg_lower.astype(F32), axis=0)
    lb = (jnp.cumsum(lb_sm, axis=0) - lb_sm[0])[1:2]
    in_half = functools.partial(_hg_inproj_half, x, norm_mix[1:2], w_in_hg[0], lb)
    hq, hlf, hk = in_half(0)
    hv, hgate = in_half(1)
    n_hh = d // HG_DK
    c_p = math.gcd(seq, HG_CHUNK)
    o_p, s_p = _hg_core(hq, hk, hlf, hv, jnp.zeros((batch, n_hh, HG_DK, HG_DK), F32),
                        c_p, 1, 2 if batch % 2 == 0 else 1, t=seq)
    c_s = SUBLANES * ((t_dec + SUBLANES - 1) // SUBLANES)
    pad = lambda a: jnp.pad(a[n_p:].reshape(n_seq, t_dec, d), ((0, 0), (0, c_s - t_dec), (0, 0)))
    o_s, s_s = _hg_core(pad(hq), pad(hk), pad(hlf), pad(hv), state_hgrn[0], c_s, 1,
                        next(s for s in (16, 8, 4, 2, 1) if n_seq % s == 0))
    o_s = o_s[:, :t_dec].reshape(n_s, d)

    wr = jnp.pad(w_router[0], ((0, 0), (0, LANES - n_exp)))
    br = jnp.pad(b_router[0:1], ((0, 0), (0, LANES - n_exp)))
    x3, ids, wts = _hg_out_router(o_p.reshape(n_p, d), o_s, hgate, x, hg_norm[0:1], w_o_hg[0], norm_ffn[1:2],
                                  wr, br, n_exp)
    y2 = _moe_experts(x3, norm_ffn[1:2], ids[:, :TOP_K], w_gu_moe[0], w_down_moe[0])
    y_p, y_s = _combine(x3, y2, wts, norm_final.reshape(1, d), n_p)

    return (y_p.reshape(batch, seq, d), y_s.reshape(n_seq, t_dec, d),
            k_win_p[None], v_win_p[None], k_win_s[None], v_win_s[None], s_p[None], s_s[None])
```

```python
import functools
import math

import jax
import jax.numpy as jnp
from jax import lax
from jax.experimental import pallas as pl
from jax.experimental.pallas import tpu as pltpu

F32 = jnp.float32
BF16 = jnp.bfloat16

NORM_EPS = 1e-5
WINDOW = 128
PAST_LEN = 16384
ROPE_THETA = 10000.0
HG_DK = 128
HG_CHUNK = 256
HG_GROUP = 2
HG_SAFE_DECAY = 60.0
HG_SAFE_Q = 1e9
TOP_K = 2
LANES = 128
SUBLANES = 8
VMEM_LIMIT = 56 * 1024 * 1024

NT_DIMS = (((1,), (1,)), ((), ()))
TN_DIMS = (((0,), (0,)), ((), ()))


def _pick_tile(n, target):
    for t in (1536, 1024, 768, 512, 384, 256, 192, 128, 64, 32, 16, 8):
        if t <= target and n % t == 0:
            return t
    raise ValueError(f"no row tile for {n}")


def _params(*sem):
    return pltpu.CompilerParams(dimension_semantics=sem, vmem_limit_bytes=VMEM_LIMIT)


def _rms(x, g):
    return x * lax.rsqrt(jnp.mean(x * x, axis=-1, keepdims=True) + NORM_EPS) * g


def _split_maps(npt):
    return (lambda i, *_: (jnp.minimum(i, npt - 1), 0)), (lambda i, *_: (jnp.maximum(i - npt, 0), 0))


def _store_split(is_prompt, ref_p, ref_s, val):
    @pl.when(is_prompt)
    def _():
        ref_p[...] = val

    @pl.when(jnp.logical_not(is_prompt))
    def _():
        ref_s[...] = val


def _qkv_body(xp_ref, xs_ref, g_ref, w_ref, b_ref, cosp_ref, sinp_ref, coss_ref, sins_ref,
              x_ref, q_ref, k_ref, v_ref, wbf_ref,
              *, npt, n_q, n_k, hd):
    i = pl.program_id(0)

    @pl.when(i == 0)
    def _():
        wbf_ref[...] = w_ref[...].astype(BF16)

    x = jnp.where(i < npt, xp_ref[...], xs_ref[...])
    x_ref[...] = x
    h = _rms(x, g_ref[...]).astype(BF16)
    y = jnp.dot(h, wbf_ref[...], preferred_element_type=F32) + b_ref[...]
    cos = jnp.where(i < npt, cosp_ref[...], coss_ref[...])
    sin = jnp.where(i < npt, sinp_ref[...], sins_ref[...])
    lane = lax.broadcasted_iota(jnp.int32, cos.shape, 1)
    first = (lane % hd) < (hd // 2)

    def rope(blk):
        partner = jnp.where(first, pltpu.roll(blk, LANES - hd // 2, 1), pltpu.roll(blk, hd // 2, 1))
        return blk * cos + partner * sin

    scale = hd ** -0.5
    for j in range(n_q // LANES):
        q_ref[:, j * LANES:(j + 1) * LANES] = (rope(y[:, j * LANES:(j + 1) * LANES]) * scale).astype(BF16)
    for j in range(n_k // LANES):
        k_ref[:, j * LANES:(j + 1) * LANES] = rope(y[:, n_q + j * LANES:n_q + (j + 1) * LANES])
    v_ref[...] = y[:, n_q + n_k:]


def _qkv_rope(xp, xs, g, w, b, rope_p, rope_s, n_q, n_k, hd):
    (n_p, d), n_s = xp.shape, xs.shape[0]
    n = n_p + n_s
    n_out = w.shape[1]
    tm = _pick_tile(math.gcd(n_p, n_s), 512)
    npt = n_p // tm
    row = lambda i: (i, 0)
    fix = lambda i: (0, 0)
    rp, rs = _split_maps(npt)
    seq_tiles = rope_p[0].shape[0] // tm
    assert rope_p[0].shape[0] % tm == 0 and rope_s[0].shape[0] == tm
    pos_p = lambda i: (jnp.minimum(i, npt - 1) % seq_tiles, 0)
    return pl.pallas_call(
        functools.partial(_qkv_body, npt=npt, n_q=n_q, n_k=n_k, hd=hd),
        grid=(n // tm,),
        in_specs=[pl.BlockSpec((tm, d), rp), pl.BlockSpec((tm, d), rs), pl.BlockSpec((1, d), fix),
                  pl.BlockSpec((d, n_out), fix), pl.BlockSpec((1, n_out), fix),
                  pl.BlockSpec((tm, LANES), pos_p), pl.BlockSpec((tm, LANES), pos_p),
                  pl.BlockSpec((tm, LANES), fix), pl.BlockSpec((tm, LANES), fix)],
        out_specs=[pl.BlockSpec((tm, d), row), pl.BlockSpec((tm, n_q), row), pl.BlockSpec((tm, n_k), row),
                   pl.BlockSpec((tm, n_k), row)],
        out_shape=[jax.ShapeDtypeStruct((n, d), F32), jax.ShapeDtypeStruct((n, n_q), BF16),
                   jax.ShapeDtypeStruct((n, n_k), F32), jax.ShapeDtypeStruct((n, n_k), F32)],
        scratch_shapes=[pltpu.VMEM((d, n_out), BF16)],
        compiler_params=_params("arbitrary"),
        name="qkv_rope",
    )(xp, xs, g, w, b, *rope_p, *rope_s)


def _sink_column(sink_ref, kh, g_per, rows_per):
    blk = lax.broadcasted_iota(jnp.int32, (g_per * rows_per, 1), 0) // rows_per
    col = jnp.full((g_per * rows_per, 1), sink_ref[kh * g_per], F32)
    for g in range(1, g_per):
        col = jnp.where(blk == g, sink_ref[kh * g_per + g], col)
    return col


def _stack_heads(q, kh, g_per, hd):
    return jnp.concatenate([q[:, (kh * g_per + g) * hd:(kh * g_per + g + 1) * hd] for g in range(g_per)], axis=0)


def _swa_prompt_body(sink_ref, q_ref, kp_ref, kc_ref, vp_ref, vc_ref, o_ref, *, n_kv, g_per, hd, w, nb, n_blocks):
    blk = pl.program_id(0)

    @pl.when(blk >= n_blocks)
    def _():
        o_ref[...] = jnp.zeros(o_ref.shape, o_ref.dtype)

    @pl.when(blk < n_blocks)
    def _():
        _swa_prompt_block(sink_ref, q_ref, kp_ref, kc_ref, vp_ref, vc_ref, o_ref, blk % nb,
                          n_kv=n_kv, g_per=g_per, hd=hd, w=w)


def _swa_prompt_block(sink_ref, q_ref, kp_ref, kc_ref, vp_ref, vc_ref, o_ref, n, *, n_kv, g_per, hd, w):
    q = q_ref[...]
    kk = jnp.concatenate([kp_ref[...], kc_ref[...]], axis=0).astype(BF16)
    vv = jnp.concatenate([vp_ref[...], vc_ref[...]], axis=0).astype(BF16)
    i = lax.broadcasted_iota(jnp.int32, (w, 2 * w), 0)
    j = lax.broadcasted_iota(jnp.int32, (w, 2 * w), 1)
    mask = (j > i) & (j <= i + w) & ((j >= w) | (n > 0))
    outs = []
    for h in range(n_kv * g_per):
        ks = slice((h // g_per) * hd, (h // g_per + 1) * hd)
        s = lax.dot_general(q[:, h * hd:(h + 1) * hd], kk[:, ks], NT_DIMS, preferred_element_type=F32)
        s = jnp.where(mask, s, -jnp.inf)
        sink = sink_ref[h]
        m = jnp.maximum(jnp.max(s, axis=-1, keepdims=True), sink)
        p = jnp.exp(s - m)
        p = p / (jnp.sum(p, axis=-1, keepdims=True) + jnp.exp(sink - m))
        outs.append(jnp.dot(p.astype(BF16), vv[:, ks], preferred_element_type=F32))
    o_ref[...] = jnp.concatenate(outs, axis=1).astype(BF16)


def _swa_prompt(q, k, v, sinks, batch, seq, n_kv, g_per, hd):
    w = WINDOW
    nb = seq // w
    n, dq = q.shape
    dkv = k.shape[1]
    cur = lambda i: (i, 0)
    prev = lambda i: (jnp.maximum(i - 1, 0), 0)
    return pl.pallas_call(
        functools.partial(_swa_prompt_body, n_kv=n_kv, g_per=g_per, hd=hd, w=w, nb=nb, n_blocks=batch * nb),
        grid=(pl.cdiv(n, w),),
        in_specs=[pl.BlockSpec(memory_space=pltpu.SMEM),
                  pl.BlockSpec((w, dq), cur), pl.BlockSpec((w, dkv), prev), pl.BlockSpec((w, dkv), cur),
                  pl.BlockSpec((w, dkv), prev), pl.BlockSpec((w, dkv), cur)],
        out_specs=pl.BlockSpec((w, dq), cur),
        out_shape=jax.ShapeDtypeStruct((n, dq), BF16),
        compiler_params=_params("arbitrary"),
        name="swa_prompt",
    )(sinks, q, k, k, v, v)


def _swa_sample_body(sink_ref, q_ref, kn_ref, vn_ref, kc_ref, vc_ref, o_all_ref, o_ref, *, n_kv, g_per, hd, bt, t, wc):
    del o_all_ref
    r = bt * t
    q = q_ref[...]
    kn = kn_ref[...].astype(BF16)
    vn = vn_ref[...].astype(BF16)
    kc = kc_ref[...].reshape(bt * wc, n_kv * hd).astype(BF16)
    vc = vc_ref[...].reshape(bt * wc, n_kv * hd).astype(BF16)
    row_c = lax.broadcasted_iota(jnp.int32, (g_per * r, bt * wc), 0) % r
    col_c = lax.broadcasted_iota(jnp.int32, (g_per * r, bt * wc), 1)
    mask_c = (col_c // wc == row_c // t) & (col_c % wc > row_c % t + (wc - WINDOW))
    row_n = lax.broadcasted_iota(jnp.int32, (g_per * r, r), 0) % r
    col_n = lax.broadcasted_iota(jnp.int32, (g_per * r, r), 1)
    mask_n = (col_n // t == row_n // t) & (col_n % t <= row_n % t)
    outs = []
    for kh in range(n_kv):
        hs = slice(kh * hd, (kh + 1) * hd)
        q4 = _stack_heads(q, kh, g_per, hd)
        sc = jnp.where(mask_c, lax.dot_general(q4, kc[:, hs], NT_DIMS, preferred_element_type=F32), -jnp.inf)
        sn = jnp.where(mask_n, lax.dot_general(q4, kn[:, hs], NT_DIMS, preferred_element_type=F32), -jnp.inf)
        sink = _sink_column(sink_ref, kh, g_per, r)
        m = jnp.maximum(jnp.maximum(jnp.max(sc, axis=-1, keepdims=True), jnp.max(sn, axis=-1, keepdims=True)), sink)
        pc = jnp.exp(sc - m)
        pn = jnp.exp(sn - m)
        den = jnp.sum(pc, axis=-1, keepdims=True) + jnp.sum(pn, axis=-1, keepdims=True) + jnp.exp(sink - m)
        o4 = (jnp.dot((pc / den).astype(BF16), vc[:, hs], preferred_element_type=F32)
              + jnp.dot((pn / den).astype(BF16), vn[:, hs], preferred_element_type=F32))
        outs.append(jnp.concatenate([o4[g * r:(g + 1) * r] for g in range(g_per)], axis=1))
    o_ref[...] = jnp.concatenate(outs, axis=1).astype(BF16)


def _swa_sample(q, k, v, k_cache, v_cache, sinks, o_all, row0, n_seq, t, n_kv, g_per, hd):
    wc = k_cache.shape[1]
    dq = q.shape[1]
    dkv = k.shape[1]
    bt = 8 if n_seq % 8 == 0 else n_seq
    r = bt * t
    assert row0 % r == 0
    off = row0 // r
    rows = lambda i: (off + i, 0)
    return pl.pallas_call(
        functools.partial(_swa_sample_body, n_kv=n_kv, g_per=g_per, hd=hd, bt=bt, t=t, wc=wc),
        grid=(n_seq // bt,),
        in_specs=[pl.BlockSpec(memory_space=pltpu.SMEM),
                  pl.BlockSpec((r, dq), rows), pl.BlockSpec((r, dkv), rows), pl.BlockSpec((r, dkv), rows),
                  pl.BlockSpec((bt, wc, dkv), lambda i: (i, 0, 0)), pl.BlockSpec((bt, wc, dkv), lambda i: (i, 0, 0)),
                  pl.BlockSpec(memory_space=pl.ANY)],
        out_specs=pl.BlockSpec((r, dq), rows),
        out_shape=jax.ShapeDtypeStruct(o_all.shape, o_all.dtype),
        input_output_aliases={6: 0},
        compiler_params=_params("arbitrary"),
        name="swa_sample",
    )(sinks, q, k, v, k_cache, v_cache, o_all)


def _swiglu_step(h, wg, wu, wd, acc_ref):
    a = jnp.dot(h, wg, preferred_element_type=F32)
    b = jnp.dot(h, wu, preferred_element_type=F32)
    act = (a * jax.nn.sigmoid(a) * b).astype(BF16)
    acc_ref[...] += jnp.dot(act, wd, preferred_element_type=F32)


def _oproj_ffn_body(o_ref, x_ref, wo_ref, bo_ref, g_ref, wg_ref, wu_ref, wd_ref, out_ref,
                    wo_bf, x1_ref, h_ref, acc_ref):
    i = pl.program_id(0)
    c = pl.program_id(1)

    @pl.when((i == 0) & (c == 0))
    def _():
        wo_bf[...] = wo_ref[...].astype(BF16)

    @pl.when(c == 0)
    def _():
        x1 = x_ref[...] + jnp.dot(o_ref[...], wo_bf[...], preferred_element_type=F32) + bo_ref[...]
        x1_ref[...] = x1
        h_ref[...] = _rms(x1, g_ref[...]).astype(BF16)
        acc_ref[...] = jnp.zeros_like(acc_ref)

    _swiglu_step(h_ref[...], wg_ref[...].astype(BF16), wu_ref[...].astype(BF16), wd_ref[...].astype(BF16), acc_ref)

    @pl.when(c == pl.num_programs(1) - 1)
    def _():
        out_ref[...] = x1_ref[...] + acc_ref[...]


def _ff_chunk(d_ff):
    for tf in (512, 256, 128):
        if d_ff % tf == 0:
            return tf
    raise ValueError(f"d_ff {d_ff} is not a multiple of {LANES}")


def _oproj_ffn(o, x, wo, bo, g, w_gu, w_down):
    n, d = x.shape
    d_ff = w_down.shape[0]
    tf = _ff_chunk(d_ff)
    nc = d_ff // tf
    tm = _pick_tile(n, 768)
    row = lambda i, c: (i, 0)
    fix = lambda i, c: (0, 0)
    return pl.pallas_call(
        _oproj_ffn_body,
        grid=(n // tm, nc),
        in_specs=[pl.BlockSpec((tm, o.shape[1]), row), pl.BlockSpec((tm, d), row),
                  pl.BlockSpec(wo.shape, fix), pl.BlockSpec((1, d), fix), pl.BlockSpec((1, d), fix),
                  pl.BlockSpec((d, tf), lambda i, c: (0, c)), pl.BlockSpec((d, tf), lambda i, c: (0, c + nc)),
                  pl.BlockSpec((tf, d), lambda i, c: (c, 0))],
        out_specs=pl.BlockSpec((tm, d), row),
        out_shape=jax.ShapeDtypeStruct((n, d), F32),
        scratch_shapes=[pltpu.VMEM(wo.shape, BF16), pltpu.VMEM((tm, d), F32), pltpu.VMEM((tm, d), BF16),
                        pltpu.VMEM((tm, d), F32)],
        compiler_params=_params("arbitrary", "arbitrary"),
        name="oproj_ffn",
    )(o, x, wo, bo, g, w_gu, w_gu, w_down)


def _hg_in_body(x_ref, g_ref, w_ref, lb_ref, *refs, half):
    out_refs, wbf_ref = refs[:-1], refs[-1]
    d = x_ref.shape[1]

    @pl.when(pl.program_id(0) == 0)
    def _():
        wbf_ref[...] = w_ref[...].astype(BF16)

    h = _rms(x_ref[...], g_ref[...]).astype(BF16)
    z = jnp.dot(h, wbf_ref[...], preferred_element_type=F32)
    z0, z1 = z[:, :d], z[:, d:]
    if half == 0:
        out_refs[0][...] = z0 * jax.nn.sigmoid(z0) * (HG_DK ** -0.5)
        lb = lb_ref[...]
        t = jnp.exp(-jnp.abs(z1))
        log_sig = jnp.minimum(z1, 0.0) - jnp.log(1.0 + t)
        a = jnp.log(lb)
        b = jnp.log1p(-lb) + log_sig
        out_refs[1][...] = jnp.maximum(a, b) + jnp.log(1.0 + jnp.exp(-jnp.abs(a - b)))
        out_refs[2][...] = (1.0 - lb) * (jnp.where(z1 >= 0.0, t, 1.0) / (1.0 + t))
    else:
        out_refs[0][...] = z0
        out_refs[1][...] = z1 * jax.nn.sigmoid(z1)


def _hg_inproj_half(x, g, w_in, lb, half):
    n, d = x.shape
    n_res = 3 if half == 0 else 2
    tm = _pick_tile(n, 512)
    row = lambda i: (i, 0)
    fix = lambda i: (0, 0)
    return pl.pallas_call(
        functools.partial(_hg_in_body, half=half),
        grid=(n // tm,),
        in_specs=[pl.BlockSpec((tm, d), row), pl.BlockSpec((1, d), fix), pl.BlockSpec((d, 2 * d), lambda i: (0, half)),
                  pl.BlockSpec((1, d), fix)],
        out_specs=[pl.BlockSpec((tm, d), row)] * n_res,
        out_shape=[jax.ShapeDtypeStruct((n, d), F32)] * n_res,
        scratch_shapes=[pltpu.VMEM((d, 2 * d), BF16)],
        compiler_params=_params("arbitrary"),
        name=f"hgrn_inproj_{half}",
    )(x, g, w_in, lb)


def _hg_core_body(*refs, c_len, n_chunk, n_seq, n_heads, carried, per_seq_inputs, gsz):
    n_in = n_seq if per_seq_inputs else 1
    in_refs, (s0_ref, o_ref, sout_ref, st_ref, g_ref, oi_ref) = refs[:4 * n_in], refs[4 * n_in:]

    def rd(kind, s, row_slice):
        if per_seq_inputs:
            return in_refs[kind * n_in + s][row_slice, :]
        return in_refs[kind][s, row_slice, :]

    c = pl.program_id(1)
    dk = HG_DK
    rb = c_len * n_chunk
    units = [(s, j) for s in range(n_seq) for j in range(n_chunk)]

    n_pairs = n_heads // gsz
    gw = gsz * dk
    state_diag = (lax.broadcasted_iota(jnp.int32, (gw, gw), 0) // dk == lax.broadcasted_iota(jnp.int32, (gw, gw), 1) // dk)
    rows_diag = (lax.broadcasted_iota(jnp.int32, (gsz * c_len, gw), 0) // c_len
                 == lax.broadcasted_iota(jnp.int32, (gsz * c_len, gw), 1) // dk)

    def pair_tile(x):
        x2 = jnp.concatenate([x] * gsz, axis=0)
        return jnp.where(rows_diag, x2, jnp.zeros_like(x2))

    def diag(g):
        return slice(g * dk, (g + 1) * dk)

    if carried:
        @pl.when(c == 0)
        def _():
            for s in range(n_seq):
                for p in range(n_pairs):
                    st_ref[s, p] = jnp.zeros((gw, gw), F32)
                    for g in range(gsz):
                        st_ref[s, p, diag(g), diag(g)] = s0_ref[s, gsz * p + g].T

    r = lax.broadcasted_iota(jnp.int32, (rb, rb), 0)
    cidx = lax.broadcasted_iota(jnp.int32, (rb, rb), 1)
    block_causal = ((r >= cidx) & (r // c_len == cidx // c_len)).astype(BF16)
    causal2 = (lax.broadcasted_iota(jnp.int32, (c_len, gsz * c_len), 0)
               >= lax.broadcasted_iota(jnp.int32, (c_len, gsz * c_len), 1) % c_len)

    def rows(j):
        return slice(j * c_len, (j + 1) * c_len)

    gcum, safe = [], None
    for s in range(n_seq):
        lf = rd(2, s, slice(None))
        lf_hi = lf.astype(BF16)
        rest = lf - lf_hi.astype(F32)
        lf_mid = rest.astype(BF16)
        lf_lo = (rest - lf_mid.astype(F32)).astype(BF16)
        gs = ((jnp.dot(block_causal, lf_lo, preferred_element_type=F32)
               + jnp.dot(block_causal, lf_mid, preferred_element_type=F32))
              + jnp.dot(block_causal, lf_hi, preferred_element_type=F32))
        gcum.append(gs)
        for j in range(n_chunk):
            mid = gs[j * c_len + c_len // 2 - 1:j * c_len + c_len // 2, :]
            ok = ((jnp.max(gs[j * c_len:j * c_len + 1, :] - mid) <= HG_SAFE_DECAY)
                  & (jnp.max(mid - gs[(j + 1) * c_len - 1:(j + 1) * c_len, :]) <= HG_SAFE_DECAY)
                  & (jnp.max(jnp.abs(rd(0, s, rows(j)))) <= HG_SAFE_Q))
            safe = ok if safe is None else (safe & ok)

    def g_mid(s, j):
        return gcum[s][j * c_len + c_len // 2 - 1:j * c_len + c_len // 2, :]

    def g_last(s, j):
        return gcum[s][(j + 1) * c_len - 1:(j + 1) * c_len, :]

    @pl.when(safe)
    def _():
        for u, (s, j) in enumerate(units):
            g = gcum[s][rows(j), :]
            qi = (rd(0, s, rows(j)) * jnp.exp(g - g_mid(s, j))).astype(BF16)
            ki = (rd(1, s, rows(j)) * jnp.exp(g_mid(s, j) - g)).astype(BF16)
            vb = rd(3, s, rows(j)).astype(BF16)
            for p in range(n_pairs):
                ps = slice(p * gw, (p + 1) * gw)
                a = lax.dot_general(qi[:, ps], pair_tile(ki[:, ps]), NT_DIMS, preferred_element_type=F32)
                a = jnp.where(causal2, a, 0.0).astype(BF16)
                oi_ref[u, :, ps] = jnp.dot(a, pair_tile(vb[:, ps]), preferred_element_type=F32)

    @pl.when(jnp.logical_not(safe))
    def _():
        lane_h = lax.broadcasted_iota(jnp.int32, (n_heads * dk, n_heads * dk), 0) // dk
        lane_w = lax.broadcasted_iota(jnp.int32, (n_heads * dk, n_heads * dk), 1) // dk
        head_sum = (lane_h == lane_w).astype(BF16)
        t_idx = lax.broadcasted_iota(jnp.int32, (c_len, 1), 0)
        for u, (s, j) in enumerate(units):
            g = gcum[s][rows(j), :]
            g_ref[...] = g
            q = rd(0, s, rows(j))

            def key_row(i, acc, s=s, j=j, g=g, q=q):
                gi = g_ref[pl.ds(i, 1), :]
                decay = jnp.exp(jnp.where(t_idx >= i, g - gi, -jnp.inf))
                term = (q * decay * rd(1, s, pl.ds(j * c_len + i, 1))).astype(BF16)
                a_i = jnp.dot(term, head_sum, preferred_element_type=F32)
                return acc + a_i * rd(3, s, pl.ds(j * c_len + i, 1))

            oi_ref[u] = lax.fori_loop(0, c_len, key_row, jnp.zeros((c_len, n_heads * dk), F32))

    if not carried:
        e_rows = [jnp.exp(g_last(s, 0)) for s in range(n_seq)]
        e_cols = jnp.concatenate(e_rows + [jnp.zeros((LANES - n_seq, n_heads * dk), F32)], axis=0).T
        zero = jnp.zeros((dk, dk), F32)

    for u, (s, j) in enumerate(units):
        g = gcum[s][rows(j), :]
        qs = (rd(0, s, rows(j)) * jnp.exp(g)).astype(BF16)
        ks = (rd(1, s, rows(j)) * jnp.exp(g_last(s, j) - g)).astype(BF16)
        vb = rd(3, s, rows(j)).astype(BF16)
        e_last = jnp.exp(g_last(s, j))
        for p in range(n_pairs):
            ps = slice(p * gw, (p + 1) * gw)
            if carried:
                st = st_ref[s, p]
                o_ref[s, rows(j), ps] = oi_ref[u, :, ps] + lax.dot_general(qs[:, ps], st.astype(BF16), NT_DIMS,
                                                                           preferred_element_type=F32)
                upd = lax.dot_general(vb[:, ps], ks[:, ps], TN_DIMS, preferred_element_type=F32)
                st_ref[s, p] = jnp.where(state_diag, e_last[:, ps] * st + upd, 0.0)
            else:
                st = jnp.concatenate([jnp.concatenate([zero] * g + [s0_ref[s, gsz * p + g]] + [zero] * (gsz - 1 - g), axis=1)
                                      for g in range(gsz)], axis=0)
                o_ref[s, rows(j), ps] = oi_ref[u, :, ps] + jnp.dot(qs[:, ps], st.astype(BF16),
                                                                   preferred_element_type=F32)
                upd = lax.dot_general(ks[:, ps], vb[:, ps], TN_DIMS, preferred_element_type=F32)
                new = e_cols[ps, s:s + 1] * st + upd
                for g in range(gsz):
                    sout_ref[s, gsz * p + g] = new[diag(g), diag(g)]

    if carried:
        @pl.when(c == pl.num_programs(1) - 1)
        def _():
            for s in range(n_seq):
                for p in range(n_pairs):
                    for g in range(gsz):
                        sout_ref[s, gsz * p + g] = st_ref[s, p, diag(g), diag(g)].T


def _hg_core(q, k, lf, v, s0, c_len, n_chunk, n_seq, t=None):
    batch = s0.shape[0]
    per_seq_inputs = q.ndim == 2
    t = t if per_seq_inputs else q.shape[1]
    d = q.shape[-1]
    n_heads = d // HG_DK
    rb = c_len * n_chunk
    nc = t // rb
    carried = t > c_len
    gsz = HG_GROUP if carried and n_heads % HG_GROUP == 0 else 2
    assert t % rb == 0 and batch % n_seq == 0 and n_heads % gsz == 0 and n_seq <= LANES
    rows = lambda b, c: (b, c, 0)
    state = lambda b, c: (b, 0, 0, 0)
    if per_seq_inputs:
        seq_rows = lambda s: (lambda b, c: ((b * n_seq + s) * nc + c, 0))
        in_specs = [pl.BlockSpec((rb, d), seq_rows(s)) for _ in range(4) for s in range(n_seq)]
        inputs = [a for a in (q, k, lf, v) for _ in range(n_seq)]
    else:
        in_specs = [pl.BlockSpec((n_seq, rb, d), rows)] * 4
        inputs = [q, k, lf, v]
    return pl.pallas_call(
        functools.partial(_hg_core_body, c_len=c_len, n_chunk=n_chunk, n_seq=n_seq, n_heads=n_heads, carried=carried,
                          per_seq_inputs=per_seq_inputs, gsz=gsz),
        grid=(batch // n_seq, nc),
        in_specs=in_specs + [pl.BlockSpec((n_seq, n_heads, HG_DK, HG_DK), state)],
        out_specs=[pl.BlockSpec((n_seq, rb, d), rows), pl.BlockSpec((n_seq, n_heads, HG_DK, HG_DK), state)],
        out_shape=[jax.ShapeDtypeStruct((batch, t, d), F32),
                   jax.ShapeDtypeStruct((batch, n_heads, HG_DK, HG_DK), F32)],
        scratch_shapes=[pltpu.VMEM((n_seq if carried else 1, n_heads // gsz, gsz * HG_DK, gsz * HG_DK), F32),
                        pltpu.VMEM((c_len, d), F32),
                        pltpu.VMEM((n_seq * n_chunk, c_len, d), F32)],
        compiler_params=_params("arbitrary", "arbitrary"),
        name=f"hgrn_core_{c_len}",
    )(*inputs, s0)


def _hg_out_router_body(op_ref, os_ref, gate_ref, x_ref, gn_ref, wo_ref, g_ref, wr_ref, br_ref,
                        x3_ref, ids_ref, wts_ref, wo_bf, *, npt, n_heads, n_exp):
    i = pl.program_id(0)

    @pl.when(i == 0)
    def _():
        wo_bf[...] = wo_ref[...].astype(BF16)

    dk = HG_DK
    o = jnp.where(i < npt, op_ref[...], os_ref[...])
    gn = gn_ref[...]
    normed = jnp.concatenate([_rms(o[:, h * dk:(h + 1) * dk], gn) for h in range(n_heads)], axis=1)
    y = (normed * gate_ref[...]).astype(BF16)
    x3 = x_ref[...] + jnp.dot(y, wo_bf[...], preferred_element_type=F32)
    x3_ref[...] = x3
    h4 = _rms(x3, g_ref[...])
    h_hi = h4.astype(BF16)
    h_lo = (h4 - h_hi.astype(F32)).astype(BF16)
    wr = wr_ref[...]
    w_hi = wr.astype(BF16)
    w_lo = (wr - w_hi.astype(F32)).astype(BF16)
    logits = (jnp.dot(h_hi, w_hi, preferred_element_type=F32)
              + (jnp.dot(h_lo, w_hi, preferred_element_type=F32) + jnp.dot(h_hi, w_lo, preferred_element_type=F32))
              + br_ref[...])
    lane = lax.broadcasted_iota(jnp.int32, logits.shape, 1)
    logits = jnp.where(lane < n_exp, logits, -jnp.inf)
    m1 = jnp.max(logits, axis=-1, keepdims=True)
    i1 = jnp.min(jnp.where(logits == m1, lane, LANES), axis=-1, keepdims=True)
    rest = jnp.where(lane == i1, -jnp.inf, logits)
    m2 = jnp.max(rest, axis=-1, keepdims=True)
    i2 = jnp.min(jnp.where(rest == m2, lane, LANES), axis=-1, keepdims=True)
    e2 = jnp.exp(m2 - m1)
    den = 1.0 + e2
    ids_ref[...] = jnp.where(lane == 0, i1, jnp.where(lane == 1, i2, 0))
    wts_ref[...] = jnp.where(lane == 0, 1.0 / den, jnp.where(lane == 1, e2 / den, 0.0))


def _hg_out_router(o_p, o_s, gate, x, gn, wo, g, w_router, b_router, n_exp):
    n, d = x.shape
    n_p, n_s = o_p.shape[0], o_s.shape[0]
    tm = _pick_tile(math.gcd(n_p, n_s), 512)
    npt = n_p // tm
    row = lambda i: (i, 0)
    fix = lambda i: (0, 0)
    rp, rs = _split_maps(npt)
    return pl.pallas_call(
        functools.partial(_hg_out_router_body, npt=npt, n_heads=d // HG_DK, n_exp=n_exp),
        grid=(n // tm,),
        in_specs=[pl.BlockSpec((tm, d), rp), pl.BlockSpec((tm, d), rs), pl.BlockSpec((tm, d), row),
                  pl.BlockSpec((tm, d), row),
                  pl.BlockSpec((1, HG_DK), fix), pl.BlockSpec(wo.shape, fix), pl.BlockSpec((1, d), fix),
                  pl.BlockSpec((d, LANES), fix), pl.BlockSpec((1, LANES), fix)],
        out_specs=[pl.BlockSpec((tm, d), row), pl.BlockSpec((tm, LANES), row), pl.BlockSpec((tm, LANES), row)],
        out_shape=[jax.ShapeDtypeStruct((n, d), F32), jax.ShapeDtypeStruct((n, LANES), jnp.int32),
                   jax.ShapeDtypeStruct((n, LANES), F32)],
        scratch_shapes=[pltpu.VMEM(wo.shape, BF16)],
        compiler_params=_params("arbitrary"),
        name="hgrn_out_router",
    )(o_p, o_s, gate, x, gn, wo, g, w_router, b_router)


def _moe_body(te_ref, nv_ref, first_ref, last_ref, rin0_ref, rin_next_ref, rout_prev_ref, rout_cur_ref,
              x_hbm, g_ref, wg_ref, wu_ref, wd_ref, y_hbm,
              xg_ref, h_ref, acc_ref, wg_res, wu_res, wd_res, sem_in, sem_out, *, tm, stream_steps):
    del te_ref
    i = pl.program_id(0)
    c = pl.program_id(1)
    slot = i % 2
    other = 1 - slot

    def row_in(buf, j, token):
        return pltpu.make_async_copy(x_hbm.at[pl.ds(token, 1)], xg_ref.at[buf, pl.ds(j, 1)], sem_in)

    def row_out(buf, j, dst):
        return pltpu.make_async_copy(acc_ref.at[buf, pl.ds(j, 1)], y_hbm.at[pl.ds(dst, 1)], sem_out)

    def all_in(buf):
        return pltpu.make_async_copy(x_hbm.at[pl.ds(0, tm)], xg_ref.at[buf], sem_in)

    def all_out(buf):
        return pltpu.make_async_copy(acc_ref.at[buf], y_hbm.at[pl.ds(0, tm)], sem_out)

    def each_row(fn):
        def body(j, carry):
            fn(j)
            return carry
        lax.fori_loop(0, tm, body, 0, unroll=8)

    @pl.when(nv_ref[i] > 0)
    def _():
        @pl.when((i == 0) & (c == 0))
        def _():
            acc_ref[1] = jnp.zeros(acc_ref.shape[1:], F32)
            each_row(lambda j: row_in(0, j, rin0_ref[0, 0, j]).start())
            all_in(0).wait()

        @pl.when(c == 0)
        def _():
            h_ref[...] = _rms(xg_ref[slot], g_ref[...]).astype(BF16)
            acc_ref[slot] = jnp.zeros(acc_ref.shape[1:], F32)

        rows_per_step = tm // stream_steps

        def stream_rows():
            for u in range(rows_per_step):
                j = c * rows_per_step + u
                row_in(other, j, rin_next_ref[0, 0, j]).start()
                row_out(other, j, rout_prev_ref[0, 0, j]).start(priority=1)

        def chunk_step(first, stream):
            if stream:
                stream_rows()
            if first:
                wg_res[c] = wg_ref[...].astype(BF16)
                wu_res[c] = wu_ref[...].astype(BF16)
                wd_res[c] = wd_ref[...].astype(BF16)
            _swiglu_step(h_ref[...], wg_res[c], wu_res[c], wd_res[c], acc_ref.at[slot])

        streaming = c < stream_steps
        for first in (True, False):
            for stream in (True, False):
                pl.when(((first_ref[i] > 0) == first) & (streaming == stream))(
                    functools.partial(chunk_step, first, stream))

        @pl.when(c == pl.num_programs(1) - 1)
        def _():
            all_in(other).wait()
            all_out(other).wait()

            @pl.when(i == last_ref[0])
            def _():
                each_row(lambda j: row_out(slot, j, rout_cur_ref[0, 0, j]).start())
                all_out(slot).wait()


MOE_TILE_ROWS = 672
BF16_SUBLANES = 16


def _moe_experts(x, g, ids, w_gu, w_down):
    n, d = x.shape
    n_exp, d_ff = w_down.shape[0], w_down.shape[1]
    tf = _ff_chunk(d_ff)
    nc = d_ff // tf
    stream_steps = max(nc - 1, 1)
    unit = BF16_SUBLANES * stream_steps
    tm = unit * max(1, round(MOE_TILE_ROWS / unit))
    n_asg = n * TOP_K
    n_tiles = (n_asg + n_exp * (tm - 1) + tm - 1) // tm

    e_flat = ids.T.reshape(n_asg)
    onehot = (e_flat[:, None] == jnp.arange(n_exp, dtype=jnp.int32)[None, :]).astype(jnp.int32)
    counts = jnp.sum(onehot, axis=0)
    rank = jnp.sum((jnp.cumsum(onehot, axis=0) - onehot) * onehot, axis=1)
    tiles_per = (counts + tm - 1) // tm
    tile_end = jnp.cumsum(tiles_per)
    tile_start = tile_end - tiles_per
    slot = jnp.sum(onehot * tile_start[None, :], axis=1) * tm + rank
    asg = jnp.full((n_tiles * tm,), -1, jnp.int32).at[slot].set(jnp.arange(n_asg, dtype=jnp.int32),
                                                                 unique_indices=True)
    pad_row = n_asg + jnp.arange(n_tiles * tm, dtype=jnp.int32) % tm
    rows_in = jnp.where(asg >= 0, jnp.where(asg >= n, asg - n, asg), 0)
    rows_out = jnp.where(asg >= 0, asg, pad_row)
    rows_in_ext = jnp.concatenate([rows_in, jnp.zeros((tm,), jnp.int32)]).reshape(n_tiles + 1, 1, tm)
    rows_out_ext = jnp.concatenate([pad_row[:tm], rows_out]).reshape(n_tiles + 1, 1, tm)
    tile = jnp.arange(n_tiles, dtype=jnp.int32)
    last = tile_end[-1] - 1
    tile_c = jnp.minimum(tile, last)
    te = jnp.minimum(jnp.sum((tile_c[:, None] >= tile_end[None, :]).astype(jnp.int32), axis=1), n_exp - 1)
    te_hot = (te[:, None] == jnp.arange(n_exp, dtype=jnp.int32)[None, :]).astype(jnp.int32)
    cnt_t = jnp.sum(te_hot * counts[None, :], axis=1)
    start_t = jnp.sum(te_hot * tile_start[None, :], axis=1)
    nv = jnp.where(tile <= last, jnp.clip(cnt_t - (tile - start_t) * tm, 0, tm), 0).astype(jnp.int32)

    first = ((tile == start_t) & (nv > 0)).astype(jnp.int32)

    def chunk(i, c, nv_ref, first_ref):
        return jnp.where((nv_ref[i] > 0) & (first_ref[i] > 0), c, nc - 1)

    smem_rows = lambda index: pl.BlockSpec((1, 1, tm), index, memory_space=pltpu.SMEM)
    return pl.pallas_call(
        functools.partial(_moe_body, tm=tm, stream_steps=stream_steps),
        grid_spec=pltpu.PrefetchScalarGridSpec(
            num_scalar_prefetch=4,
            grid=(n_tiles, nc),
            in_specs=[smem_rows(lambda i, c, te_ref, nv_ref, first_ref, last_ref: (0, 0, 0)),
                      smem_rows(lambda i, c, te_ref, nv_ref, first_ref, last_ref: (i + 1, 0, 0)),
                      smem_rows(lambda i, c, te_ref, nv_ref, first_ref, last_ref: (i, 0, 0)),
                      smem_rows(lambda i, c, te_ref, nv_ref, first_ref, last_ref: (i + 1, 0, 0)),
                      pl.BlockSpec(memory_space=pl.ANY),
                      pl.BlockSpec((1, d), lambda i, c, te_ref, nv_ref, first_ref, last_ref: (0, 0)),
                      pl.BlockSpec((None, d, tf), lambda i, c, te_ref, nv_ref, first_ref, last_ref:
                                   (te_ref[i], 0, chunk(i, c, nv_ref, first_ref))),
                      pl.BlockSpec((None, d, tf), lambda i, c, te_ref, nv_ref, first_ref, last_ref:
                                   (te_ref[i], 0, chunk(i, c, nv_ref, first_ref) + nc)),
                      pl.BlockSpec((None, tf, d), lambda i, c, te_ref, nv_ref, first_ref, last_ref:
                                   (te_ref[i], chunk(i, c, nv_ref, first_ref), 0))],
            out_specs=pl.BlockSpec(memory_space=pl.ANY),
            scratch_shapes=[pltpu.VMEM((2, tm, d), F32), pltpu.VMEM((tm, d), BF16), pltpu.VMEM((2, tm, d), F32),
                            pltpu.VMEM((nc, d, tf), BF16), pltpu.VMEM((nc, d, tf), BF16), pltpu.VMEM((nc, tf, d), BF16),
                            pltpu.SemaphoreType.DMA, pltpu.SemaphoreType.DMA]),
        out_shape=jax.ShapeDtypeStruct((n_asg + tm, d), F32),
        compiler_params=_params("arbitrary", "arbitrary"),
        name="moe_experts",
    )(te.astype(jnp.int32), nv, first, last.reshape(1).astype(jnp.int32), rows_in_ext, rows_in_ext, rows_out_ext,
      rows_out_ext, x, g, w_gu, w_gu, w_down)


def _combine_body(x_ref, y0_ref, y1_ref, wts_ref, g_ref, outp_ref, outs_ref, *, npt):
    wts = wts_ref[...]
    moe = wts[:, 0:1] * y0_ref[...] + wts[:, 1:2] * y1_ref[...]
    _store_split(pl.program_id(0) < npt, outp_ref, outs_ref, _rms(x_ref[...] + moe, g_ref[...]))


def _combine(x, y2, wts, g, n_p):
    n, d = x.shape
    n_s = n - n_p
    tm = _pick_tile(math.gcd(n_p, n_s), 512)
    nt = n // tm
    npt = n_p // tm
    row = lambda i: (i, 0)
    rp, rs = _split_maps(npt)
    return pl.pallas_call(
        functools.partial(_combine_body, npt=npt),
        grid=(nt,),
        in_specs=[pl.BlockSpec((tm, d), row), pl.BlockSpec((tm, d), row), pl.BlockSpec((tm, d), lambda i: (i + nt, 0)),
                  pl.BlockSpec((tm, LANES), row), pl.BlockSpec((1, d), lambda i: (0, 0))],
        out_specs=[pl.BlockSpec((tm, d), rp), pl.BlockSpec((tm, d), rs)],
        out_shape=[jax.ShapeDtypeStruct((n_p, d), F32), jax.ShapeDtypeStruct((n_s, d), F32)],
        compiler_params=_params("arbitrary"),
        name="moe_combine",
    )(x, y2, y2, wts, g)


def _rope_tables(pos, hd):
    half = hd // 2
    inv = ROPE_THETA ** (-jnp.arange(half, dtype=F32) / half)
    ang = pos.astype(F32)[:, None] * inv[None, :]
    cos = jnp.cos(ang)
    sin = jnp.sin(ang)
    reps = LANES // hd
    return jnp.tile(jnp.concatenate([cos, cos], axis=1), (1, reps)), jnp.tile(jnp.concatenate([-sin, sin], axis=1), (1, reps))


def kernel(x_prompt, x_sample, cache_k_win, cache_v_win, state_hgrn, norm_mix, norm_ffn, norm_final,
           w_qkv, b_qkv, w_o_attn, b_o_attn, sinks, w_in_hg, hg_lower, hg_norm, w_o_hg,
           w_gu_dense, w_down_dense, w_router, b_router, w_gu_moe, w_down_moe):
    batch, seq, d = x_prompt.shape
    n_seq, t_dec, _ = x_sample.shape
    n_kv, hd = cache_k_win.shape[3], cache_k_win.shape[4]
    n_heads = sinks.shape[1]
    g_per = n_heads // n_kv
    n_q, n_k = n_heads * hd, n_kv * hd
    n_p, n_s = batch * seq, n_seq * t_dec
    n_exp = w_router.shape[2]
    wc = cache_k_win.shape[2]
    assert norm_mix.shape[0] == 2 and hd * 2 == LANES and d % HG_DK == 0 and seq % WINDOW == 0 and wc == WINDOW

    tile_q = _pick_tile(math.gcd(n_p, n_s), 512)
    assert tile_q % t_dec == 0
    rope_p = _rope_tables(jnp.arange(seq), hd)
    rope_s = _rope_tables(jnp.tile(PAST_LEN + jnp.arange(t_dec), tile_q // t_dec), hd)
    x, q, k, v = _qkv_rope(x_prompt.reshape(n_p, d), x_sample.reshape(n_s, d), norm_mix[0:1], w_qkv[0], b_qkv[0:1],
                           rope_p, rope_s, n_q, n_k, hd)
    o = _swa_prompt(q, k, v, sinks[0], batch, seq, n_kv, g_per, hd)
    kc = cache_k_win[0].reshape(n_seq, wc, n_k)
    vc = cache_v_win[0].reshape(n_seq, wc, n_k)
    o = _swa_sample(q, k, v, kc, vc, sinks[0], o, n_p, n_seq, t_dec, n_kv, g_per, hd)
    last_win = lambda a: jnp.stack([a[(b + 1) * seq - wc:(b + 1) * seq] for b in range(batch)]).reshape(batch, wc, n_kv, hd)
    k_win_p, v_win_p = last_win(k), last_win(v)
    k_win_s = jnp.concatenate([cache_k_win[0][:, t_dec:], k[n_p:].reshape(n_seq, t_dec, n_kv, hd)], axis=1)
    v_win_s = jnp.concatenate([cache_v_win[0][:, t_dec:], v[n_p:].reshape(n_seq, t_dec, n_kv, hd)], axis=1)
    x = _oproj_ffn(o, x, w_o_attn[0], b_o_attn[0:1], norm_ffn[0:1], w_gu_dense[0], w_down_dense[0])

    lb_sm = jax.nn.softmax(hg_lower.astype(F32), axis=0)
    lb = (jnp.cumsum(lb_sm, axis=0) - lb_sm[0])[1:2]
    in_half = functools.partial(_hg_inproj_half, x, norm_mix[1:2], w_in_hg[0], lb)
    hq, hlf, hk = in_half(0)
    hv, hgate = in_half(1)
    n_hh = d // HG_DK
    c_p = math.gcd(seq, HG_CHUNK)
    o_p, s_p = _hg_core(hq, hk, hlf, hv, jnp.zeros((batch, n_hh, HG_DK, HG_DK), F32),
                        c_p, 1, 2 if batch % 2 == 0 else 1, t=seq)
    c_s = SUBLANES * ((t_dec + SUBLANES - 1) // SUBLANES)
    pad = lambda a: jnp.pad(a[n_p:].reshape(n_seq, t_dec, d), ((0, 0), (0, c_s - t_dec), (0, 0)))
    o_s, s_s = _hg_core(pad(hq), pad(hk), pad(hlf), pad(hv), state_hgrn[0], c_s, 1,
                        next(s for s in (16, 8, 4, 2, 1) if n_seq % s == 0))
    o_s = o_s[:, :t_dec].reshape(n_s, d)

    wr = jnp.pad(w_router[0], ((0, 0), (0, LANES - n_exp)))
    br = jnp.pad(b_router[0:1], ((0, 0), (0, LANES - n_exp)))
    x3, ids, wts = _hg_out_router(o_p.reshape(n_p, d), o_s, hgate, x, hg_norm[0:1], w_o_hg[0], norm_ffn[1:2],
                                  wr, br, n_exp)
    y2 = _moe_experts(x3, norm_ffn[1:2], ids[:, :TOP_K], w_gu_moe[0], w_down_moe[0])
    y_p, y_s = _combine(x3, y2, wts, norm_final.reshape(1, d), n_p)

    return (y_p.reshape(batch, seq, d), y_s.reshape(n_seq, t_dec, d),
            k_win_p[None], v_win_p[None], k_win_s[None], v_win_s[None], s_p[None], s_s[None])
```

```python
import functools
import math

import jax
import jax.numpy as jnp
from jax import lax
from jax.experimental import pallas as pl
from jax.experimental.pallas import tpu as pltpu

F32 = jnp.float32
BF16 = jnp.bfloat16

NORM_EPS = 1e-5
WINDOW = 128
PAST_LEN = 16384
ROPE_THETA = 10000.0
HG_DK = 128
HG_CHUNK = 256
HG_GROUP = 2
HG_SAFE_DECAY = 60.0
HG_SAFE_Q = 1e9
TOP_K = 2
LANES = 128
SUBLANES = 8
VMEM_LIMIT = 56 * 1024 * 1024

NT_DIMS = (((1,), (1,)), ((), ()))
TN_DIMS = (((0,), (0,)), ((), ()))


def _pick_tile(n, target):
    for t in (1536, 1024, 768, 512, 384, 256, 192, 128, 64, 32, 16, 8):
        if t <= target and n % t == 0:
            return t
    raise ValueError(f"no row tile for {n}")


def _params(*sem):
    return pltpu.CompilerParams(dimension_semantics=sem, vmem_limit_bytes=VMEM_LIMIT)


def _rms(x, g):
    return x * lax.rsqrt(jnp.mean(x * x, axis=-1, keepdims=True) + NORM_EPS) * g


def _split_maps(npt):
    return (lambda i, *_: (jnp.minimum(i, npt - 1), 0)), (lambda i, *_: (jnp.maximum(i - npt, 0), 0))


def _store_split(is_prompt, ref_p, ref_s, val):
    @pl.when(is_prompt)
    def _():
        ref_p[...] = val

    @pl.when(jnp.logical_not(is_prompt))
    def _():
        ref_s[...] = val


def _qkv_body(xp_ref, xs_ref, g_ref, w_ref, b_ref, cosp_ref, sinp_ref, coss_ref, sins_ref,
              x_ref, q_ref, k_ref, v_ref, wbf_ref,
              *, npt, n_q, n_k, hd):
    i = pl.program_id(0)

    @pl.when(i == 0)
    def _():
        wbf_ref[...] = w_ref[...].astype(BF16)

    x = jnp.where(i < npt, xp_ref[...], xs_ref[...])
    x_ref[...] = x
    h = _rms(x, g_ref[...]).astype(BF16)
    y = jnp.dot(h, wbf_ref[...], preferred_element_type=F32) + b_ref[...]
    cos = jnp.where(i < npt, cosp_ref[...], coss_ref[...])
    sin = jnp.where(i < npt, sinp_ref[...], sins_ref[...])
    lane = lax.broadcasted_iota(jnp.int32, cos.shape, 1)
    first = (lane % hd) < (hd // 2)

    def rope(blk):
        partner = jnp.where(first, pltpu.roll(blk, LANES - hd // 2, 1), pltpu.roll(blk, hd // 2, 1))
        return blk * cos + partner * sin

    scale = hd ** -0.5
    for j in range(n_q // LANES):
        q_ref[:, j * LANES:(j + 1) * LANES] = (rope(y[:, j * LANES:(j + 1) * LANES]) * scale).astype(BF16)
    for j in range(n_k // LANES):
        k_ref[:, j * LANES:(j + 1) * LANES] = rope(y[:, n_q + j * LANES:n_q + (j + 1) * LANES])
    v_ref[...] = y[:, n_q + n_k:]


def _qkv_rope(xp, xs, g, w, b, rope_p, rope_s, n_q, n_k, hd):
    (n_p, d), n_s = xp.shape, xs.shape[0]
    n = n_p + n_s
    n_out = w.shape[1]
    tm = _pick_tile(math.gcd(n_p, n_s), 512)
    npt = n_p // tm
    row = lambda i: (i, 0)
    fix = lambda i: (0, 0)
    rp, rs = _split_maps(npt)
    seq_tiles = rope_p[0].shape[0] // tm
    assert rope_p[0].shape[0] % tm == 0 and rope_s[0].shape[0] == tm
    pos_p = lambda i: (jnp.minimum(i, npt - 1) % seq_tiles, 0)
    return pl.pallas_call(
        functools.partial(_qkv_body, npt=npt, n_q=n_q, n_k=n_k, hd=hd),
        grid=(n // tm,),
        in_specs=[pl.BlockSpec((tm, d), rp), pl.BlockSpec((tm, d), rs), pl.BlockSpec((1, d), fix),
                  pl.BlockSpec((d, n_out), fix), pl.BlockSpec((1, n_out), fix),
                  pl.BlockSpec((tm, LANES), pos_p), pl.BlockSpec((tm, LANES), pos_p),
                  pl.BlockSpec((tm, LANES), fix), pl.BlockSpec((tm, LANES), fix)],
        out_specs=[pl.BlockSpec((tm, d), row), pl.BlockSpec((tm, n_q), row), pl.BlockSpec((tm, n_k), row),
                   pl.BlockSpec((tm, n_k), row)],
        out_shape=[jax.ShapeDtypeStruct((n, d), F32), jax.ShapeDtypeStruct((n, n_q), BF16),
                   jax.ShapeDtypeStruct((n, n_k), F32), jax.ShapeDtypeStruct((n, n_k), F32)],
        scratch_shapes=[pltpu.VMEM((d, n_out), BF16)],
        compiler_params=_params("arbitrary"),
        name="qkv_rope",
    )(xp, xs, g, w, b, *rope_p, *rope_s)


def _sink_column(sink_ref, kh, g_per, rows_per):
    blk = lax.broadcasted_iota(jnp.int32, (g_per * rows_per, 1), 0) // rows_per
    col = jnp.full((g_per * rows_per, 1), sink_ref[kh * g_per], F32)
    for g in range(1, g_per):
        col = jnp.where(blk == g, sink_ref[kh * g_per + g], col)
    return col


def _stack_heads(q, kh, g_per, hd):
    return jnp.concatenate([q[:, (kh * g_per + g) * hd:(kh * g_per + g + 1) * hd] for g in range(g_per)], axis=0)


def _swa_prompt_body(sink_ref, q_ref, kp_ref, kc_ref, vp_ref, vc_ref, o_ref, *, n_kv, g_per, hd, w, nb, n_blocks):
    blk = pl.program_id(0)

    @pl.when(blk >= n_blocks)
    def _():
        o_ref[...] = jnp.zeros(o_ref.shape, o_ref.dtype)

    @pl.when(blk < n_blocks)
    def _():
        _swa_prompt_block(sink_ref, q_ref, kp_ref, kc_ref, vp_ref, vc_ref, o_ref, blk % nb,
                          n_kv=n_kv, g_per=g_per, hd=hd, w=w)


def _swa_prompt_block(sink_ref, q_ref, kp_ref, kc_ref, vp_ref, vc_ref, o_ref, n, *, n_kv, g_per, hd, w):
    q = q_ref[...]
    kk = jnp.concatenate([kp_ref[...], kc_ref[...]], axis=0).astype(BF16)
    vv = jnp.concatenate([vp_ref[...], vc_ref[...]], axis=0).astype(BF16)
    i = lax.broadcasted_iota(jnp.int32, (w, 2 * w), 0)
    j = lax.broadcasted_iota(jnp.int32, (w, 2 * w), 1)
    mask = (j > i) & (j <= i + w) & ((j >= w) | (n > 0))
    outs = []
    for h in range(n_kv * g_per):
        ks = slice((h // g_per) * hd, (h // g_per + 1) * hd)
        s = lax.dot_general(q[:, h * hd:(h + 1) * hd], kk[:, ks], NT_DIMS, preferred_element_type=F32)
        s = jnp.where(mask, s, -jnp.inf)
        sink = sink_ref[h]
        m = jnp.maximum(jnp.max(s, axis=-1, keepdims=True), sink)
        p = jnp.exp(s - m)
        p = p / (jnp.sum(p, axis=-1, keepdims=True) + jnp.exp(sink - m))
        outs.append(jnp.dot(p.astype(BF16), vv[:, ks], preferred_element_type=F32))
    o_ref[...] = jnp.concatenate(outs, axis=1).astype(BF16)


def _swa_prompt(q, k, v, sinks, batch, seq, n_kv, g_per, hd):
    w = WINDOW
    nb = seq // w
    n, dq = q.shape
    dkv = k.shape[1]
    cur = lambda i: (i, 0)
    prev = lambda i: (jnp.maximum(i - 1, 0), 0)
    return pl.pallas_call(
        functools.partial(_swa_prompt_body, n_kv=n_kv, g_per=g_per, hd=hd, w=w, nb=nb, n_blocks=batch * nb),
        grid=(pl.cdiv(n, w),),
        in_specs=[pl.BlockSpec(memory_space=pltpu.SMEM),
                  pl.BlockSpec((w, dq), cur), pl.BlockSpec((w, dkv), prev), pl.BlockSpec((w, dkv), cur),
                  pl.BlockSpec((w, dkv), prev), pl.BlockSpec((w, dkv), cur)],
        out_specs=pl.BlockSpec((w, dq), cur),
        out_shape=jax.ShapeDtypeStruct((n, dq), BF16),
        compiler_params=_params("arbitrary"),
        name="swa_prompt",
    )(sinks, q, k, k, v, v)


def _swa_sample_body(sink_ref, q_ref, kn_ref, vn_ref, kc_ref, vc_ref, o_all_ref, o_ref, *, n_kv, g_per, hd, bt, t, wc):
    del o_all_ref
    r = bt * t
    q = q_ref[...]
    kn = kn_ref[...].astype(BF16)
    vn = vn_ref[...].astype(BF16)
    kc = kc_ref[...].reshape(bt * wc, n_kv * hd).astype(BF16)
    vc = vc_ref[...].reshape(bt * wc, n_kv * hd).astype(BF16)
    row_c = lax.broadcasted_iota(jnp.int32, (g_per * r, bt * wc), 0) % r
    col_c = lax.broadcasted_iota(jnp.int32, (g_per * r, bt * wc), 1)
    mask_c = (col_c // wc == row_c // t) & (col_c % wc > row_c % t + (wc - WINDOW))
    row_n = lax.broadcasted_iota(jnp.int32, (g_per * r, r), 0) % r
    col_n = lax.broadcasted_iota(jnp.int32, (g_per * r, r), 1)
    mask_n = (col_n // t == row_n // t) & (col_n % t <= row_n % t)
    outs = []
    for kh in range(n_kv):
        hs = slice(kh * hd, (kh + 1) * hd)
        q4 = _stack_heads(q, kh, g_per, hd)
        sc = jnp.where(mask_c, lax.dot_general(q4, kc[:, hs], NT_DIMS, preferred_element_type=F32), -jnp.inf)
        sn = jnp.where(mask_n, lax.dot_general(q4, kn[:, hs], NT_DIMS, preferred_element_type=F32), -jnp.inf)
        sink = _sink_column(sink_ref, kh, g_per, r)
        m = jnp.maximum(jnp.maximum(jnp.max(sc, axis=-1, keepdims=True), jnp.max(sn, axis=-1, keepdims=True)), sink)
        pc = jnp.exp(sc - m)
        pn = jnp.exp(sn - m)
        den = jnp.sum(pc, axis=-1, keepdims=True) + jnp.sum(pn, axis=-1, keepdims=True) + jnp.exp(sink - m)
        o4 = (jnp.dot((pc / den).astype(BF16), vc[:, hs], preferred_element_type=F32)
              + jnp.dot((pn / den).astype(BF16), vn[:, hs], preferred_element_type=F32))
        outs.append(jnp.concatenate([o4[g * r:(g + 1) * r] for g in range(g_per)], axis=1))
    o_ref[...] = jnp.concatenate(outs, axis=1).astype(BF16)


def _swa_sample(q, k, v, k_cache, v_cache, sinks, o_all, row0, n_seq, t, n_kv, g_per, hd):
    wc = k_cache.shape[1]
    dq = q.shape[1]
    dkv = k.shape[1]
    bt = 8 if n_seq % 8 == 0 else n_seq
    r = bt * t
    assert row0 % r == 0
    off = row0 // r
    rows = lambda i: (off + i, 0)
    return pl.pallas_call(
        functools.partial(_swa_sample_body, n_kv=n_kv, g_per=g_per, hd=hd, bt=bt, t=t, wc=wc),
        grid=(n_seq // bt,),
        in_specs=[pl.BlockSpec(memory_space=pltpu.SMEM),
                  pl.BlockSpec((r, dq), rows), pl.BlockSpec((r, dkv), rows), pl.BlockSpec((r, dkv), rows),
                  pl.BlockSpec((bt, wc, dkv), lambda i: (i, 0, 0)), pl.BlockSpec((bt, wc, dkv), lambda i: (i, 0, 0)),
                  pl.BlockSpec(memory_space=pl.ANY)],
        out_specs=pl.BlockSpec((r, dq), rows),
        out_shape=jax.ShapeDtypeStruct(o_all.shape, o_all.dtype),
        input_output_aliases={6: 0},
        compiler_params=_params("arbitrary"),
        name="swa_sample",
    )(sinks, q, k, v, k_cache, v_cache, o_all)


def _swiglu_step(h, wg, wu, wd, acc_ref):
    a = jnp.dot(h, wg, preferred_element_type=F32)
    b = jnp.dot(h, wu, preferred_element_type=F32)
    act = (a * jax.nn.sigmoid(a) * b).astype(BF16)
    acc_ref[...] += jnp.dot(act, wd, preferred_element_type=F32)


def _oproj_ffn_body(o_ref, x_ref, wo_ref, bo_ref, g_ref, wg_ref, wu_ref, wd_ref, out_ref,
                    wo_bf, x1_ref, h_ref, acc_ref):
    i = pl.program_id(0)
    c = pl.program_id(1)

    @pl.when((i == 0) & (c == 0))
    def _():
        wo_bf[...] = wo_ref[...].astype(BF16)

    @pl.when(c == 0)
    def _():
        x1 = x_ref[...] + jnp.dot(o_ref[...], wo_bf[...], preferred_element_type=F32) + bo_ref[...]
        x1_ref[...] = x1
        h_ref[...] = _rms(x1, g_ref[...]).astype(BF16)
        acc_ref[...] = jnp.zeros_like(acc_ref)

    _swiglu_step(h_ref[...], wg_ref[...].astype(BF16), wu_ref[...].astype(BF16), wd_ref[...].astype(BF16), acc_ref)

    @pl.when(c == pl.num_programs(1) - 1)
    def _():
        out_ref[...] = x1_ref[...] + acc_ref[...]


def _ff_chunk(d_ff):
    for tf in (512, 256, 128):
        if d_ff % tf == 0:
            return tf
    raise ValueError(f"d_ff {d_ff} is not a multiple of {LANES}")


def _oproj_ffn(o, x, wo, bo, g, w_gu, w_down):
    n, d = x.shape
    d_ff = w_down.shape[0]
    tf = _ff_chunk(d_ff)
    nc = d_ff // tf
    tm = _pick_tile(n, 768)
    row = lambda i, c: (i, 0)
    fix = lambda i, c: (0, 0)
    return pl.pallas_call(
        _oproj_ffn_body,
        grid=(n // tm, nc),
        in_specs=[pl.BlockSpec((tm, o.shape[1]), row), pl.BlockSpec((tm, d), row),
                  pl.BlockSpec(wo.shape, fix), pl.BlockSpec((1, d), fix), pl.BlockSpec((1, d), fix),
                  pl.BlockSpec((d, tf), lambda i, c: (0, c)), pl.BlockSpec((d, tf), lambda i, c: (0, c + nc)),
                  pl.BlockSpec((tf, d), lambda i, c: (c, 0))],
        out_specs=pl.BlockSpec((tm, d), row),
        out_shape=jax.ShapeDtypeStruct((n, d), F32),
        scratch_shapes=[pltpu.VMEM(wo.shape, BF16), pltpu.VMEM((tm, d), F32), pltpu.VMEM((tm, d), BF16),
                        pltpu.VMEM((tm, d), F32)],
        compiler_params=_params("arbitrary", "arbitrary"),
        name="oproj_ffn",
    )(o, x, wo, bo, g, w_gu, w_gu, w_down)


def _hg_in_body(x_ref, g_ref, w_ref, lb_ref, *refs, half):
    out_refs, wbf_ref = refs[:-1], refs[-1]
    d = x_ref.shape[1]

    @pl.when(pl.program_id(0) == 0)
    def _():
        wbf_ref[...] = w_ref[...].astype(BF16)

    h = _rms(x_ref[...], g_ref[...]).astype(BF16)
    z = jnp.dot(h, wbf_ref[...], preferred_element_type=F32)
    z0, z1 = z[:, :d], z[:, d:]
    if half == 0:
        out_refs[0][...] = z0 * jax.nn.sigmoid(z0) * (HG_DK ** -0.5)
        lb = lb_ref[...]
        t = jnp.exp(-jnp.abs(z1))
        log_sig = jnp.minimum(z1, 0.0) - jnp.log(1.0 + t)
        a = jnp.log(lb)
        b = jnp.log1p(-lb) + log_sig
        out_refs[1][...] = jnp.maximum(a, b) + jnp.log(1.0 + jnp.exp(-jnp.abs(a - b)))
        out_refs[2][...] = (1.0 - lb) * (jnp.where(z1 >= 0.0, t, 1.0) / (1.0 + t))
    else:
        out_refs[0][...] = z0
        out_refs[1][...] = z1 * jax.nn.sigmoid(z1)


def _hg_inproj_half(x, g, w_in, lb, half):
    n, d = x.shape
    n_res = 3 if half == 0 else 2
    tm = _pick_tile(n, 512)
    row = lambda i: (i, 0)
    fix = lambda i: (0, 0)
    return pl.pallas_call(
        functools.partial(_hg_in_body, half=half),
        grid=(n // tm,),
        in_specs=[pl.BlockSpec((tm, d), row), pl.BlockSpec((1, d), fix), pl.BlockSpec((d, 2 * d), lambda i: (0, half)),
                  pl.BlockSpec((1, d), fix)],
        out_specs=[pl.BlockSpec((tm, d), row)] * n_res,
        out_shape=[jax.ShapeDtypeStruct((n, d), F32)] * n_res,
        scratch_shapes=[pltpu.VMEM((d, 2 * d), BF16)],
        compiler_params=_params("arbitrary"),
        name=f"hgrn_inproj_{half}",
    )(x, g, w_in, lb)


def _hg_core_body(*refs, c_len, n_chunk, n_seq, n_heads, carried, per_seq_inputs, gsz):
    n_in = n_seq if per_seq_inputs else 1
    in_refs, (s0_ref, o_ref, sout_ref, st_ref, g_ref, oi_ref) = refs[:4 * n_in], refs[4 * n_in:]

    def rd(kind, s, row_slice):
        if per_seq_inputs:
            return in_refs[kind * n_in + s][row_slice, :]
        return in_refs[kind][s, row_slice, :]

    c = pl.program_id(1)
    dk = HG_DK
    rb = c_len * n_chunk
    units = [(s, j) for s in range(n_seq) for j in range(n_chunk)]

    n_pairs = n_heads // gsz
    gw = gsz * dk
    state_diag = (lax.broadcasted_iota(jnp.int32, (gw, gw), 0) // dk == lax.broadcasted_iota(jnp.int32, (gw, gw), 1) // dk)
    rows_diag = (lax.broadcasted_iota(jnp.int32, (gsz * c_len, gw), 0) // c_len
                 == lax.broadcasted_iota(jnp.int32, (gsz * c_len, gw), 1) // dk)

    def pair_tile(x):
        x2 = jnp.concatenate([x] * gsz, axis=0)
        return jnp.where(rows_diag, x2, jnp.zeros_like(x2))

    def diag(g):
        return slice(g * dk, (g + 1) * dk)

    if carried:
        @pl.when(c == 0)
        def _():
            for s in range(n_seq):
                for p in range(n_pairs):
                    st_ref[s, p] = jnp.zeros((gw, gw), F32)
                    for g in range(gsz):
                        st_ref[s, p, diag(g), diag(g)] = s0_ref[s, gsz * p + g].T

    r = lax.broadcasted_iota(jnp.int32, (rb, rb), 0)
    cidx = lax.broadcasted_iota(jnp.int32, (rb, rb), 1)
    block_causal = ((r >= cidx) & (r // c_len == cidx // c_len)).astype(BF16)
    causal2 = (lax.broadcasted_iota(jnp.int32, (c_len, gsz * c_len), 0)
               >= lax.broadcasted_iota(jnp.int32, (c_len, gsz * c_len), 1) % c_len)

    def rows(j):
        return slice(j * c_len, (j + 1) * c_len)

    gcum, safe = [], None
    for s in range(n_seq):
        lf = rd(2, s, slice(None))
        lf_hi = lf.astype(BF16)
        rest = lf - lf_hi.astype(F32)
        lf_mid = rest.astype(BF16)
        lf_lo = (rest - lf_mid.astype(F32)).astype(BF16)
        gs = ((jnp.dot(block_causal, lf_lo, preferred_element_type=F32)
               + jnp.dot(block_causal, lf_mid, preferred_element_type=F32))
              + jnp.dot(block_causal, lf_hi, preferred_element_type=F32))
        gcum.append(gs)
        for j in range(n_chunk):
            mid = gs[j * c_len + c_len // 2 - 1:j * c_len + c_len // 2, :]
            ok = ((jnp.max(gs[j * c_len:j * c_len + 1, :] - mid) <= HG_SAFE_DECAY)
                  & (jnp.max(mid - gs[(j + 1) * c_len - 1:(j + 1) * c_len, :]) <= HG_SAFE_DECAY)
                  & (jnp.max(jnp.abs(rd(0, s, rows(j)))) <= HG_SAFE_Q))
            safe = ok if safe is None else (safe & ok)

    def g_mid(s, j):
        return gcum[s][j * c_len + c_len // 2 - 1:j * c_len + c_len // 2, :]

    def g_last(s, j):
        return gcum[s][(j + 1) * c_len - 1:(j + 1) * c_len, :]

    @pl.when(safe)
    def _():
        for u, (s, j) in enumerate(units):
            g = gcum[s][rows(j), :]
            qi = (rd(0, s, rows(j)) * jnp.exp(g - g_mid(s, j))).astype(BF16)
            ki = (rd(1, s, rows(j)) * jnp.exp(g_mid(s, j) - g)).astype(BF16)
            vb = rd(3, s, rows(j)).astype(BF16)
            for p in range(n_pairs):
                ps = slice(p * gw, (p + 1) * gw)
                a = lax.dot_general(qi[:, ps], pair_tile(ki[:, ps]), NT_DIMS, preferred_element_type=F32)
                a = jnp.where(causal2, a, 0.0).astype(BF16)
                oi_ref[u, :, ps] = jnp.dot(a, pair_tile(vb[:, ps]), preferred_element_type=F32)

    @pl.when(jnp.logical_not(safe))
    def _():
        lane_h = lax.broadcasted_iota(jnp.int32, (n_heads * dk, n_heads * dk), 0) // dk
        lane_w = lax.broadcasted_iota(jnp.int32, (n_heads * dk, n_heads * dk), 1) // dk
        head_sum = (lane_h == lane_w).astype(BF16)
        t_idx = lax.broadcasted_iota(jnp.int32, (c_len, 1), 0)
        for u, (s, j) in enumerate(units):
            g = gcum[s][rows(j), :]
            g_ref[...] = g
            q = rd(0, s, rows(j))

            def key_row(i, acc, s=s, j=j, g=g, q=q):
                gi = g_ref[pl.ds(i, 1), :]
                decay = jnp.exp(jnp.where(t_idx >= i, g - gi, -jnp.inf))
                term = (q * decay * rd(1, s, pl.ds(j * c_len + i, 1))).astype(BF16)
                a_i = jnp.dot(term, head_sum, preferred_element_type=F32)
                return acc + a_i * rd(3, s, pl.ds(j * c_len + i, 1))

            oi_ref[u] = lax.fori_loop(0, c_len, key_row, jnp.zeros((c_len, n_heads * dk), F32))

    if not carried:
        e_rows = [jnp.exp(g_last(s, 0)) for s in range(n_seq)]
        e_cols = jnp.concatenate(e_rows + [jnp.zeros((LANES - n_seq, n_heads * dk), F32)], axis=0).T
        zero = jnp.zeros((dk, dk), F32)

    for u, (s, j) in enumerate(units):
        g = gcum[s][rows(j), :]
        qs = (rd(0, s, rows(j)) * jnp.exp(g)).astype(BF16)
        ks = (rd(1, s, rows(j)) * jnp.exp(g_last(s, j) - g)).astype(BF16)
        vb = rd(3, s, rows(j)).astype(BF16)
        e_last = jnp.exp(g_last(s, j))
        for p in range(n_pairs):
            ps = slice(p * gw, (p + 1) * gw)
            if carried:
                st = st_ref[s, p]
                o_ref[s, rows(j), ps] = oi_ref[u, :, ps] + lax.dot_general(qs[:, ps], st.astype(BF16), NT_DIMS,
                                                                           preferred_element_type=F32)
                upd = lax.dot_general(vb[:, ps], ks[:, ps], TN_DIMS, preferred_element_type=F32)
                st_ref[s, p] = jnp.where(state_diag, e_last[:, ps] * st + upd, 0.0)
            else:
                st = jnp.concatenate([jnp.concatenate([zero] * g + [s0_ref[s, gsz * p + g]] + [zero] * (gsz - 1 - g), axis=1)
                                      for g in range(gsz)], axis=0)
                o_ref[s, rows(j), ps] = oi_ref[u, :, ps] + jnp.dot(qs[:, ps], st.astype(BF16),
                                                                   preferred_element_type=F32)
                upd = lax.dot_general(ks[:, ps], vb[:, ps], TN_DIMS, preferred_element_type=F32)
                new = e_cols[ps, s:s + 1] * st + upd
                for g in range(gsz):
                    sout_ref[s, gsz * p + g] = new[diag(g), diag(g)]

    if carried:
        @pl.when(c == pl.num_programs(1) - 1)
        def _():
            for s in range(n_seq):
                for p in range(n_pairs):
                    for g in range(gsz):
                        sout_ref[s, gsz * p + g] = st_ref[s, p, diag(g), diag(g)].T


def _hg_core(q, k, lf, v, s0, c_len, n_chunk, n_seq, t=None):
    batch = s0.shape[0]
    per_seq_inputs = q.ndim == 2
    t = t if per_seq_inputs else q.shape[1]
    d = q.shape[-1]
    n_heads = d // HG_DK
    rb = c_len * n_chunk
    nc = t // rb
    carried = t > c_len
    gsz = HG_GROUP if carried and n_heads % HG_GROUP == 0 else 2
    assert t % rb == 0 and batch % n_seq == 0 and n_heads % gsz == 0 and n_seq <= LANES
    rows = lambda b, c: (b, c, 0)
    state = lambda b, c: (b, 0, 0, 0)
    if per_seq_inputs:
        seq_rows = lambda s: (lambda b, c: ((b * n_seq + s) * nc + c, 0))
        in_specs = [pl.BlockSpec((rb, d), seq_rows(s)) for _ in range(4) for s in range(n_seq)]
        inputs = [a for a in (q, k, lf, v) for _ in range(n_seq)]
    else:
        in_specs = [pl.BlockSpec((n_seq, rb, d), rows)] * 4
        inputs = [q, k, lf, v]
    return pl.pallas_call(
        functools.partial(_hg_core_body, c_len=c_len, n_chunk=n_chunk, n_seq=n_seq, n_heads=n_heads, carried=carried,
                          per_seq_inputs=per_seq_inputs, gsz=gsz),
        grid=(batch // n_seq, nc),
        in_specs=in_specs + [pl.BlockSpec((n_seq, n_heads, HG_DK, HG_DK), state)],
        out_specs=[pl.BlockSpec((n_seq, rb, d), rows), pl.BlockSpec((n_seq, n_heads, HG_DK, HG_DK), state)],
        out_shape=[jax.ShapeDtypeStruct((batch, t, d), F32),
                   jax.ShapeDtypeStruct((batch, n_heads, HG_DK, HG_DK), F32)],
        scratch_shapes=[pltpu.VMEM((n_seq if carried else 1, n_heads // gsz, gsz * HG_DK, gsz * HG_DK), F32),
                        pltpu.VMEM((c_len, d), F32),
                        pltpu.VMEM((n_seq * n_chunk, c_len, d), F32)],
        compiler_params=_params("arbitrary", "arbitrary"),
        name=f"hgrn_core_{c_len}",
    )(*inputs, s0)


def _hg_out_router_body(op_ref, os_ref, gate_ref, x_ref, gn_ref, wo_ref, g_ref, wr_ref, br_ref,
                        x3_ref, ids_ref, wts_ref, wo_bf, *, npt, n_heads, n_exp):
    i = pl.program_id(0)

    @pl.when(i == 0)
    def _():
        wo_bf[...] = wo_ref[...].astype(BF16)

    dk = HG_DK
    o = jnp.where(i < npt, op_ref[...], os_ref[...])
    gn = gn_ref[...]
    normed = jnp.concatenate([_rms(o[:, h * dk:(h + 1) * dk], gn) for h in range(n_heads)], axis=1)
    y = (normed * gate_ref[...]).astype(BF16)
    x3 = x_ref[...] + jnp.dot(y, wo_bf[...], preferred_element_type=F32)
    x3_ref[...] = x3
    h4 = _rms(x3, g_ref[...])
    h_hi = h4.astype(BF16)
    h_lo = (h4 - h_hi.astype(F32)).astype(BF16)
    wr = wr_ref[...]
    w_hi = wr.astype(BF16)
    w_lo = (wr - w_hi.astype(F32)).astype(BF16)
    hh = jnp.dot(h_hi, jnp.concatenate([w_hi, w_lo], axis=1), preferred_element_type=F32)
    logits = (hh[:, :LANES] + (jnp.dot(h_lo, w_hi, preferred_element_type=F32) + hh[:, LANES:])) + br_ref[...]
    lane = lax.broadcasted_iota(jnp.int32, logits.shape, 1)
    logits = jnp.where(lane < n_exp, logits, -jnp.inf)
    m1 = jnp.max(logits, axis=-1, keepdims=True)
    i1 = jnp.min(jnp.where(logits == m1, lane, LANES), axis=-1, keepdims=True)
    rest = jnp.where(lane == i1, -jnp.inf, logits)
    m2 = jnp.max(rest, axis=-1, keepdims=True)
    i2 = jnp.min(jnp.where(rest == m2, lane, LANES), axis=-1, keepdims=True)
    e2 = jnp.exp(m2 - m1)
    den = 1.0 + e2
    ids_ref[...] = jnp.where(lane == 0, i1, jnp.where(lane == 1, i2, 0))
    wts_ref[...] = jnp.where(lane == 0, 1.0 / den, jnp.where(lane == 1, e2 / den, 0.0))


def _hg_out_router(o_p, o_s, gate, x, gn, wo, g, w_router, b_router, n_exp):
    n, d = x.shape
    n_p, n_s = o_p.shape[0], o_s.shape[0]
    tm = _pick_tile(math.gcd(n_p, n_s), 512)
    npt = n_p // tm
    row = lambda i: (i, 0)
    fix = lambda i: (0, 0)
    rp, rs = _split_maps(npt)
    return pl.pallas_call(
        functools.partial(_hg_out_router_body, npt=npt, n_heads=d // HG_DK, n_exp=n_exp),
        grid=(n // tm,),
        in_specs=[pl.BlockSpec((tm, d), rp), pl.BlockSpec((tm, d), rs), pl.BlockSpec((tm, d), row),
                  pl.BlockSpec((tm, d), row),
                  pl.BlockSpec((1, HG_DK), fix), pl.BlockSpec(wo.shape, fix), pl.BlockSpec((1, d), fix),
                  pl.BlockSpec((d, LANES), fix), pl.BlockSpec((1, LANES), fix)],
        out_specs=[pl.BlockSpec((tm, d), row), pl.BlockSpec((tm, LANES), row), pl.BlockSpec((tm, LANES), row)],
        out_shape=[jax.ShapeDtypeStruct((n, d), F32), jax.ShapeDtypeStruct((n, LANES), jnp.int32),
                   jax.ShapeDtypeStruct((n, LANES), F32)],
        scratch_shapes=[pltpu.VMEM(wo.shape, BF16)],
        compiler_params=_params("arbitrary"),
        name="hgrn_out_router",
    )(o_p, o_s, gate, x, gn, wo, g, w_router, b_router)


def _moe_body(te_ref, nv_ref, first_ref, last_ref, rin0_ref, rin_next_ref, rout_prev_ref, rout_cur_ref,
              x_hbm, g_ref, wg_ref, wu_ref, wd_ref, y_hbm,
              xg_ref, h_ref, acc_ref, wg_res, wu_res, wd_res, sem_in, sem_out, *, tm, stream_steps):
    del te_ref
    i = pl.program_id(0)
    c = pl.program_id(1)
    slot = i % 2
    other = 1 - slot

    def row_in(buf, j, token):
        return pltpu.make_async_copy(x_hbm.at[pl.ds(token, 1)], xg_ref.at[buf, pl.ds(j, 1)], sem_in)

    def row_out(buf, j, dst):
        return pltpu.make_async_copy(acc_ref.at[buf, pl.ds(j, 1)], y_hbm.at[pl.ds(dst, 1)], sem_out)

    def all_in(buf):
        return pltpu.make_async_copy(x_hbm.at[pl.ds(0, tm)], xg_ref.at[buf], sem_in)

    def all_out(buf):
        return pltpu.make_async_copy(acc_ref.at[buf], y_hbm.at[pl.ds(0, tm)], sem_out)

    def each_row(fn):
        def body(j, carry):
            fn(j)
            return carry
        lax.fori_loop(0, tm, body, 0, unroll=8)

    @pl.when(nv_ref[i] > 0)
    def _():
        @pl.when((i == 0) & (c == 0))
        def _():
            acc_ref[1] = jnp.zeros(acc_ref.shape[1:], F32)
            each_row(lambda j: row_in(0, j, rin0_ref[0, 0, j]).start())
            all_in(0).wait()

        @pl.when(c == 0)
        def _():
            h_ref[...] = _rms(xg_ref[slot], g_ref[...]).astype(BF16)
            acc_ref[slot] = jnp.zeros(acc_ref.shape[1:], F32)

        rows_per_step = tm // stream_steps

        def stream_rows():
            for u in range(rows_per_step):
                j = c * rows_per_step + u
                row_in(other, j, rin_next_ref[0, 0, j]).start()
                row_out(other, j, rout_prev_ref[0, 0, j]).start(priority=1)

        def chunk_step(first, stream):
            if stream:
                stream_rows()
            if first:
                wg_res[c] = wg_ref[...].astype(BF16)
                wu_res[c] = wu_ref[...].astype(BF16)
                wd_res[c] = wd_ref[...].astype(BF16)
            _swiglu_step(h_ref[...], wg_res[c], wu_res[c], wd_res[c], acc_ref.at[slot])

        streaming = c < stream_steps
        for first in (True, False):
            for stream in (True, False):
                pl.when(((first_ref[i] > 0) == first) & (streaming == stream))(
                    functools.partial(chunk_step, first, stream))

        @pl.when(c == pl.num_programs(1) - 1)
        def _():
            all_in(other).wait()
            all_out(other).wait()

            @pl.when(i == last_ref[0])
            def _():
                each_row(lambda j: row_out(slot, j, rout_cur_ref[0, 0, j]).start())
                all_out(slot).wait()


MOE_TILE_ROWS = 672
BF16_SUBLANES = 16


def _moe_experts(x, g, ids, w_gu, w_down):
    n, d = x.shape
    n_exp, d_ff = w_down.shape[0], w_down.shape[1]
    tf = _ff_chunk(d_ff)
    nc = d_ff // tf
    stream_steps = max(nc - 1, 1)
    unit = BF16_SUBLANES * stream_steps
    tm = unit * max(1, round(MOE_TILE_ROWS / unit))
    n_asg = n * TOP_K
    n_tiles = (n_asg + n_exp * (tm - 1) + tm - 1) // tm

    e_flat = ids.T.reshape(n_asg)
    onehot = (e_flat[:, None] == jnp.arange(n_exp, dtype=jnp.int32)[None, :]).astype(jnp.int32)
    counts = jnp.sum(onehot, axis=0)
    rank = jnp.sum((jnp.cumsum(onehot, axis=0) - onehot) * onehot, axis=1)
    tiles_per = (counts + tm - 1) // tm
    tile_end = jnp.cumsum(tiles_per)
    tile_start = tile_end - tiles_per
    slot = jnp.sum(onehot * tile_start[None, :], axis=1) * tm + rank
    asg = jnp.full((n_tiles * tm,), -1, jnp.int32).at[slot].set(jnp.arange(n_asg, dtype=jnp.int32),
                                                                 unique_indices=True)
    pad_row = n_asg + jnp.arange(n_tiles * tm, dtype=jnp.int32) % tm
    rows_in = jnp.where(asg >= 0, jnp.where(asg >= n, asg - n, asg), 0)
    rows_out = jnp.where(asg >= 0, asg, pad_row)
    rows_in_ext = jnp.concatenate([rows_in, jnp.zeros((tm,), jnp.int32)]).reshape(n_tiles + 1, 1, tm)
    rows_out_ext = jnp.concatenate([pad_row[:tm], rows_out]).reshape(n_tiles + 1, 1, tm)
    tile = jnp.arange(n_tiles, dtype=jnp.int32)
    last = tile_end[-1] - 1
    tile_c = jnp.minimum(tile, last)
    te = jnp.minimum(jnp.sum((tile_c[:, None] >= tile_end[None, :]).astype(jnp.int32), axis=1), n_exp - 1)
    te_hot = (te[:, None] == jnp.arange(n_exp, dtype=jnp.int32)[None, :]).astype(jnp.int32)
    cnt_t = jnp.sum(te_hot * counts[None, :], axis=1)
    start_t = jnp.sum(te_hot * tile_start[None, :], axis=1)
    nv = jnp.where(tile <= last, jnp.clip(cnt_t - (tile - start_t) * tm, 0, tm), 0).astype(jnp.int32)

    first = ((tile == start_t) & (nv > 0)).astype(jnp.int32)

    def chunk(i, c, nv_ref, first_ref):
        return jnp.where((nv_ref[i] > 0) & (first_ref[i] > 0), c, nc - 1)

    smem_rows = lambda index: pl.BlockSpec((1, 1, tm), index, memory_space=pltpu.SMEM)
    return pl.pallas_call(
        functools.partial(_moe_body, tm=tm, stream_steps=stream_steps),
        grid_spec=pltpu.PrefetchScalarGridSpec(
            num_scalar_prefetch=4,
            grid=(n_tiles, nc),
            in_specs=[smem_rows(lambda i, c, te_ref, nv_ref, first_ref, last_ref: (0, 0, 0)),
                      smem_rows(lambda i, c, te_ref, nv_ref, first_ref, last_ref: (i + 1, 0, 0)),
                      smem_rows(lambda i, c, te_ref, nv_ref, first_ref, last_ref: (i, 0, 0)),
                      smem_rows(lambda i, c, te_ref, nv_ref, first_ref, last_ref: (i + 1, 0, 0)),
                      pl.BlockSpec(memory_space=pl.ANY),
                      pl.BlockSpec((1, d), lambda i, c, te_ref, nv_ref, first_ref, last_ref: (0, 0)),
                      pl.BlockSpec((None, d, tf), lambda i, c, te_ref, nv_ref, first_ref, last_ref:
                                   (te_ref[i], 0, chunk(i, c, nv_ref, first_ref))),
                      pl.BlockSpec((None, d, tf), lambda i, c, te_ref, nv_ref, first_ref, last_ref:
                                   (te_ref[i], 0, chunk(i, c, nv_ref, first_ref) + nc)),
                      pl.BlockSpec((None, tf, d), lambda i, c, te_ref, nv_ref, first_ref, last_ref:
                                   (te_ref[i], chunk(i, c, nv_ref, first_ref), 0))],
            out_specs=pl.BlockSpec(memory_space=pl.ANY),
            scratch_shapes=[pltpu.VMEM((2, tm, d), F32), pltpu.VMEM((tm, d), BF16), pltpu.VMEM((2, tm, d), F32),
                            pltpu.VMEM((nc, d, tf), BF16), pltpu.VMEM((nc, d, tf), BF16), pltpu.VMEM((nc, tf, d), BF16),
                            pltpu.SemaphoreType.DMA, pltpu.SemaphoreType.DMA]),
        out_shape=jax.ShapeDtypeStruct((n_asg + tm, d), F32),
        compiler_params=_params("arbitrary", "arbitrary"),
        name="moe_experts",
    )(te.astype(jnp.int32), nv, first, last.reshape(1).astype(jnp.int32), rows_in_ext, rows_in_ext, rows_out_ext,
      rows_out_ext, x, g, w_gu, w_gu, w_down)


def _combine_body(x_ref, y0_ref, y1_ref, wts_ref, g_ref, outp_ref, outs_ref, *, npt):
    wts = wts_ref[...]
    moe = wts[:, 0:1] * y0_ref[...] + wts[:, 1:2] * y1_ref[...]
    _store_split(pl.program_id(0) < npt, outp_ref, outs_ref, _rms(x_ref[...] + moe, g_ref[...]))


def _combine(x, y2, wts, g, n_p):
    n, d = x.shape
    n_s = n - n_p
    tm = _pick_tile(math.gcd(n_p, n_s), 512)
    nt = n // tm
    npt = n_p // tm
    row = lambda i: (i, 0)
    rp, rs = _split_maps(npt)
    return pl.pallas_call(
        functools.partial(_combine_body, npt=npt),
        grid=(nt,),
        in_specs=[pl.BlockSpec((tm, d), row), pl.BlockSpec((tm, d), row), pl.BlockSpec((tm, d), lambda i: (i + nt, 0)),
                  pl.BlockSpec((tm, LANES), row), pl.BlockSpec((1, d), lambda i: (0, 0))],
        out_specs=[pl.BlockSpec((tm, d), rp), pl.BlockSpec((tm, d), rs)],
        out_shape=[jax.ShapeDtypeStruct((n_p, d), F32), jax.ShapeDtypeStruct((n_s, d), F32)],
        compiler_params=_params("arbitrary"),
        name="moe_combine",
    )(x, y2, y2, wts, g)


def _rope_tables(pos, hd):
    half = hd // 2
    inv = ROPE_THETA ** (-jnp.arange(half, dtype=F32) / half)
    ang = pos.astype(F32)[:, None] * inv[None, :]
    cos = jnp.cos(ang)
    sin = jnp.sin(ang)
    reps = LANES // hd
    return jnp.tile(jnp.concatenate([cos, cos], axis=1), (1, reps)), jnp.tile(jnp.concatenate([-sin, sin], axis=1), (1, reps))


def kernel(x_prompt, x_sample, cache_k_win, cache_v_win, state_hgrn, norm_mix, norm_ffn, norm_final,
           w_qkv, b_qkv, w_o_attn, b_o_attn, sinks, w_in_hg, hg_lower, hg_norm, w_o_hg,
           w_gu_dense, w_down_dense, w_router, b_router, w_gu_moe, w_down_moe):
    batch, seq, d = x_prompt.shape
    n_seq, t_dec, _ = x_sample.shape
    n_kv, hd = cache_k_win.shape[3], cache_k_win.shape[4]
    n_heads = sinks.shape[1]
    g_per = n_heads // n_kv
    n_q, n_k = n_heads * hd, n_kv * hd
    n_p, n_s = batch * seq, n_seq * t_dec
    n_exp = w_router.shape[2]
    wc = cache_k_win.shape[2]
    assert norm_mix.shape[0] == 2 and hd * 2 == LANES and d % HG_DK == 0 and seq % WINDOW == 0 and wc == WINDOW

    tile_q = _pick_tile(math.gcd(n_p, n_s), 512)
    assert tile_q % t_dec == 0
    rope_p = _rope_tables(jnp.arange(seq), hd)
    rope_s = _rope_tables(jnp.tile(PAST_LEN + jnp.arange(t_dec), tile_q // t_dec), hd)
    x, q, k, v = _qkv_rope(x_prompt.reshape(n_p, d), x_sample.reshape(n_s, d), norm_mix[0:1], w_qkv[0], b_qkv[0:1],
                           rope_p, rope_s, n_q, n_k, hd)
    o = _swa_prompt(q, k, v, sinks[0], batch, seq, n_kv, g_per, hd)
    kc = cache_k_win[0].reshape(n_seq, wc, n_k)
    vc = cache_v_win[0].reshape(n_seq, wc, n_k)
    o = _swa_sample(q, k, v, kc, vc, sinks[0], o, n_p, n_seq, t_dec, n_kv, g_per, hd)
    last_win = lambda a: jnp.stack([a[(b + 1) * seq - wc:(b + 1) * seq] for b in range(batch)]).reshape(batch, wc, n_kv, hd)
    k_win_p, v_win_p = last_win(k), last_win(v)
    k_win_s = jnp.concatenate([cache_k_win[0][:, t_dec:], k[n_p:].reshape(n_seq, t_dec, n_kv, hd)], axis=1)
    v_win_s = jnp.concatenate([cache_v_win[0][:, t_dec:], v[n_p:].reshape(n_seq, t_dec, n_kv, hd)], axis=1)
    x = _oproj_ffn(o, x, w_o_attn[0], b_o_attn[0:1], norm_ffn[0:1], w_gu_dense[0], w_down_dense[0])

    lb_sm = jax.nn.softmax(hg_lower.astype(F32), axis=0)
    lb = (jnp.cumsum(lb_sm, axis=0) - lb_sm[0])[1:2]
    in_half = functools.partial(_hg_inproj_half, x, norm_mix[1:2], w_in_hg[0], lb)
    hq, hlf, hk = in_half(0)
    hv, hgate = in_half(1)
    n_hh = d // HG_DK
    c_p = math.gcd(seq, HG_CHUNK)
    o_p, s_p = _hg_core(hq, hk, hlf, hv, jnp.zeros((batch, n_hh, HG_DK, HG_DK), F32),
                        c_p, 1, 2 if batch % 2 == 0 else 1, t=seq)
    c_s = SUBLANES * ((t_dec + SUBLANES - 1) // SUBLANES)
    pad = lambda a: jnp.pad(a[n_p:].reshape(n_seq, t_dec, d), ((0, 0), (0, c_s - t_dec), (0, 0)))
    o_s, s_s = _hg_core(pad(hq), pad(hk), pad(hlf), pad(hv), state_hgrn[0], c_s, 1,
                        next(s for s in (16, 8, 4, 2, 1) if n_seq % s == 0))
    o_s = o_s[:, :t_dec].reshape(n_s, d)

    wr = jnp.pad(w_router[0], ((0, 0), (0, LANES - n_exp)))
    br = jnp.pad(b_router[0:1], ((0, 0), (0, LANES - n_exp)))
    x3, ids, wts = _hg_out_router(o_p.reshape(n_p, d), o_s, hgate, x, hg_norm[0:1], w_o_hg[0], norm_ffn[1:2],
                                  wr, br, n_exp)
    y2 = _moe_experts(x3, norm_ffn[1:2], ids[:, :TOP_K], w_gu_moe[0], w_down_moe[0])
    y_p, y_s = _combine(x3, y2, wts, norm_final.reshape(1, d), n_p)

    return (y_p.reshape(batch, seq, d), y_s.reshape(n_seq, t_dec, d),
            k_win_p[None], v_win_p[None], k_win_s[None], v_win_s[None], s_p[None], s_s[None])
```
